```python
import math
import jax, jax.numpy as jnp
from jax import lax
import numpy as np

D_MODEL = 1024
BATCH = 16
SEQ = 2048
DEPTH = 2

N_MEM = 256
CONV_CH = D_MODEL // 2
CONV_WIDTH = 31
DIFF_HEADS = 4
DIFF_HEAD_DIM = 64
DIFF_V_DIM = 2 * DIFF_HEAD_DIM
DIFF_QK = DIFF_HEADS * 2 * DIFF_HEAD_DIM
DIFF_WIDTH = DIFF_HEADS * DIFF_V_DIM
IN_COLS = 2 * CONV_CH + 2 * DIFF_QK + DIFF_WIDTH
Q_BLOCK = 128
REL_BUCKETS = 32
REL_MAX_DIST = 128
RWKV_HEAD_DIM = 64
RWKV_HEADS = D_MODEL // RWKV_HEAD_DIM
DECAY_LORA = 64
ICLR_LORA = 64
GATE_LORA = 160
GN_EPS = 64e-5
XATTN_HEADS = 4
XATTN_HEAD_DIM = D_MODEL // XATTN_HEADS
FFN_HIDDEN = 4 * D_MODEL
N_EVEN = (DEPTH + 1) // 2
N_ODD = DEPTH // 2
NORM_EPS = 1e-6

kernel_name = 'hybrid_conv_diffattn_rwkv7_encoder'


def rms_norm(x, g):
    xf = x.astype(jnp.float32)
    y = xf * lax.rsqrt(jnp.mean(jnp.square(xf), axis=-1, keepdims=True) + NORM_EPS)
    return (y * g.astype(jnp.float32)).astype(x.dtype)


def layer_norm(x, g, b, eps):
    xf = x.astype(jnp.float32)
    mu = jnp.mean(xf, axis=-1, keepdims=True)
    var = jnp.mean(jnp.square(xf - mu), axis=-1, keepdims=True)
    y = (xf - mu) * lax.rsqrt(var + eps) * g.astype(jnp.float32) + b.astype(jnp.float32)
    return y.astype(x.dtype)


def t5_bucket(rel):
    nb = REL_BUCKETS // 2
    max_exact = nb // 2
    ret = jnp.where(rel > 0, nb, 0)
    n = jnp.abs(rel)
    nf = jnp.maximum(n, 1).astype(jnp.float32)
    large = max_exact + (jnp.log(nf / max_exact) / math.log(REL_MAX_DIST / max_exact)
                         * (nb - max_exact)).astype(jnp.int32)
    large = jnp.minimum(large, nb - 1)
    return ret + jnp.where(n < max_exact, n, large)


def diff_attention(q, k, v, rel_table, lam):
    B, T = q.shape[0], q.shape[1]
    nblk = T // Q_BLOCK
    scale = DIFF_HEAD_DIM ** -0.5
    qb = q.reshape(B, nblk, Q_BLOCK, DIFF_HEADS, 2, DIFF_HEAD_DIM).transpose(1, 0, 3, 4, 2, 5)
    kt = k.transpose(0, 2, 3, 1, 4)
    vt = v.transpose(0, 2, 1, 3)
    key_pos = jnp.arange(T, dtype=jnp.int32)

    def block(args):
        q_blk, start = args
        q_pos = start + jnp.arange(Q_BLOCK, dtype=jnp.int32)
        bias = rel_table[t5_bucket(key_pos[None, :] - q_pos[:, None])]
        bias = jnp.transpose(bias, (2, 0, 1)).astype(jnp.float32)
        s = jnp.einsum('bhcqd,bhckd->bhcqk', q_blk, kt).astype(jnp.float32) * scale + bias[None, :, None]
        p = jax.nn.softmax(s, axis=-1)
        a = (p[:, :, 0] - lam * p[:, :, 1]).astype(vt.dtype)
        return jnp.einsum('bhqk,bhkv->bhqv', a, vt)

    starts = jnp.arange(nblk, dtype=jnp.int32) * Q_BLOCK
    o = lax.map(block, (qb, starts))
    return o.transpose(1, 0, 3, 2, 4).reshape(B, T, DIFF_HEADS, DIFF_V_DIM)


def conv_diff_mixer(h, w_in, w_out, conv_w, conv_b, ln_g, ln_b, lq1, lk1, lq2, lk2, subln_g, rel_table, layer_idx):
    B, T, _ = h.shape
    proj = h @ w_in
    u, q, k, v = jnp.split(proj, [2 * CONV_CH, 2 * CONV_CH + DIFF_QK, 2 * CONV_CH + 2 * DIFF_QK], axis=-1)
    u = u[..., :CONV_CH] * jax.nn.sigmoid(u[..., CONV_CH:])
    pad = CONV_WIDTH // 2
    u = lax.conv_general_dilated(u, conv_w[:, None, :].astype(u.dtype), (1,), [(pad, pad)],
                                 dimension_numbers=('NWC', 'WIO', 'NWC'),
                                 feature_group_count=CONV_CH) + conv_b
    u = jax.nn.silu(layer_norm(u, ln_g, ln_b, 1e-5))
    lam_init = 0.8 - 0.6 * math.exp(-0.3 * layer_idx)
    lam = (jnp.exp(jnp.sum(lq1.astype(jnp.float32) * lk1.astype(jnp.float32)))
           - jnp.exp(jnp.sum(lq2.astype(jnp.float32) * lk2.astype(jnp.float32))) + lam_init)
    o = diff_attention(q.reshape(B, T, DIFF_HEADS, 2, DIFF_HEAD_DIM),
                       k.reshape(B, T, DIFF_HEADS, 2, DIFF_HEAD_DIM),
                       v.reshape(B, T, DIFF_HEADS, DIFF_V_DIM), rel_table, lam)
    o = rms_norm(o, subln_g) * (1.0 - lam_init)
    mixed = jnp.concatenate([u, o.reshape(B, T, DIFF_WIDTH).astype(u.dtype)], axis=-1)
    return mixed @ w_out


def wkv7_scan(r, w, k, v, a, b, reverse):
    B, T, H, N = r.shape
    xs = tuple(jnp.moveaxis(t.astype(jnp.float32), 1, 0) for t in (r, w, k, v, a, b))

    def step(S, inp):
        r_t, w_t, k_t, v_t, a_t, b_t = inp
        sa = jnp.einsum('bhij,bhj->bhi', S, a_t)
        S = S * w_t[:, :, None, :] + sa[..., None] * b_t[:, :, None, :] + v_t[..., None] * k_t[:, :, None, :]
        return S, jnp.einsum('bhij,bhj->bhi', S, r_t)

    S0 = jnp.zeros((B, H, N, N), jnp.float32)
    _, ys = lax.scan(step, S0, xs, reverse=reverse)
    return jnp.moveaxis(ys, 0, 1)


def rwkv7_mixer(h, mu, w_r, w_k, w_v, w_o, w0, w1, w2, a0, a1, a2, g1, g2, k_k, k_a, r_k, ln_g, ln_b):
    B, T, D = h.shape
    H, N = RWKV_HEADS, RWKV_HEAD_DIM
    f32 = jnp.float32
    zero = jnp.zeros_like(h[:, :1])
    h_prev = jnp.concatenate([zero, h[:, :-1]], axis=1)
    h_next = jnp.concatenate([h[:, 1:], zero], axis=1)
    hh = 0.5 * (h_prev + h_next) - h
    xr, xw, xk, xv, xa, xg = [h + hh * mu[i] for i in range(6)]
    r = (xr @ w_r).reshape(B, T, H, N)
    k = (xk @ w_k).reshape(B, T, H, N)
    v = (xv @ w_v).reshape(B, T, H, N)
    g = jax.nn.sigmoid(xg @ g1) @ g2
    w_pre = w0[:, None, None, :] + jnp.einsum('zbtr,zrd->zbtd', jnp.tanh(jnp.einsum('btd,zdr->zbtr', xw, w1)), w2)
    w_pre = w_pre.astype(f32)
    decay = jnp.exp(-jnp.exp(-jax.nn.softplus(-w_pre) - 0.5)).reshape(2, B, T, H, N)
    a_pre = a0[:, None, None, :] + jnp.einsum('zbtr,zrd->zbtd', jnp.einsum('btd,zdr->zbtr', xa, a1), a2)
    a_rate = jax.nn.sigmoid(a_pre.astype(f32)).reshape(2, B, T, H, N)
    kf = k.astype(f32)
    kk = kf * k_k.reshape(H, N).astype(f32)
    kk = kk / jnp.maximum(jnp.sqrt(jnp.sum(jnp.square(kk), axis=-1, keepdims=True)), 1e-12)
    k_dir = kf[None] * (1.0 + (a_rate - 1.0) * k_a.reshape(H, N).astype(f32))
    y_f = wkv7_scan(r, decay[0], k_dir[0], v, -kk, kk * a_rate[0], reverse=False)
    y_b = wkv7_scan(r, decay[1], k_dir[1], v, -kk, kk * a_rate[1], reverse=True)
    y = layer_norm(y_f + y_b, ln_g.reshape(H, N), ln_b.reshape(H, N), GN_EPS)
    bonus = jnp.sum(r.astype(f32)[None] * k_dir * r_k.astype(f32), axis=(0, 4))[..., None] * v.astype(f32)
    out = (y + bonus).reshape(B, T, D).astype(h.dtype) * g
    return out @ w_o


def cross_attention(h, m, w_q, w_kv, w_o):
    B, T, D = h.shape
    M = m.shape[1]
    q = (h @ w_q).reshape(B, T, XATTN_HEADS, XATTN_HEAD_DIM)
    kv = (m @ w_kv).reshape(B, M, 2, XATTN_HEADS, XATTN_HEAD_DIM)
    k, v = kv[:, :, 0], kv[:, :, 1]
    s = jnp.einsum('bqhd,bkhd->bhqk', q, k).astype(jnp.float32) * (XATTN_HEAD_DIM ** -0.5)
    p = jax.nn.softmax(s, axis=-1).astype(v.dtype)
    o = jnp.einsum('bhqk,bkhd->bqhd', p, v).reshape(B, T, D)
    return o @ w_o


def squared_relu_mlp(h, w_up, w_down):
    return jnp.square(jax.nn.relu(h @ w_up)) @ w_down


def setup_inputs(seed: int = 0) -> dict:
    key = jax.random.key(seed)
    ks = iter(jax.random.split(key, 64))
    D = D_MODEL

    def nrm(shape, scale):
        return scale * jax.random.normal(next(ks), shape, jnp.float32)

    def gain(shape):
        return 1.0 + nrm(shape, 0.02)

    def unif(shape, lo, hi):
        return jax.random.uniform(next(ks), shape, jnp.float32, lo, hi)

    return {
        'x': nrm((BATCH, SEQ, D), 1.0),
        'mem': nrm((BATCH, N_MEM, D), 1.0),
        'rel_bias_table': nrm((REL_BUCKETS, DIFF_HEADS), 0.5),
        'norm_mix': gain((DEPTH, D)),
        'norm_xattn': gain((DEPTH, D)),
        'norm_mem': gain((DEPTH, D)),
        'norm_ffn': gain((DEPTH, D)),
        'norm_final': gain((D,)),
        'ab_w_in': nrm((N_EVEN, D, IN_COLS), D ** -0.5),
        'ab_w_out': nrm((N_EVEN, CONV_CH + DIFF_WIDTH, D), (CONV_CH + DIFF_WIDTH) ** -0.5),
        'conv_w': nrm((N_EVEN, CONV_WIDTH, CONV_CH), CONV_WIDTH ** -0.5),
        'conv_b': nrm((N_EVEN, CONV_CH), 0.02),
        'conv_ln_g': gain((N_EVEN, CONV_CH)),
        'conv_ln_b': nrm((N_EVEN, CONV_CH), 0.02),
        'diff_lq1': nrm((N_EVEN, DIFF_HEAD_DIM), 0.1),
        'diff_lk1': nrm((N_EVEN, DIFF_HEAD_DIM), 0.1),
        'diff_lq2': nrm((N_EVEN, DIFF_HEAD_DIM), 0.1),
        'diff_lk2': nrm((N_EVEN, DIFF_HEAD_DIM), 0.1),
        'diff_subln_g': gain((N_EVEN, DIFF_V_DIM)),
        'rwkv_mu': unif((N_ODD, 6, D), 0.0, 1.0),
        'rwkv_w_r': nrm((N_ODD, D, D), D ** -0.5),
        'rwkv_w_k': nrm((N_ODD, D, D), D ** -0.5),
        'rwkv_w_v': nrm((N_ODD, D, D), D ** -0.5),
        'rwkv_w_o': nrm((N_ODD, D, D), D ** -0.5),
        'rwkv_w0': unif((N_ODD, 2, D), -4.0, 1.0),
        'rwkv_w1': nrm((N_ODD, 2, D, DECAY_LORA), D ** -0.5),
        'rwkv_w2': nrm((N_ODD, 2, DECAY_LORA, D), 0.1 * DECAY_LORA ** -0.5),
        'rwkv_a0': nrm((N_ODD, 2, D), 0.1),
        'rwkv_a1': nrm((N_ODD, 2, D, ICLR_LORA), D ** -0.5),
        'rwkv_a2': nrm((N_ODD, 2, ICLR_LORA, D), 0.5 * ICLR_LORA ** -0.5),
        'rwkv_g1': nrm((N_ODD, D, GATE_LORA), D ** -0.5),
        'rwkv_g2': nrm((N_ODD, GATE_LORA, D), GATE_LORA ** -0.5),
        'rwkv_k_k': 0.85 + nrm((N_ODD, D), 0.02),
        'rwkv_k_a': gain((N_ODD, D)),
        'rwkv_r_k': nrm((N_ODD, RWKV_HEADS, RWKV_HEAD_DIM), 0.1),
        'rwkv_ln_g': gain((N_ODD, D)),
        'rwkv_ln_b': nrm((N_ODD, D), 0.02),
        'xattn_w_q': nrm((DEPTH, D, D), D ** -0.5),
        'xattn_w_kv': nrm((DEPTH, D, 2 * D), D ** -0.5),
        'xattn_w_o': nrm((DEPTH, D, D), D ** -0.5),
        'ffn_w_up': nrm((DEPTH, D, FFN_HIDDEN), D ** -0.5),
        'ffn_w_down': nrm((DEPTH, FFN_HIDDEN, D), FFN_HIDDEN ** -0.5),
    }


def reference(x, mem, rel_bias_table, norm_mix, norm_xattn, norm_mem, norm_ffn, norm_final,
              ab_w_in, ab_w_out, conv_w, conv_b, conv_ln_g, conv_ln_b,
              diff_lq1, diff_lk1, diff_lq2, diff_lk2, diff_subln_g,
              rwkv_mu, rwkv_w_r, rwkv_w_k, rwkv_w_v, rwkv_w_o, rwkv_w0, rwkv_w1, rwkv_w2,
              rwkv_a0, rwkv_a1, rwkv_a2, rwkv_g1, rwkv_g2, rwkv_k_k, rwkv_k_a, rwkv_r_k,
              rwkv_ln_g, rwkv_ln_b, xattn_w_q, xattn_w_kv, xattn_w_o, ffn_w_up, ffn_w_down):
    h = x
    for i in range(DEPTH):
        hn = rms_norm(h, norm_mix[i])
        j = i // 2
        if i % 2 == 0:
            h = h + conv_diff_mixer(hn, ab_w_in[j], ab_w_out[j], conv_w[j], conv_b[j], conv_ln_g[j], conv_ln_b[j],
                                    diff_lq1[j], diff_lk1[j], diff_lq2[j], diff_lk2[j], diff_subln_g[j],
                                    rel_bias_table, i)
        else:
            h = h + rwkv7_mixer(hn, rwkv_mu[j], rwkv_w_r[j], rwkv_w_k[j], rwkv_w_v[j], rwkv_w_o[j],
                                rwkv_w0[j], rwkv_w1[j], rwkv_w2[j], rwkv_a0[j], rwkv_a1[j], rwkv_a2[j],
                                rwkv_g1[j], rwkv_g2[j], rwkv_k_k[j], rwkv_k_a[j], rwkv_r_k[j],
                                rwkv_ln_g[j], rwkv_ln_b[j])
        h = h + cross_attention(rms_norm(h, norm_xattn[i]), rms_norm(mem, norm_mem[i]),
                                xattn_w_q[i], xattn_w_kv[i], xattn_w_o[i])
        h = h + squared_relu_mlp(rms_norm(h, norm_ffn[i]), ffn_w_up[i], ffn_w_down[i])
    return rms_norm(h, norm_final)
```

```python
import functools
import math

import jax
import jax.numpy as jnp
from jax import lax
from jax.experimental import pallas as pl
from jax.experimental.pallas import tpu as pltpu

F32 = jnp.float32
BF16 = jnp.bfloat16

V7X_VMEM_BYTES = 64 * 1024 * 1024
VMEM_LIMIT_BYTES = V7X_VMEM_BYTES - 8 * 1024 * 1024

NORM_EPS = 1e-6
CONV_LN_EPS = 1e-5
GN_EPS = 64e-5
CONV_WIDTH = 31
CONV_PAD = CONV_WIDTH // 2
CONV_HALO = 16
DIFF_HEADS = 4
DIFF_HEAD_DIM = 64
REL_BUCKETS = 32
REL_MAX_DIST = 128
XATTN_HEADS = 4
RWKV_HEAD_DIM = 64
CHUNK = 64
PAIR = 2 * RWKV_HEAD_DIM


def _cparams(*sem):
    return pltpu.CompilerParams(dimension_semantics=sem, vmem_limit_bytes=VMEM_LIMIT_BYTES)


def _resident(shape):
    nd = len(shape)
    return pl.BlockSpec(shape, lambda *_: (0,) * nd, pipeline_mode=pl.Buffered(1))


def _rms(x, g):
    ms = jnp.mean(x * x, axis=-1, keepdims=True)
    return x * lax.rsqrt(ms + NORM_EPS) * g


def _sigmoid(x):
    return 1.0 / (1.0 + jnp.exp(-x))


def _dot(a, b):
    return jnp.dot(a.astype(BF16), b.astype(BF16), preferred_element_type=F32)


def _dot_nt(a, b):
    return lax.dot_general(a.astype(BF16), b.astype(BF16), (((1,), (1,)), ((), ())),
                           preferred_element_type=F32)


def _inproj_kernel(h_ref, g_ref, w_ref, u_ref, q_ref, k_ref, v_ref, *, cc, qk, scale):
    xn = _rms(h_ref[...], g_ref[...])
    p = _dot(xn, w_ref[...])
    u_ref[...] = p[:, :cc] * _sigmoid(p[:, cc:2 * cc])
    o = 2 * cc
    q_ref[...] = (p[:, o:o + qk] * scale).astype(BF16)
    k_ref[...] = p[:, o + qk:o + 2 * qk].astype(BF16)
    v_ref[...] = p[:, o + 2 * qk:].astype(BF16)


def _inproj(h, g, w, cc, qk, vw, tm=512):
    n, d = h.shape
    row = lambda i: (i, 0)
    return pl.pallas_call(
        functools.partial(_inproj_kernel, cc=cc, qk=qk, scale=DIFF_HEAD_DIM ** -0.5),
        grid=(n // tm,),
        in_specs=[pl.BlockSpec((tm, d), row), _resident((1, d)), _resident(w.shape)],
        out_specs=[pl.BlockSpec((tm, cc), row), pl.BlockSpec((tm, qk), row),
                   pl.BlockSpec((tm, qk), row), pl.BlockSpec((tm, vw), row)],
        out_shape=[jax.ShapeDtypeStruct((n, cc), F32), jax.ShapeDtypeStruct((n, qk), BF16),
                   jax.ShapeDtypeStruct((n, qk), BF16), jax.ShapeDtypeStruct((n, vw), BF16)],
        compiler_params=_cparams("parallel"),
        name="l0_inproj",
    )(h, g, w)


def _conv_kernel(u_ref, w_ref, b_ref, g_ref, beta_ref, o_ref, xp_ref, *, t, rows):
    c = u_ref.shape[-1]
    zeros = jnp.zeros((CONV_HALO, c), F32)
    xp_ref[0:CONV_HALO, :] = zeros
    xp_ref[CONV_HALO + t:, :] = zeros
    xp_ref[CONV_HALO:CONV_HALO + t, :] = u_ref[0]
    off = CONV_HALO - CONV_PAD

    def body(i, carry):
        base = pl.multiple_of(i * rows, rows)
        win = xp_ref[pl.ds(base, rows + 2 * CONV_HALO), :]
        acc = jnp.zeros((rows, c), F32)
        for k in range(CONV_WIDTH):
            acc = acc + win[off + k:off + k + rows, :] * w_ref[k:k + 1, :]
        y = acc + b_ref[...]
        mu = jnp.mean(y, axis=-1, keepdims=True)
        yc = y - mu
        var = jnp.mean(yc * yc, axis=-1, keepdims=True)
        yn = yc * lax.rsqrt(var + CONV_LN_EPS) * g_ref[...] + beta_ref[...]
        o_ref[0, pl.ds(base, rows), :] = (yn * _sigmoid(yn)).astype(BF16)
        return carry

    lax.fori_loop(0, t // rows, body, 0)


def _conv_module(u, w, b, g, beta, rows=64):
    bsz, t, c = u.shape
    return pl.pallas_call(
        functools.partial(_conv_kernel, t=t, rows=rows),
        grid=(bsz,),
        in_specs=[pl.BlockSpec((1, t, c), lambda i: (i, 0, 0)), _resident(w.shape),
                  _resident((1, c)), _resident((1, c)), _resident((1, c))],
        out_specs=pl.BlockSpec((1, t, c), lambda i: (i, 0, 0)),
        out_shape=jax.ShapeDtypeStruct((bsz, t, c), BF16),
        scratch_shapes=[pltpu.VMEM((t + 2 * CONV_HALO, c), F32)],
        compiler_params=_cparams("parallel"),
        name="l0_conv",
    )(u, w, b, g, beta)


def _t5_bucket(rel):
    nb = REL_BUCKETS // 2
    max_exact = nb // 2
    n = jnp.abs(rel)
    large = jnp.full(rel.shape, max_exact, jnp.int32)
    steps = nb - max_exact
    for m in range(1, steps):
        thr = math.ceil(max_exact * (REL_MAX_DIST / max_exact) ** (m / steps) - 1e-9)
        large = large + jnp.where(n >= thr, 1, 0)
    mag = jnp.where(n < max_exact, n, large)
    return mag + jnp.where(rel > 0, nb, 0)


def _diffattn_kernel(tbl_ref, lq_ref, sg_ref, q_ref, k_ref, v_ref, o_ref, bias_ref, *, tq, lam_init):
    h = pl.program_id(0)
    qi = pl.program_id(1)
    b = pl.program_id(2)
    t = k_ref.shape[1]

    @pl.when(b == 0)
    def _():
        qpos = qi * tq + lax.broadcasted_iota(jnp.int32, (tq, t), 0)
        kpos = lax.broadcasted_iota(jnp.int32, (tq, t), 1)
        bucket = _t5_bucket(kpos - qpos)
        bias = jnp.zeros((tq, t), F32)
        for i in range(REL_BUCKETS):
            bias = jnp.where(bucket == i, tbl_ref[i * DIFF_HEADS + h], bias)
        bias_ref[...] = bias

    lq = lq_ref[...]
    lam = (jnp.exp(jnp.sum(lq[0:1] * lq[1:2], axis=-1, keepdims=True))
           - jnp.exp(jnp.sum(lq[2:3] * lq[3:4], axis=-1, keepdims=True)) + lam_init)
    q = q_ref[0]
    k = k_ref[0]
    v = v_ref[0]
    first = lax.broadcasted_iota(jnp.int32, q.shape, 1) < DIFF_HEAD_DIM
    zero = jnp.zeros_like(q)

    def component(qc):
        s = _dot_nt(qc, k) + bias_ref[...]
        m = jnp.max(s, axis=-1, keepdims=True)
        e = jnp.exp(s - m)
        l = jnp.sum(e, axis=-1, keepdims=True)
        return _dot(e, v) / l

    o = component(jnp.where(first, q, zero)) - lam * component(jnp.where(first, zero, q))
    o = o * lax.rsqrt(jnp.mean(o * o, axis=-1, keepdims=True) + NORM_EPS) * sg_ref[...] * (1.0 - lam_init)
    o_ref[0] = o.astype(BF16)


def _diff_attention(q, k, v, tbl, lq, sg, lam_init, tq=256):
    bsz, t, _ = q.shape
    hw = 2 * DIFF_HEAD_DIM
    return pl.pallas_call(
        functools.partial(_diffattn_kernel, tq=tq, lam_init=lam_init),
        grid=(DIFF_HEADS, t // tq, bsz),
        in_specs=[pl.BlockSpec(memory_space=pltpu.SMEM), _resident(lq.shape), _resident(sg.shape),
                  pl.BlockSpec((1, tq, hw), lambda h, i, b: (b, i, h)),
                  pl.BlockSpec((1, t, hw), lambda h, i, b: (b, 0, h)),
                  pl.BlockSpec((1, t, hw), lambda h, i, b: (b, 0, h))],
        out_specs=pl.BlockSpec((1, tq, hw), lambda h, i, b: (b, i, h)),
        out_shape=jax.ShapeDtypeStruct(q.shape, BF16),
        scratch_shapes=[pltpu.VMEM((tq, t), F32)],
        compiler_params=_cparams("parallel", "parallel", "arbitrary"),
        name="l0_diffattn",
    )(tbl, lq, sg, q, k, v)


def _outproj_kernel(u_ref, o_ref, w_ref, h_ref, out_ref):
    cc = u_ref.shape[-1]
    out_ref[...] = (h_ref[...] + _dot(u_ref[...], w_ref[:cc, :]) + _dot(o_ref[...], w_ref[cc:, :]))


def _outproj(u, o, w, h, tm=512):
    n, d = h.shape
    row = lambda i: (i, 0)
    return pl.pallas_call(
        _outproj_kernel,
        grid=(n // tm,),
        in_specs=[pl.BlockSpec((tm, u.shape[1]), row), pl.BlockSpec((tm, o.shape[1]), row),
                  _resident(w.shape), pl.BlockSpec((tm, d), row)],
        out_specs=pl.BlockSpec((tm, d), row),
        out_shape=jax.ShapeDtypeStruct((n, d), F32),
        compiler_params=_cparams("parallel"),
        name="l0_outproj",
    )(u, o, w, h)


def _norm_linear_kernel(x_ref, g_ref, w_ref, o_ref):
    o_ref[...] = _dot(_rms(x_ref[...], g_ref[...]), w_ref[...]).astype(o_ref.dtype)


def _norm_linear(x, g, w, out_dtype, tm=512):
    n, d = x.shape
    m = w.shape[1]
    return pl.pallas_call(
        _norm_linear_kernel,
        grid=(n // tm,),
        in_specs=[pl.BlockSpec((tm, d), lambda i: (i, 0)), _resident((1, d)), _resident(w.shape)],
        out_specs=pl.BlockSpec((tm, m), lambda i: (i, 0)),
        out_shape=jax.ShapeDtypeStruct((n, m), out_dtype),
        compiler_params=_cparams("parallel"),
        name="norm_linear",
    )(x, g, w)


def _xattn_kernel(h_ref, g_ref, wq_ref, kv_ref, wo_ref, o_ref):
    d = h_ref.shape[-1]
    hd = d // XATTN_HEADS
    h = h_ref[0]
    q = (_dot(_rms(h, g_ref[...]), wq_ref[...]) * (hd ** -0.5)).astype(BF16)
    outs = []
    for i in range(XATTN_HEADS):
        kh = kv_ref[0, :, i * hd:(i + 1) * hd]
        vh = kv_ref[0, :, d + i * hd:d + (i + 1) * hd]
        s = _dot_nt(q[:, i * hd:(i + 1) * hd], kh)
        m = jnp.max(s, axis=-1, keepdims=True)
        e = jnp.exp(s - m)
        p = e / jnp.sum(e, axis=-1, keepdims=True)
        outs.append(_dot(p, vh).astype(BF16))
    o_ref[0] = h + _dot(jnp.concatenate(outs, axis=1), wo_ref[...])


def _cross_attention(h, kv, g, wq, wo, tq=512):
    bsz, t, d = h.shape
    m = kv.shape[1]
    return pl.pallas_call(
        _xattn_kernel,
        grid=(bsz, t // tq),
        in_specs=[pl.BlockSpec((1, tq, d), lambda b, i: (b, i, 0)), _resident((1, d)), _resident(wq.shape),
                  pl.BlockSpec((1, m, 2 * d), lambda b, i: (b, 0, 0)), _resident(wo.shape)],
        out_specs=pl.BlockSpec((1, tq, d), lambda b, i: (b, i, 0)),
        out_shape=jax.ShapeDtypeStruct(h.shape, F32),
        compiler_params=_cparams("parallel", "parallel"),
        name="xattn",
    )(h, g, wq, kv, wo)


def _mlp_kernel(h_ref, g_ref, wu_ref, wd_ref, gf_ref, o_ref, *, hc, final_norm):
    h = h_ref[...]
    xn = _rms(h, g_ref[...]).astype(BF16)
    acc = h
    for c in range(wu_ref.shape[1] // hc):
        a = jnp.maximum(_dot(xn, wu_ref[:, c * hc:(c + 1) * hc]), 0.0)
        acc = acc + _dot(a * a, wd_ref[c * hc:(c + 1) * hc, :])
    if final_norm:
        acc = _rms(acc, gf_ref[...])
    o_ref[...] = acc


def _mlp(h, g, wu, wd, gf, final_norm, tm=512, hc=1024):
    n, d = h.shape
    row = lambda i: (i, 0)
    return pl.pallas_call(
        functools.partial(_mlp_kernel, hc=hc, final_norm=final_norm),
        grid=(n // tm,),
        in_specs=[pl.BlockSpec((tm, d), row), _resident((1, d)), _resident(wu.shape), _resident(wd.shape),
                  _resident((1, d))],
        out_specs=pl.BlockSpec((tm, d), row),
        out_shape=jax.ShapeDtypeStruct((n, d), F32),
        compiler_params=_cparams("parallel"),
        name="mlp",
    )(h, g, wu, wd, gf)


def _head_sum(x, sel_ref, selt_ref):
    return _dot(_dot(x, sel_ref[...]), selt_ref[...])


def _rwkv_prep_kernel(h_ref, hp_ref, hn_ref, g_ref, mu_ref, wr_ref, wk_ref, wv_ref, w1_ref, w2_ref, w0_ref,
                      a1_ref, a2_ref, a0_ref, g1_ref, g2_ref, kk_ref, ka_ref, rk_ref, sel_ref, selt_ref,
                      r_out, v_out, kn_out, gate_out, bonus_out, kd_out, lw_out, b_out):
    i = pl.program_id(1)
    last = pl.num_programs(1) - 1
    tm = h_ref.shape[1]
    g = g_ref[...]
    x = _rms(h_ref[0], g)
    prev_row = _rms(hp_ref[0], g)[7:8, :] * jnp.where(i > 0, 1.0, 0.0)
    next_row = _rms(hn_ref[0], g)[0:1, :] * jnp.where(i < last, 1.0, 0.0)
    rowid = lax.broadcasted_iota(jnp.int32, x.shape, 0)
    x_prev = jnp.where(rowid == 0, prev_row, pltpu.roll(x, 1, 0))
    x_next = jnp.where(rowid == tm - 1, next_row, pltpu.roll(x, tm - 1, 0))
    hh = 0.5 * (x_prev + x_next) - x
    xr, xw, xk, xv, xa, xg = [x + hh * mu_ref[j:j + 1, :] for j in range(6)]
    r = _dot(xr, wr_ref[...])
    k = _dot(xk, wk_ref[...])
    v = _dot(xv, wv_ref[...])
    gate_out[0] = _dot(_sigmoid(_dot(xg, g1_ref[...])), g2_ref[...])
    lw = jnp.tanh(_dot(xw, w1_ref[...]))
    la = _dot(xa, a1_ref[...])
    kk = k * kk_ref[...]
    kn = kk / jnp.maximum(jnp.sqrt(_head_sum(kk * kk, sel_ref, selt_ref)), 1e-12)
    r_out[0] = r
    v_out[0] = v
    kn_out[0] = kn
    kd_sum = jnp.zeros_like(k)
    for z in range(2):
        w_pre = w0_ref[z:z + 1, :] + _dot(lw, w2_ref[z])
        softplus = jnp.maximum(-w_pre, 0.0) + jnp.log(1.0 + jnp.exp(-jnp.abs(w_pre)))
        lw_out[z, 0] = -jnp.exp(-softplus - 0.5)
        rate = _sigmoid(a0_ref[z:z + 1, :] + _dot(la, a2_ref[z]))
        kd = k * (1.0 + (rate - 1.0) * ka_ref[...])
        kd_out[z, 0] = kd
        b_out[z, 0] = kn * rate
        kd_sum = kd_sum + kd
    bonus_out[0] = _head_sum(r * kd_sum * rk_ref[...], sel_ref, selt_ref) * v


def _rwkv_prep(h, g, mu, wr, wk, wv, w1, w2, w0, a1, a2, a0, g1, g2, kk, ka, rk, sel, selt, tm=256):
    bsz, t, d = h.shape
    nb = tm // 8
    tile = pl.BlockSpec((1, tm, d), lambda b, i: (b, i, 0))
    tile2 = pl.BlockSpec((2, 1, tm, d), lambda b, i: (0, b, i, 0))
    one = jax.ShapeDtypeStruct((bsz, t, d), F32)
    two = jax.ShapeDtypeStruct((2, bsz, t, d), F32)
    consts = [g, mu, wr, wk, wv, w1, w2, w0, a1, a2, a0, g1, g2, kk, ka, rk, sel, selt]
    return pl.pallas_call(
        _rwkv_prep_kernel,
        grid=(bsz, t // tm),
        in_specs=[tile,
                  pl.BlockSpec((1, 8, d), lambda b, i: (b, jnp.maximum(i * nb - 1, 0), 0)),
                  pl.BlockSpec((1, 8, d), lambda b, i: (b, jnp.minimum((i + 1) * nb, t // 8 - 1), 0))]
                 + [_resident(c.shape) for c in consts],
        out_specs=[tile, tile, tile, tile, tile, tile2, tile2, tile2],
        out_shape=[one, one, one, one, one, two, two, two],
        compiler_params=_cparams("parallel", "parallel"),
        name="l1_rwkv_prep",
    )(h, h, h, *consts)


def _blockdiag(x, head0):
    zero = jnp.zeros_like(x)
    return jnp.concatenate([jnp.where(head0, x, zero), jnp.where(head0, zero, x)], axis=0)


def _split_dot(mat, x):
    hi = x.astype(BF16)
    lo = (x - hi.astype(F32)).astype(BF16)
    return (jnp.dot(mat, hi, preferred_element_type=F32) + jnp.dot(mat, lo, preferred_element_type=F32))


def _chunk_local(r, k, v, kn, logw, b, sign):
    c = CHUNK
    row = lax.broadcasted_iota(jnp.int32, (c, PAIR), 0)
    lane = lax.broadcasted_iota(jnp.int32, (c, PAIR), 1)
    s_idx = lane & (c - 1)
    head0 = lane < c
    ti = lax.broadcasted_iota(jnp.int32, (c, c), 0)
    si = lax.broadcasted_iota(jnp.int32, (c, c), 1)
    before_incl = jnp.where((si - ti) * sign <= 0, 1.0, 0.0).astype(BF16)
    cum = _split_dot(before_incl, logw)
    tot = _split_dot(jnp.ones((c, c), BF16), logw)
    w_incl = jnp.exp(cum)
    w_excl = jnp.exp(cum - logw)
    w_inv = jnp.exp(-cum)
    w_tot = jnp.exp(tot)
    w_rest = jnp.exp(tot - cum)
    a_t = -kn * w_excl
    r_t = r * w_incl
    b_t = b * w_inv
    k_t = k * w_inv
    b_h = b * w_rest
    k_h = k * w_rest
    d = (s_idx - row) * sign
    strict = d < 0
    incl = d <= 0
    same_blk = (s_idx >> 4) == (row >> 4)
    bd = functools.partial(_blockdiag, head0=head0)

    sc = _dot_nt(jnp.concatenate([a_t, r_t], axis=0), jnp.concatenate([bd(b_t), bd(k_t)], axis=0))
    p_ab = jnp.where(strict, sc[:c, :PAIR], 0.0)
    p_ak = jnp.where(strict, sc[:c, PAIR:], 0.0)
    p_rb = jnp.where(incl, sc[c:, :PAIR], 0.0)
    p_rk = jnp.where(incl, sc[c:, PAIR:], 0.0)
    dm = jnp.where(same_blk, p_ab, 0.0)
    em = p_ab - dm
    eye = jnp.where(s_idx == row, 1.0, 0.0)
    x2 = _dot(dm, bd(dm))
    x4 = _dot(x2, bd(x2))
    x8 = _dot(x4, bd(x4))
    tm = eye + dm
    tm = tm + _dot(tm, bd(x2))
    tm = tm + _dot(tm, bd(x4))
    tm = tm + _dot(tm, bd(x8))
    f1 = _dot(tm, bd(em))
    f2 = _dot(f1, bd(f1))
    g1 = eye + f1
    gm = g1 + _dot(g1, bd(f2))
    tf = _dot(gm, bd(tm))
    av = _dot(p_ak, bd(v))
    au = _dot(tf, jnp.concatenate([bd(a_t), bd(av)], axis=1))
    abar, uloc = au[:, :PAIR], au[:, PAIR:]
    rhs = jnp.concatenate([jnp.concatenate([bd(abar), bd(uloc)], axis=1),
                           jnp.concatenate([jnp.zeros((PAIR, PAIR), F32), bd(v)], axis=1)], axis=0)
    ry = _dot(jnp.concatenate([p_rb, p_rk], axis=1), rhs)
    rbar = r_t + ry[:, :PAIR]
    yloc = ry[:, PAIR:]
    rhs2 = jnp.concatenate([jnp.concatenate([abar, uloc], axis=1),
                            jnp.concatenate([jnp.zeros((c, PAIR), F32), v], axis=1)], axis=0)
    lhs2 = jnp.concatenate([b_h, k_h], axis=0)
    mg = _dot(lhs2.T, rhs2)
    r2 = lax.broadcasted_iota(jnp.int32, (PAIR, PAIR), 0)
    l2 = lax.broadcasted_iota(jnp.int32, (PAIR, PAIR), 1)
    same_head = (r2 >> 6) == (l2 >> 6)
    w_tot2 = jnp.concatenate([w_tot, w_tot], axis=0)
    m = jnp.where(same_head, mg[:, :PAIR], 0.0) + jnp.where(r2 == l2, w_tot2, 0.0)
    gg = jnp.where(same_head, mg[:, PAIR:], 0.0)
    return rbar, yloc, m, gg


def _scan_kernel(r_ref, v_ref, kn_ref, kd_ref, lw_ref, b_ref, y_ref, ds_ref, *, unroll):
    z = pl.program_id(2)
    nc = r_ref.shape[1] // CHUNK
    sign = 1 - 2 * z
    ds_ref[...] = jnp.zeros(ds_ref.shape, F32)

    def body(it, carry):
        slices = []
        local = []
        for u in range(unroll):
            cidx = it * unroll + u
            cidx = jnp.where(z == 0, cidx, nc - 1 - cidx)
            sl = pl.ds(pl.multiple_of(cidx * CHUNK, CHUNK), CHUNK)
            slices.append(sl)
            local.append(_chunk_local(r_ref[0, sl, :], kd_ref[0, 0, sl, :], v_ref[0, sl, :], kn_ref[0, sl, :],
                                      lw_ref[0, 0, sl, :], b_ref[0, 0, sl, :], sign))
        ds = ds_ref[...]
        for sl, (rbar, yloc, m, gg) in zip(slices, local):
            y = _dot(rbar, ds) + yloc
            ds = _dot(m, ds) + gg

            @pl.when(z == 0)
            def _():
                y_ref[0, sl, :] = y

            @pl.when(z != 0)
            def _():
                y_ref[0, sl, :] = y_ref[0, sl, :] + y

        ds_ref[...] = ds
        return carry

    lax.fori_loop(0, nc // unroll, body, 0)


def _wkv7_scan(r, v, kn, kd, lw, b, unroll=2):
    bsz, t, d = r.shape
    one = pl.BlockSpec((1, t, PAIR), lambda bb, p, z: (bb, 0, p))
    two = pl.BlockSpec((1, 1, t, PAIR), lambda bb, p, z: (z, bb, 0, p))
    return pl.pallas_call(
        functools.partial(_scan_kernel, unroll=unroll),
        grid=(bsz, d // PAIR, 2),
        in_specs=[one, one, one, two, two, two],
        out_specs=one,
        out_shape=jax.ShapeDtypeStruct((bsz, t, d), F32),
        scratch_shapes=[pltpu.VMEM((PAIR, PAIR), F32)],
        compiler_params=_cparams("parallel", "parallel", "arbitrary"),
        name="l1_wkv7_scan",
    )(r, v, kn, kd, lw, b)


def _rwkv_post_kernel(y_ref, bonus_ref, gate_ref, h_ref, lg_ref, lb_ref, wo_ref, sel_ref, selt_ref, o_ref):
    y = y_ref[...]
    inv_n = 1.0 / RWKV_HEAD_DIM
    mu = _head_sum(y, sel_ref, selt_ref) * inv_n
    yc = y - mu
    var = _head_sum(yc * yc, sel_ref, selt_ref) * inv_n
    yn = yc * lax.rsqrt(var + GN_EPS) * lg_ref[...] + lb_ref[...]
    o_ref[...] = h_ref[...] + _dot((yn + bonus_ref[...]) * gate_ref[...], wo_ref[...])


def _rwkv_post(y, bonus, gate, h, lg, lb, wo, sel, selt, tm=512):
    n, d = h.shape
    row = pl.BlockSpec((tm, d), lambda i: (i, 0))
    return pl.pallas_call(
        _rwkv_post_kernel,
        grid=(n // tm,),
        in_specs=[row, row, row, row, _resident((1, d)), _resident((1, d)), _resident(wo.shape),
                  _resident(sel.shape), _resident(selt.shape)],
        out_specs=row,
        out_shape=jax.ShapeDtypeStruct((n, d), F32),
        compiler_params=_cparams("parallel"),
        name="l1_rwkv_post",
    )(y, bonus, gate, h, lg, lb, wo, sel, selt)


def _pad_lora_out(w2):
    zero = jnp.zeros_like(w2[0])
    return jnp.stack([jnp.concatenate([w2[0], zero], axis=0), jnp.concatenate([zero, w2[1]], axis=0)])


def kernel(x, mem, rel_bias_table, norm_mix, norm_xattn, norm_mem, norm_ffn, norm_final, ab_w_in, ab_w_out, conv_w, conv_b, conv_ln_g, conv_ln_b, diff_lq1, diff_lk1, diff_lq2, diff_lk2, diff_subln_g, rwkv_mu, rwkv_w_r, rwkv_w_k, rwkv_w_v, rwkv_w_o, rwkv_w0, rwkv_w1, rwkv_w2, rwkv_a0, rwkv_a1, rwkv_a2, rwkv_g1, rwkv_g2, rwkv_k_k, rwkv_k_a, rwkv_r_k, rwkv_ln_g, rwkv_ln_b, xattn_w_q, xattn_w_kv, xattn_w_o, ffn_w_up, ffn_w_down):
    bsz, t, d = x.shape
    n = bsz * t
    depth = norm_mix.shape[0]
    n_mem = mem.shape[1]
    cc = conv_w.shape[-1]
    qk = DIFF_HEADS * 2 * DIFF_HEAD_DIM
    vw = ab_w_in.shape[-1] - 2 * cc - 2 * qk
    bf = lambda w: w.astype(BF16)
    row = lambda w: w.reshape(1, -1)

    heads = d // RWKV_HEAD_DIM
    head_of = jnp.arange(d, dtype=jnp.int32) // RWKV_HEAD_DIM
    sel = (head_of[:, None] == jnp.arange(128, dtype=jnp.int32)[None, :]).astype(BF16)
    selt = sel.T
    assert heads <= 128

    h = x.reshape(n, d)
    mem2 = mem.reshape(bsz * n_mem, d)
    for i in range(depth):
        j = i // 2
        if i % 2 == 0:
            lam_init = 0.8 - 0.6 * math.exp(-0.3 * i)
            u, q, k, v = _inproj(h, row(norm_mix[i]), bf(ab_w_in[j]), cc, qk, vw)
            u = _conv_module(u.reshape(bsz, t, cc), conv_w[j], row(conv_b[j]), row(conv_ln_g[j]),
                             row(conv_ln_b[j]))
            lq = jnp.stack([diff_lq1[j], diff_lk1[j], diff_lq2[j], diff_lk2[j]])
            o = _diff_attention(q.reshape(bsz, t, qk), k.reshape(bsz, t, qk), v.reshape(bsz, t, vw),
                                rel_bias_table.reshape(-1), lq, row(diff_subln_g[j]), lam_init)
            h = _outproj(u.reshape(n, cc), o.reshape(n, vw), bf(ab_w_out[j]), h)
        else:
            h3 = h.reshape(bsz, t, d)
            w1 = bf(jnp.concatenate([rwkv_w1[j, 0], rwkv_w1[j, 1]], axis=1))
            a1 = bf(jnp.concatenate([rwkv_a1[j, 0], rwkv_a1[j, 1]], axis=1))
            r, v, kn, gate, bonus, kd, lw, b = _rwkv_prep(
                h3, row(norm_mix[i]), rwkv_mu[j], bf(rwkv_w_r[j]), bf(rwkv_w_k[j]), bf(rwkv_w_v[j]),
                w1, bf(_pad_lora_out(rwkv_w2[j])), rwkv_w0[j], a1, bf(_pad_lora_out(rwkv_a2[j])), rwkv_a0[j],
                bf(rwkv_g1[j]), bf(rwkv_g2[j]), row(rwkv_k_k[j]), row(rwkv_k_a[j]), row(rwkv_r_k[j]),
                sel, selt)
            y = _wkv7_scan(r, v, kn, kd, lw, b)
            h = _rwkv_post(y.reshape(n, d), bonus.reshape(n, d), gate.reshape(n, d), h,
                           row(rwkv_ln_g[j]), row(rwkv_ln_b[j]), bf(rwkv_w_o[j]), sel, selt)
        kv = _norm_linear(mem2, row(norm_mem[i]), bf(xattn_w_kv[i]), BF16)
        h = _cross_attention(h.reshape(bsz, t, d), kv.reshape(bsz, n_mem, 2 * d), row(norm_xattn[i]),
                             bf(xattn_w_q[i]), bf(xattn_w_o[i])).reshape(n, d)
        h = _mlp(h, row(norm_ffn[i]), bf(ffn_w_up[i]), bf(ffn_w_down[i]), row(norm_final),
                 final_norm=(i == depth - 1))
    return h.reshape(bsz, t, d)
```

```python
import functools
import math

import jax
import jax.numpy as jnp
from jax import lax
from jax.experimental import pallas as pl
from jax.experimental.pallas import tpu as pltpu

F32 = jnp.float32
BF16 = jnp.bfloat16

V7X_VMEM_BYTES = 64 * 1024 * 1024
VMEM_LIMIT_BYTES = V7X_VMEM_BYTES - 8 * 1024 * 1024

NORM_EPS = 1e-6
CONV_LN_EPS = 1e-5
GN_EPS = 64e-5
CONV_WIDTH = 31
CONV_PAD = CONV_WIDTH // 2
CONV_HALO = 16
DIFF_HEADS = 4
DIFF_HEAD_DIM = 64
REL_BUCKETS = 32
REL_MAX_DIST = 128
XATTN_HEADS = 4
RWKV_HEAD_DIM = 64
CHUNK = 64
CHUNK_SHIFT = CHUNK.bit_length() - 1
PAIR = 2 * RWKV_HEAD_DIM


def _cparams(*sem):
    return pltpu.CompilerParams(dimension_semantics=sem, vmem_limit_bytes=VMEM_LIMIT_BYTES)


def _resident(shape):
    nd = len(shape)
    return pl.BlockSpec(shape, lambda *_: (0,) * nd, pipeline_mode=pl.Buffered(1))


def _rms(x, g):
    ms = jnp.mean(x * x, axis=-1, keepdims=True)
    return x * lax.rsqrt(ms + NORM_EPS) * g


def _sigmoid(x):
    return 1.0 / (1.0 + jnp.exp(-x))


def _dot(a, b):
    return jnp.dot(a.astype(BF16), b.astype(BF16), preferred_element_type=F32)


def _dot_nt(a, b):
    return lax.dot_general(a.astype(BF16), b.astype(BF16), (((1,), (1,)), ((), ())),
                           preferred_element_type=F32)


def _split_dot(mat, x):
    hi = x.astype(BF16)
    lo = (x - hi.astype(F32)).astype(BF16)
    return (jnp.dot(mat, hi, preferred_element_type=F32) + jnp.dot(mat, lo, preferred_element_type=F32))


def _inproj_kernel(h_ref, g_ref, w_ref, u_ref, q_ref, k_ref, v_ref, *, cc, qk, scale):
    xn = _rms(h_ref[...], g_ref[...])
    p = _dot(xn, w_ref[...])
    u_ref[...] = p[:, :cc] * _sigmoid(p[:, cc:2 * cc])
    o = 2 * cc
    q_ref[...] = (p[:, o:o + qk] * scale).astype(BF16)
    k_ref[...] = p[:, o + qk:o + 2 * qk].astype(BF16)
    v_ref[...] = p[:, o + 2 * qk:].astype(BF16)


def _inproj(h, g, w, cc, qk, vw, tm=512):
    n, d = h.shape
    row = lambda i: (i, 0)
    return pl.pallas_call(
        functools.partial(_inproj_kernel, cc=cc, qk=qk, scale=DIFF_HEAD_DIM ** -0.5),
        grid=(n // tm,),
        in_specs=[pl.BlockSpec((tm, d), row), _resident((1, d)), _resident(w.shape)],
        out_specs=[pl.BlockSpec((tm, cc), row), pl.BlockSpec((tm, qk), row),
                   pl.BlockSpec((tm, qk), row), pl.BlockSpec((tm, vw), row)],
        out_shape=[jax.ShapeDtypeStruct((n, cc), F32), jax.ShapeDtypeStruct((n, qk), BF16),
                   jax.ShapeDtypeStruct((n, qk), BF16), jax.ShapeDtypeStruct((n, vw), BF16)],
        compiler_params=_cparams("parallel"),
        name="l0_inproj",
    )(h, g, w)


def _conv_kernel(u_ref, w_ref, b_ref, g_ref, beta_ref, o_ref, xp_ref, *, t, rows):
    c = u_ref.shape[-1]
    zeros = jnp.zeros((CONV_HALO, c), F32)
    xp_ref[0:CONV_HALO, :] = zeros
    xp_ref[CONV_HALO + t:, :] = zeros
    xp_ref[CONV_HALO:CONV_HALO + t, :] = u_ref[0]
    off = CONV_HALO - CONV_PAD

    def body(i, carry):
        base = pl.multiple_of(i * rows, rows)
        win = xp_ref[pl.ds(base, rows + 2 * CONV_HALO), :]
        acc = jnp.zeros((rows, c), F32)
        for k in range(CONV_WIDTH):
            acc = acc + win[off + k:off + k + rows, :] * w_ref[k:k + 1, :]
        y = acc + b_ref[...]
        mu = jnp.mean(y, axis=-1, keepdims=True)
        yc = y - mu
        var = jnp.mean(yc * yc, axis=-1, keepdims=True)
        yn = yc * lax.rsqrt(var + CONV_LN_EPS) * g_ref[...] + beta_ref[...]
        o_ref[0, pl.ds(base, rows), :] = (yn * _sigmoid(yn)).astype(BF16)
        return carry

    lax.fori_loop(0, t // rows, body, 0)


def _conv_module(u, w, b, g, beta, rows=64):
    bsz, t, c = u.shape
    return pl.pallas_call(
        functools.partial(_conv_kernel, t=t, rows=rows),
        grid=(bsz,),
        in_specs=[pl.BlockSpec((1, t, c), lambda i: (i, 0, 0)), _resident(w.shape),
                  _resident((1, c)), _resident((1, c)), _resident((1, c))],
        out_specs=pl.BlockSpec((1, t, c), lambda i: (i, 0, 0)),
        out_shape=jax.ShapeDtypeStruct((bsz, t, c), BF16),
        scratch_shapes=[pltpu.VMEM((t + 2 * CONV_HALO, c), F32)],
        compiler_params=_cparams("parallel"),
        name="l0_conv",
    )(u, w, b, g, beta)


def _t5_bucket(rel):
    nb = REL_BUCKETS // 2
    max_exact = nb // 2
    n = jnp.abs(rel)
    large = jnp.full(rel.shape, max_exact, jnp.int32)
    steps = nb - max_exact
    for m in range(1, steps):
        thr = math.ceil(max_exact * (REL_MAX_DIST / max_exact) ** (m / steps) - 1e-9)
        large = large + jnp.where(n >= thr, 1, 0)
    mag = jnp.where(n < max_exact, n, large)
    return mag + jnp.where(rel > 0, nb, 0)


def _diffattn_kernel(tbl_ref, lq_ref, sg_ref, q_ref, k_ref, v_ref, o_ref, bias_ref, *, tq, lam_init):
    h = pl.program_id(0)
    qi = pl.program_id(1)
    b = pl.program_id(2)
    t = k_ref.shape[1]

    @pl.when(b == 0)
    def _():
        qpos = qi * tq + lax.broadcasted_iota(jnp.int32, (tq, t), 0)
        kpos = lax.broadcasted_iota(jnp.int32, (tq, t), 1)
        bucket = _t5_bucket(kpos - qpos)
        bias = jnp.zeros((tq, t), F32)
        for i in range(REL_BUCKETS):
            bias = jnp.where(bucket == i, tbl_ref[i * DIFF_HEADS + h], bias)
        bias_ref[...] = bias

    lq = lq_ref[...]
    lam = (jnp.exp(jnp.sum(lq[0:1] * lq[1:2], axis=-1, keepdims=True))
           - jnp.exp(jnp.sum(lq[2:3] * lq[3:4], axis=-1, keepdims=True)) + lam_init)
    q = q_ref[0]
    k = k_ref[0]
    v = v_ref[0]
    first = lax.broadcasted_iota(jnp.int32, q.shape, 1) < DIFF_HEAD_DIM
    zero = jnp.zeros_like(q)

    def component(qc):
        s = _dot_nt(qc, k) + bias_ref[...]
        m = jnp.max(s, axis=-1, keepdims=True)
        e = jnp.exp(s - m)
        l = jnp.sum(e, axis=-1, keepdims=True)
        return _dot(e, v) / l

    o = component(jnp.where(first, q, zero)) - lam * component(jnp.where(first, zero, q))
    o = o * lax.rsqrt(jnp.mean(o * o, axis=-1, keepdims=True) + NORM_EPS) * sg_ref[...] * (1.0 - lam_init)
    o_ref[0] = o.astype(BF16)


def _diff_attention(q, k, v, tbl, lq, sg, lam_init, tq=256):
    bsz, t, _ = q.shape
    hw = 2 * DIFF_HEAD_DIM
    return pl.pallas_call(
        functools.partial(_diffattn_kernel, tq=tq, lam_init=lam_init),
        grid=(DIFF_HEADS, t // tq, bsz),
        in_specs=[pl.BlockSpec(memory_space=pltpu.SMEM), _resident(lq.shape), _resident(sg.shape),
                  pl.BlockSpec((1, tq, hw), lambda h, i, b: (b, i, h)),
                  pl.BlockSpec((1, t, hw), lambda h, i, b: (b, 0, h)),
                  pl.BlockSpec((1, t, hw), lambda h, i, b: (b, 0, h))],
        out_specs=pl.BlockSpec((1, tq, hw), lambda h, i, b: (b, i, h)),
        out_shape=jax.ShapeDtypeStruct(q.shape, BF16),
        scratch_shapes=[pltpu.VMEM((tq, t), F32)],
        compiler_params=_cparams("parallel", "parallel", "arbitrary"),
        name="l0_diffattn",
    )(tbl, lq, sg, q, k, v)


def _outproj_kernel(u_ref, o_ref, w_ref, h_ref, out_ref):
    cc = u_ref.shape[-1]
    out_ref[...] = (h_ref[...] + _dot(u_ref[...], w_ref[:cc, :]) + _dot(o_ref[...], w_ref[cc:, :]))


def _outproj(u, o, w, h, tm=512):
    n, d = h.shape
    row = lambda i: (i, 0)
    return pl.pallas_call(
        _outproj_kernel,
        grid=(n // tm,),
        in_specs=[pl.BlockSpec((tm, u.shape[1]), row), pl.BlockSpec((tm, o.shape[1]), row),
                  _resident(w.shape), pl.BlockSpec((tm, d), row)],
        out_specs=pl.BlockSpec((tm, d), row),
        out_shape=jax.ShapeDtypeStruct((n, d), F32),
        compiler_params=_cparams("parallel"),
        name="l0_outproj",
    )(u, o, w, h)


def _norm_linear_kernel(x_ref, g_ref, w_ref, o_ref):
    o_ref[...] = _dot(_rms(x_ref[...], g_ref[...]), w_ref[...]).astype(o_ref.dtype)


def _norm_linear(x, g, w, out_dtype, tm=512):
    n, d = x.shape
    m = w.shape[1]
    return pl.pallas_call(
        _norm_linear_kernel,
        grid=(n // tm,),
        in_specs=[pl.BlockSpec((tm, d), lambda i: (i, 0)), _resident((1, d)), _resident(w.shape)],
        out_specs=pl.BlockSpec((tm, m), lambda i: (i, 0)),
        out_shape=jax.ShapeDtypeStruct((n, m), out_dtype),
        compiler_params=_cparams("parallel"),
        name="norm_linear",
    )(x, g, w)


def _xattn_kernel(h_ref, g_ref, wq_ref, kv_ref, wo_ref, o_ref):
    d = h_ref.shape[-1]
    hd = d // XATTN_HEADS
    h = h_ref[0]
    q = (_dot(_rms(h, g_ref[...]), wq_ref[...]) * (hd ** -0.5)).astype(BF16)
    outs = []
    for i in range(XATTN_HEADS):
        kh = kv_ref[0, :, i * hd:(i + 1) * hd]
        vh = kv_ref[0, :, d + i * hd:d + (i + 1) * hd]
        s = _dot_nt(q[:, i * hd:(i + 1) * hd], kh)
        m = jnp.max(s, axis=-1, keepdims=True)
        e = jnp.exp(s - m)
        p = e / jnp.sum(e, axis=-1, keepdims=True)
        outs.append(_dot(p, vh).astype(BF16))
    o_ref[0] = h + _dot(jnp.concatenate(outs, axis=1), wo_ref[...])


def _cross_attention(h, kv, g, wq, wo, tq=512):
    bsz, t, d = h.shape
    m = kv.shape[1]
    return pl.pallas_call(
        _xattn_kernel,
        grid=(bsz, t // tq),
        in_specs=[pl.BlockSpec((1, tq, d), lambda b, i: (b, i, 0)), _resident((1, d)), _resident(wq.shape),
                  pl.BlockSpec((1, m, 2 * d), lambda b, i: (b, 0, 0)), _resident(wo.shape)],
        out_specs=pl.BlockSpec((1, tq, d), lambda b, i: (b, i, 0)),
        out_shape=jax.ShapeDtypeStruct(h.shape, F32),
        compiler_params=_cparams("parallel", "parallel"),
        name="xattn",
    )(h, g, wq, kv, wo)


def _mlp_kernel(h_ref, g_ref, wu_ref, wd_ref, gf_ref, o_ref, *, hc, final_norm):
    h = h_ref[...]
    xn = _rms(h, g_ref[...]).astype(BF16)
    acc = h
    for c in range(wu_ref.shape[1] // hc):
        a = jnp.maximum(_dot(xn, wu_ref[:, c * hc:(c + 1) * hc]), 0.0)
        acc = acc + _dot(a * a, wd_ref[c * hc:(c + 1) * hc, :])
    if final_norm:
        acc = _rms(acc, gf_ref[...])
    o_ref[...] = acc


def _mlp(h, g, wu, wd, gf, final_norm, tm=512, hc=1024):
    n, d = h.shape
    row = lambda i: (i, 0)
    return pl.pallas_call(
        functools.partial(_mlp_kernel, hc=hc, final_norm=final_norm),
        grid=(n // tm,),
        in_specs=[pl.BlockSpec((tm, d), row), _resident((1, d)), _resident(wu.shape), _resident(wd.shape),
                  _resident((1, d))],
        out_specs=pl.BlockSpec((tm, d), row),
        out_shape=jax.ShapeDtypeStruct((n, d), F32),
        compiler_params=_cparams("parallel"),
        name="mlp",
    )(h, g, wu, wd, gf)


def _head_sum(x, sel_ref, selt_ref):
    return _dot(_dot(x, sel_ref[...]), selt_ref[...])


def _rwkv_prep_kernel(h_ref, hp_ref, hn_ref, g_ref, mu_ref, wr_ref, wk_ref, wv_ref, w1_ref, w2_ref, w0_ref,
                      a1_ref, a2_ref, a0_ref, g1_ref, g2_ref, kk_ref, ka_ref, rk_ref, sel_ref, selt_ref,
                      r_out, v_out, kn_out, gate_out, bonus_out, kd_out, cum_out, b_out):
    i = pl.program_id(1)
    last = pl.num_programs(1) - 1
    tm = h_ref.shape[1]
    g = g_ref[...]
    x = _rms(h_ref[0], g)
    prev_row = _rms(hp_ref[0], g)[7:8, :] * jnp.where(i > 0, 1.0, 0.0)
    next_row = _rms(hn_ref[0], g)[0:1, :] * jnp.where(i < last, 1.0, 0.0)
    rowid = lax.broadcasted_iota(jnp.int32, x.shape, 0)
    x_prev = jnp.where(rowid == 0, prev_row, pltpu.roll(x, 1, 0))
    x_next = jnp.where(rowid == tm - 1, next_row, pltpu.roll(x, tm - 1, 0))
    hh = 0.5 * (x_prev + x_next) - x
    xr, xw, xk, xv, xa, xg = [x + hh * mu_ref[j:j + 1, :] for j in range(6)]
    r = _dot(xr, wr_ref[...])
    k = _dot(xk, wk_ref[...])
    v = _dot(xv, wv_ref[...])
    gate_out[0] = _dot(_sigmoid(_dot(xg, g1_ref[...])), g2_ref[...])
    lw = jnp.tanh(_dot(xw, w1_ref[...]))
    la = _dot(xa, a1_ref[...])
    kk = k * kk_ref[...]
    kn = kk / jnp.maximum(jnp.sqrt(_head_sum(kk * kk, sel_ref, selt_ref)), 1e-12)
    r_out[0] = r
    v_out[0] = v
    kn_out[0] = kn
    kd_sum = jnp.zeros_like(k)
    ti = lax.broadcasted_iota(jnp.int32, (tm, tm), 0)
    si = lax.broadcasted_iota(jnp.int32, (tm, tm), 1)
    same_chunk = (ti >> CHUNK_SHIFT) == (si >> CHUNK_SHIFT)
    before = (jnp.where(same_chunk, jnp.where(si <= ti, 1.0, 0.0), 0.0).astype(BF16),
              jnp.where(same_chunk, jnp.where(si >= ti, 1.0, 0.0), 0.0).astype(BF16))
    for z in range(2):
        w_pre = w0_ref[z:z + 1, :] + _dot(lw, w2_ref[z])
        softplus = jnp.maximum(-w_pre, 0.0) + jnp.log(1.0 + jnp.exp(-jnp.abs(w_pre)))
        cum_out[z, 0] = _split_dot(before[z], -jnp.exp(-softplus - 0.5))
        rate = _sigmoid(a0_ref[z:z + 1, :] + _dot(la, a2_ref[z]))
        kd = k * (1.0 + (rate - 1.0) * ka_ref[...])
        kd_out[z, 0] = kd
        b_out[z, 0] = kn * rate
        kd_sum = kd_sum + kd
    bonus_out[0] = _head_sum(r * kd_sum * rk_ref[...], sel_ref, selt_ref) * v


def _rwkv_prep(h, g, mu, wr, wk, wv, w1, w2, w0, a1, a2, a0, g1, g2, kk, ka, rk, sel, selt, tm=256):
    bsz, t, d = h.shape
    nb = tm // 8
    tile = pl.BlockSpec((1, tm, d), lambda b, i: (b, i, 0))
    tile2 = pl.BlockSpec((2, 1, tm, d), lambda b, i: (0, b, i, 0))
    one = jax.ShapeDtypeStruct((bsz, t, d), F32)
    two = jax.ShapeDtypeStruct((2, bsz, t, d), F32)
    consts = [g, mu, wr, wk, wv, w1, w2, w0, a1, a2, a0, g1, g2, kk, ka, rk, sel, selt]
    return pl.pallas_call(
        _rwkv_prep_kernel,
        grid=(bsz, t // tm),
        in_specs=[tile,
                  pl.BlockSpec((1, 8, d), lambda b, i: (b, jnp.maximum(i * nb - 1, 0), 0)),
                  pl.BlockSpec((1, 8, d), lambda b, i: (b, jnp.minimum((i + 1) * nb, t // 8 - 1), 0))]
                 + [_resident(c.shape) for c in consts],
        out_specs=[tile, tile, tile, tile, tile, tile2, tile2, tile2],
        out_shape=[one, one, one, one, one, two, two, two],
        compiler_params=_cparams("parallel", "parallel"),
        name="l1_rwkv_prep",
    )(h, h, h, *consts)


def _blockdiag(x):
    lane = lax.broadcasted_iota(jnp.int32, x.shape, 1)
    head0 = (lane & (PAIR - 1)) < RWKV_HEAD_DIM
    zero = jnp.zeros_like(x)
    return jnp.concatenate([jnp.where(head0, x, zero), jnp.where(head0, zero, x)], axis=0)


def _chunk_local(r, k, v, kn, cum, b, rev):
    c = CHUNK
    bd = _blockdiag
    row = lax.broadcasted_iota(jnp.int32, (c, PAIR), 0)
    lane = lax.broadcasted_iota(jnp.int32, (c, PAIR), 1)
    s_idx = lane & (c - 1)
    if rev:
        cum_prev = jnp.where(row == c - 1, 0.0, pltpu.roll(cum, c - 1, 0))
        tot = cum[0:1, :]
        strict = s_idx > row
        incl = s_idx >= row
    else:
        cum_prev = jnp.where(row == 0, 0.0, pltpu.roll(cum, 1, 0))
        tot = cum[c - 1:c, :]
        strict = s_idx < row
        incl = s_idx <= row
    w_incl = jnp.exp(cum)
    w_excl = jnp.exp(cum_prev)
    w_inv = jnp.exp(-cum)
    w_tot = jnp.exp(tot)
    w_rest = jnp.exp(tot - cum)
    a_t = -kn * w_excl
    r_t = r * w_incl
    b_t = b * w_inv
    k_t = k * w_inv
    b_h = b * w_rest
    k_h = k * w_rest
    same_blk = (s_idx >> 4) == (row >> 4)

    sc = _dot_nt(jnp.concatenate([a_t, r_t], axis=0), jnp.concatenate([bd(b_t), bd(k_t)], axis=0))
    yield
    p_ab = jnp.where(strict, sc[:c, :PAIR], 0.0)
    p_ak = jnp.where(strict, sc[:c, PAIR:], 0.0)
    p_rb = jnp.where(incl, sc[c:, :PAIR], 0.0)
    p_rk = jnp.where(incl, sc[c:, PAIR:], 0.0)
    dm = jnp.where(same_blk, p_ab, 0.0)
    em = p_ab - dm
    x2 = _dot(dm, bd(dm))
    av = _dot(p_ak, bd(v))
    yield
    td = jnp.where(s_idx == row, 1.0, 0.0) + dm
    x4 = _dot(x2, bd(x2))
    td = td + _dot(td, bd(x2))
    yield
    x8 = _dot(x4, bd(x4))
    td = td + _dot(td, bd(x4))
    yield
    td = td + _dot(td, bd(x8))
    yield
    ty = _dot(td, bd(jnp.concatenate([a_t, av, em], axis=1)))
    yield
    au, f1 = ty[:, :2 * PAIR], ty[:, 2 * PAIR:]
    f2 = _dot(f1, bd(f1))
    au = au + _dot(f1, bd(au))
    yield
    au = au + _dot(f2, bd(au))
    yield
    rhs = jnp.concatenate([bd(au), jnp.concatenate([jnp.zeros((PAIR, PAIR), F32), bd(v)], axis=1)], axis=0)
    ry = _dot(jnp.concatenate([p_rb, p_rk], axis=1), rhs)
    rhs2 = jnp.concatenate([au, jnp.concatenate([jnp.zeros((c, PAIR), F32), v], axis=1)], axis=0)
    mg = _dot(jnp.concatenate([b_h, k_h], axis=0).T, rhs2)
    yield
    rbar = r_t + ry[:, :PAIR]
    yloc = ry[:, PAIR:]
    r2 = lax.broadcasted_iota(jnp.int32, (PAIR, PAIR), 0)
    l2 = lax.broadcasted_iota(jnp.int32, (PAIR, PAIR), 1)
    same_head = (r2 >> 6) == (l2 >> 6)
    m = jnp.where(same_head, mg[:, :PAIR], 0.0) + jnp.where(r2 == l2, w_tot, 0.0)
    gg = jnp.where(same_head, mg[:, PAIR:], 0.0)
    return rbar, yloc, m, gg


def _run_lockstep(gens):
    results = [None] * len(gens)
    active = list(range(len(gens)))
    while active:
        for i in list(active):
            try:
                next(gens[i])
            except StopIteration as stop:
                results[i] = stop.value
                active.remove(i)
    return results


def _scan_kernel(r_ref, v_ref, kn_ref, kd_ref, cum_ref, b_ref, yf_ref, yb_ref, ds_ref, *, unroll):
    nc = r_ref.shape[1] // CHUNK
    ds_ref[...] = jnp.zeros(ds_ref.shape, F32)
    y_refs = (yf_ref, yb_ref)

    def body(it, carry):
        where = []
        gens = []
        for u in range(unroll):
            for z in range(2):
                cidx = it * unroll + u
                if z == 1:
                    cidx = nc - 1 - cidx
                sl = pl.ds(pl.multiple_of(cidx * CHUNK, CHUNK), CHUNK)
                where.append((z, sl))
                gens.append(_chunk_local(r_ref[0, sl, :], kd_ref[z, 0, sl, :], v_ref[0, sl, :], kn_ref[0, sl, :],
                                         cum_ref[z, 0, sl, :], b_ref[z, 0, sl, :], rev=(z == 1)))
        ds = [ds_ref[0], ds_ref[1]]
        for (z, sl), (rbar, yloc, m, gg) in zip(where, _run_lockstep(gens)):
            y_refs[z][0, sl, :] = _dot(rbar, ds[z]) + yloc
            ds[z] = _dot(m, ds[z]) + gg
        ds_ref[0] = ds[0]
        ds_ref[1] = ds[1]
        return carry

    lax.fori_loop(0, nc // unroll, body, 0)


def _wkv7_scan(r, v, kn, kd, cum, b, unroll=8):
    bsz, t, d = r.shape
    one = pl.BlockSpec((1, t, PAIR), lambda bb, p: (bb, 0, p))
    two = pl.BlockSpec((2, 1, t, PAIR), lambda bb, p: (0, bb, 0, p))
    out = jax.ShapeDtypeStruct((bsz, t, d), F32)
    return pl.pallas_call(
        functools.partial(_scan_kernel, unroll=unroll),
        grid=(bsz, d // PAIR),
        in_specs=[one, one, one, two, two, two],
        out_specs=[one, one],
        out_shape=[out, out],
        scratch_shapes=[pltpu.VMEM((2, PAIR, PAIR), F32)],
        compiler_params=_cparams("parallel", "parallel"),
        name="l1_wkv7_scan",
    )(r, v, kn, kd, cum, b)


def _rwkv_post_kernel(yf_ref, yb_ref, bonus_ref, gate_ref, h_ref, lg_ref, lb_ref, wo_ref, sel_ref, selt_ref,
                      o_ref):
    y = yf_ref[...] + yb_ref[...]
    inv_n = 1.0 / RWKV_HEAD_DIM
    mu = _head_sum(y, sel_ref, selt_ref) * inv_n
    yc = y - mu
    var = _head_sum(yc * yc, sel_ref, selt_ref) * inv_n
    yn = yc * lax.rsqrt(var + GN_EPS) * lg_ref[...] + lb_ref[...]
    o_ref[...] = h_ref[...] + _dot((yn + bonus_ref[...]) * gate_ref[...], wo_ref[...])


def _rwkv_post(yf, yb, bonus, gate, h, lg, lb, wo, sel, selt, tm=512):
    n, d = h.shape
    row = pl.BlockSpec((tm, d), lambda i: (i, 0))
    return pl.pallas_call(
        _rwkv_post_kernel,
        grid=(n // tm,),
        in_specs=[row, row, row, row, row, _resident((1, d)), _resident((1, d)), _resident(wo.shape),
                  _resident(sel.shape), _resident(selt.shape)],
        out_specs=row,
        out_shape=jax.ShapeDtypeStruct((n, d), F32),
        compiler_params=_cparams("parallel"),
        name="l1_rwkv_post",
    )(yf, yb, bonus, gate, h, lg, lb, wo, sel, selt)


def _pad_lora_out(w2):
    zero = jnp.zeros_like(w2[0])
    return jnp.stack([jnp.concatenate([w2[0], zero], axis=0), jnp.concatenate([zero, w2[1]], axis=0)])


def kernel(x, mem, rel_bias_table, norm_mix, norm_xattn, norm_mem, norm_ffn, norm_final, ab_w_in, ab_w_out, conv_w, conv_b, conv_ln_g, conv_ln_b, diff_lq1, diff_lk1, diff_lq2, diff_lk2, diff_subln_g, rwkv_mu, rwkv_w_r, rwkv_w_k, rwkv_w_v, rwkv_w_o, rwkv_w0, rwkv_w1, rwkv_w2, rwkv_a0, rwkv_a1, rwkv_a2, rwkv_g1, rwkv_g2, rwkv_k_k, rwkv_k_a, rwkv_r_k, rwkv_ln_g, rwkv_ln_b, xattn_w_q, xattn_w_kv, xattn_w_o, ffn_w_up, ffn_w_down):
    bsz, t, d = x.shape
    n = bsz * t
    depth = norm_mix.shape[0]
    n_mem = mem.shape[1]
    cc = conv_w.shape[-1]
    qk = DIFF_HEADS * 2 * DIFF_HEAD_DIM
    vw = ab_w_in.shape[-1] - 2 * cc - 2 * qk
    bf = lambda w: w.astype(BF16)
    row = lambda w: w.reshape(1, -1)

    heads = d // RWKV_HEAD_DIM
    head_of = jnp.arange(d, dtype=jnp.int32) // RWKV_HEAD_DIM
    sel = (head_of[:, None] == jnp.arange(128, dtype=jnp.int32)[None, :]).astype(BF16)
    selt = sel.T
    assert heads <= 128

    h = x.reshape(n, d)
    mem2 = mem.reshape(bsz * n_mem, d)
    for i in range(depth):
        j = i // 2
        if i % 2 == 0:
            lam_init = 0.8 - 0.6 * math.exp(-0.3 * i)
            u, q, k, v = _inproj(h, row(norm_mix[i]), bf(ab_w_in[j]), cc, qk, vw)
            u = _conv_module(u.reshape(bsz, t, cc), conv_w[j], row(conv_b[j]), row(conv_ln_g[j]),
                             row(conv_ln_b[j]))
            lq = jnp.stack([diff_lq1[j], diff_lk1[j], diff_lq2[j], diff_lk2[j]])
            o = _diff_attention(q.reshape(bsz, t, qk), k.reshape(bsz, t, qk), v.reshape(bsz, t, vw),
                                rel_bias_table.reshape(-1), lq, row(diff_subln_g[j]), lam_init)
            h = _outproj(u.reshape(n, cc), o.reshape(n, vw), bf(ab_w_out[j]), h)
        else:
            h3 = h.reshape(bsz, t, d)
            w1 = bf(jnp.concatenate([rwkv_w1[j, 0], rwkv_w1[j, 1]], axis=1))
            a1 = bf(jnp.concatenate([rwkv_a1[j, 0], rwkv_a1[j, 1]], axis=1))
            r, v, kn, gate, bonus, kd, cum, b = _rwkv_prep(
                h3, row(norm_mix[i]), rwkv_mu[j], bf(rwkv_w_r[j]), bf(rwkv_w_k[j]), bf(rwkv_w_v[j]),
                w1, bf(_pad_lora_out(rwkv_w2[j])), rwkv_w0[j], a1, bf(_pad_lora_out(rwkv_a2[j])), rwkv_a0[j],
                bf(rwkv_g1[j]), bf(rwkv_g2[j]), row(rwkv_k_k[j]), row(rwkv_k_a[j]), row(rwkv_r_k[j]),
                sel, selt)
            yf, yb = _wkv7_scan(r, v, kn, kd, cum, b)
            h = _rwkv_post(yf.reshape(n, d), yb.reshape(n, d), bonus.reshape(n, d), gate.reshape(n, d), h,
                           row(rwkv_ln_g[j]), row(rwkv_ln_b[j]), bf(rwkv_w_o[j]), sel, selt)
        kv = _norm_linear(mem2, row(norm_mem[i]), bf(xattn_w_kv[i]), BF16)
        h = _cross_attention(h.reshape(bsz, t, d), kv.reshape(bsz, n_mem, 2 * d), row(norm_xattn[i]),
                             bf(xattn_w_q[i]), bf(xattn_w_o[i])).reshape(n, d)
        h = _mlp(h, row(norm_ffn[i]), bf(ffn_w_up[i]), bf(ffn_w_down[i]), row(norm_final),
                 final_norm=(i == depth - 1))
    return h.reshape(bsz, t, d)
```

```python
import functools
import math

import jax
import jax.numpy as jnp
from jax import lax
from jax.experimental import pallas as pl
from jax.experimental.pallas import tpu as pltpu

F32 = jnp.float32
BF16 = jnp.bfloat16

V7X_VMEM_BYTES = 64 * 1024 * 1024
VMEM_LIMIT_BYTES = V7X_VMEM_BYTES - 8 * 1024 * 1024

LOG2E = math.log2(math.e)
NORM_EPS = 1e-6
CONV_LN_EPS = 1e-5
GN_EPS = 64e-5
CONV_WIDTH = 31
CONV_PAD = CONV_WIDTH // 2
CONV_HALO = 16
DIFF_HEADS = 4
DIFF_HEAD_DIM = 64
REL_BUCKETS = 32
REL_MAX_DIST = 128
XATTN_HEADS = 4
RWKV_HEAD_DIM = 64
CHUNK = 64
CHUNK_SHIFT = CHUNK.bit_length() - 1
PAIR = 2 * RWKV_HEAD_DIM


def _cparams(*sem):
    return pltpu.CompilerParams(dimension_semantics=sem, vmem_limit_bytes=VMEM_LIMIT_BYTES)


def _resident(shape):
    nd = len(shape)
    return pl.BlockSpec(shape, lambda *_: (0,) * nd, pipeline_mode=pl.Buffered(1))


def _rms(x, g):
    ms = jnp.mean(x * x, axis=-1, keepdims=True)
    return x * lax.rsqrt(ms + NORM_EPS) * g


def _sigmoid(x):
    return 1.0 / (1.0 + jnp.exp(-x))


def _dot(a, b):
    return jnp.dot(a.astype(BF16), b.astype(BF16), preferred_element_type=F32)


def _dot_nt(a, b):
    return lax.dot_general(a.astype(BF16), b.astype(BF16), (((1,), (1,)), ((), ())),
                           preferred_element_type=F32)


def _split_dot(mat, x):
    hi = x.astype(BF16)
    lo = (x - hi.astype(F32)).astype(BF16)
    return (jnp.dot(mat, hi, preferred_element_type=F32) + jnp.dot(mat, lo, preferred_element_type=F32))


def _run_staggered(gens, offset):
    results = [None] * len(gens)
    done = 0
    rnd = 0
    while done < len(gens):
        for i, gen in enumerate(gens):
            if rnd >= i * offset and results[i] is None:
                try:
                    next(gen)
                except StopIteration as stop:
                    results[i] = stop.value
                    done += 1
        rnd += 1
    return results


def _inproj_kernel(h_ref, g_ref, w_ref, u_ref, q_ref, k_ref, v_ref, *, cc, qk, scale):
    xn = _rms(h_ref[...], g_ref[...])
    p = _dot(xn, w_ref[...])
    u_ref[...] = p[:, :cc] * _sigmoid(p[:, cc:2 * cc])
    o = 2 * cc
    q_ref[...] = (p[:, o:o + qk] * scale).astype(BF16)
    k_ref[...] = p[:, o + qk:o + 2 * qk].astype(BF16)
    v_ref[...] = p[:, o + 2 * qk:].astype(BF16)


def _inproj(h, g, w, cc, qk, vw, tm=512):
    n, d = h.shape
    row = lambda i: (i, 0)
    return pl.pallas_call(
        functools.partial(_inproj_kernel, cc=cc, qk=qk, scale=DIFF_HEAD_DIM ** -0.5 * LOG2E),
        grid=(n // tm,),
        in_specs=[pl.BlockSpec((tm, d), row), _resident((1, d)), _resident(w.shape)],
        out_specs=[pl.BlockSpec((tm, cc), row), pl.BlockSpec((tm, qk), row),
                   pl.BlockSpec((tm, qk), row), pl.BlockSpec((tm, vw), row)],
        out_shape=[jax.ShapeDtypeStruct((n, cc), F32), jax.ShapeDtypeStruct((n, qk), BF16),
                   jax.ShapeDtypeStruct((n, qk), BF16), jax.ShapeDtypeStruct((n, vw), BF16)],
        compiler_params=_cparams("parallel"),
        name="l0_inproj",
    )(h, g, w)


def _conv_kernel(u_ref, w_ref, b_ref, g_ref, beta_ref, o_ref, xp_ref, *, t, rows):
    c = u_ref.shape[-1]
    zeros = jnp.zeros((CONV_HALO, c), F32)
    xp_ref[0:CONV_HALO, :] = zeros
    xp_ref[CONV_HALO + t:, :] = zeros
    xp_ref[CONV_HALO:CONV_HALO + t, :] = u_ref[0]
    off = CONV_HALO - CONV_PAD

    def body(i, carry):
        base = pl.multiple_of(i * rows, rows)
        win = xp_ref[pl.ds(base, rows + 2 * CONV_HALO), :]
        acc = jnp.zeros((rows, c), F32)
        for k in range(CONV_WIDTH):
            acc = acc + win[off + k:off + k + rows, :] * w_ref[k:k + 1, :]
        y = acc + b_ref[...]
        mu = jnp.mean(y, axis=-1, keepdims=True)
        yc = y - mu
        var = jnp.mean(yc * yc, axis=-1, keepdims=True)
        yn = yc * lax.rsqrt(var + CONV_LN_EPS) * g_ref[...] + beta_ref[...]
        o_ref[0, pl.ds(base, rows), :] = (yn * _sigmoid(yn)).astype(BF16)
        return carry

    lax.fori_loop(0, t // rows, body, 0)


def _conv_module(u, w, b, g, beta, rows=64):
    bsz, t, c = u.shape
    return pl.pallas_call(
        functools.partial(_conv_kernel, t=t, rows=rows),
        grid=(bsz,),
        in_specs=[pl.BlockSpec((1, t, c), lambda i: (i, 0, 0)), _resident(w.shape),
                  _resident((1, c)), _resident((1, c)), _resident((1, c))],
        out_specs=pl.BlockSpec((1, t, c), lambda i: (i, 0, 0)),
        out_shape=jax.ShapeDtypeStruct((bsz, t, c), BF16),
        scratch_shapes=[pltpu.VMEM((t + 2 * CONV_HALO, c), F32)],
        compiler_params=_cparams("parallel"),
        name="l0_conv",
    )(u, w, b, g, beta)


def _t5_bucket(rel):
    nb = REL_BUCKETS // 2
    max_exact = nb // 2
    n = jnp.abs(rel)
    large = jnp.full(rel.shape, max_exact, jnp.int32)
    steps = nb - max_exact
    for m in range(1, steps):
        thr = math.ceil(max_exact * (REL_MAX_DIST / max_exact) ** (m / steps) - 1e-9)
        large = large + jnp.where(n >= thr, 1, 0)
    mag = jnp.where(n < max_exact, n, large)
    return mag + jnp.where(rel > 0, nb, 0)


def _diffattn_kernel(tbl_ref, lq_ref, sg_ref, q_ref, k_ref, v_ref, o_ref, bias_ref, *, tq, ts, kb, lam_init):
    h = pl.program_id(0)
    qi = pl.program_id(1)
    b = pl.program_id(2)
    t = k_ref.shape[1]

    @pl.when(b == 0)
    def _():
        u = lax.broadcasted_iota(jnp.int32, (1, t + tq), 1)
        bucket = _t5_bucket(u - (tq - 1) - qi * tq)
        line = jnp.zeros((1, t + tq), F32)
        for i in range(REL_BUCKETS):
            line = jnp.where(bucket == i, tbl_ref[i * DIFF_HEADS + h], line)
        rows = pltpu.roll(jnp.broadcast_to(line * LOG2E, (tq, t + tq)), 1, 1, stride=1, stride_axis=0)
        bias_ref[...] = rows[:, tq:]

    lq = lq_ref[...]
    lam = (jnp.exp(jnp.sum(lq[0:1] * lq[1:2], axis=-1, keepdims=True))
           - jnp.exp(jnp.sum(lq[2:3] * lq[3:4], axis=-1, keepdims=True)) + lam_init)
    hw = q_ref.shape[-1]
    first = lax.broadcasted_iota(jnp.int32, (ts, hw), 1) < DIFF_HEAD_DIM
    nkb = t // kb

    def softmax_v(r0, comp):
        q = q_ref[0, r0:r0 + ts, :]
        qc = jnp.where(first, q, jnp.zeros_like(q)) if comp == 0 else jnp.where(first, jnp.zeros_like(q), q)
        s = []
        mx = None
        for j in range(nkb):
            sj = _dot_nt(qc, k_ref[0, j * kb:(j + 1) * kb, :]) + bias_ref[r0:r0 + ts, j * kb:(j + 1) * kb]
            mj = jnp.max(sj, axis=-1, keepdims=True)
            mx = mj if mx is None else jnp.maximum(mx, mj)
            s.append(sj)
            yield
        pv = None
        for j in range(nkb):
            vj = v_ref[0, j * kb:(j + 1) * kb, :]
            dj = _dot(jnp.exp2(s[j] - mx), jnp.concatenate([vj, jnp.ones_like(vj)], axis=1))
            pv = dj if pv is None else pv + dj
            yield
        return pv[:, :hw] / pv[:, hw:]

    starts = range(0, tq, ts)
    maps = _run_staggered([softmax_v(r0, comp) for r0 in starts for comp in range(2)], nkb)
    for i, r0 in enumerate(starts):
        o = maps[2 * i] - lam * maps[2 * i + 1]
        o = o * lax.rsqrt(jnp.mean(o * o, axis=-1, keepdims=True) + NORM_EPS) * sg_ref[...] * (1.0 - lam_init)
        o_ref[0, r0:r0 + ts, :] = o.astype(BF16)


def _diff_attention(q, k, v, tbl, lq, sg, lam_init, tq=512, ts=256, kb=512):
    bsz, t, _ = q.shape
    hw = 2 * DIFF_HEAD_DIM
    return pl.pallas_call(
        functools.partial(_diffattn_kernel, tq=tq, ts=ts, kb=kb, lam_init=lam_init),
        grid=(DIFF_HEADS, t // tq, bsz),
        in_specs=[pl.BlockSpec(memory_space=pltpu.SMEM), _resident(lq.shape), _resident(sg.shape),
                  pl.BlockSpec((1, tq, hw), lambda h, i, b: (b, i, h)),
                  pl.BlockSpec((1, t, hw), lambda h, i, b: (b, 0, h)),
                  pl.BlockSpec((1, t, hw), lambda h, i, b: (b, 0, h))],
        out_specs=pl.BlockSpec((1, tq, hw), lambda h, i, b: (b, i, h)),
        out_shape=jax.ShapeDtypeStruct(q.shape, BF16),
        scratch_shapes=[pltpu.VMEM((tq, t), F32)],
        compiler_params=_cparams("parallel", "parallel", "arbitrary"),
        name="l0_diffattn",
    )(tbl, lq, sg, q, k, v)


def _outproj_kernel(u_ref, o_ref, w_ref, h_ref, out_ref):
    cc = u_ref.shape[-1]
    out_ref[...] = (h_ref[...] + _dot(u_ref[...], w_ref[:cc, :]) + _dot(o_ref[...], w_ref[cc:, :]))


def _outproj(u, o, w, h, tm=512):
    n, d = h.shape
    row = lambda i: (i, 0)
    return pl.pallas_call(
        _outproj_kernel,
        grid=(n // tm,),
        in_specs=[pl.BlockSpec((tm, u.shape[1]), row), pl.BlockSpec((tm, o.shape[1]), row),
                  _resident(w.shape), pl.BlockSpec((tm, d), row)],
        out_specs=pl.BlockSpec((tm, d), row),
        out_shape=jax.ShapeDtypeStruct((n, d), F32),
        compiler_params=_cparams("parallel"),
        name="l0_outproj",
    )(u, o, w, h)


def _norm_linear_kernel(x_ref, g_ref, w_ref, o_ref):
    o_ref[...] = _dot(_rms(x_ref[...], g_ref[...]), w_ref[...]).astype(o_ref.dtype)


def _norm_linear(x, g, w, out_dtype, tm=512):
    n, d = x.shape
    m = w.shape[1]
    return pl.pallas_call(
        _norm_linear_kernel,
        grid=(n // tm,),
        in_specs=[pl.BlockSpec((tm, d), lambda i: (i, 0)), _resident((1, d)), _resident(w.shape)],
        out_specs=pl.BlockSpec((tm, m), lambda i: (i, 0)),
        out_shape=jax.ShapeDtypeStruct((n, m), out_dtype),
        compiler_params=_cparams("parallel"),
        name="norm_linear",
    )(x, g, w)


def _xattn_kernel(h_ref, g_ref, wq_ref, kv_ref, wo_ref, o_ref):
    d = h_ref.shape[-1]
    hd = d // XATTN_HEADS
    h = h_ref[0]
    q = (_dot(_rms(h, g_ref[...]), wq_ref[...]) * (hd ** -0.5)).astype(BF16)
    outs = []
    for i in range(XATTN_HEADS):
        kh = kv_ref[0, :, i * hd:(i + 1) * hd]
        vh = kv_ref[0, :, d + i * hd:d + (i + 1) * hd]
        s = _dot_nt(q[:, i * hd:(i + 1) * hd], kh)
        m = jnp.max(s, axis=-1, keepdims=True)
        e = jnp.exp(s - m)
        p = e / jnp.sum(e, axis=-1, keepdims=True)
        outs.append(_dot(p, vh).astype(BF16))
    o_ref[0] = h + _dot(jnp.concatenate(outs, axis=1), wo_ref[...])


def _cross_attention(h, kv, g, wq, wo, tq=512):
    bsz, t, d = h.shape
    m = kv.shape[1]
    return pl.pallas_call(
        _xattn_kernel,
        grid=(bsz, t // tq),
        in_specs=[pl.BlockSpec((1, tq, d), lambda b, i: (b, i, 0)), _resident((1, d)), _resident(wq.shape),
                  pl.BlockSpec((1, m, 2 * d), lambda b, i: (b, 0, 0)), _resident(wo.shape)],
        out_specs=pl.BlockSpec((1, tq, d), lambda b, i: (b, i, 0)),
        out_shape=jax.ShapeDtypeStruct(h.shape, F32),
        compiler_params=_cparams("parallel", "parallel"),
        name="xattn",
    )(h, g, wq, kv, wo)


def _mlp_kernel(h_ref, g_ref, wu_ref, wd_ref, gf_ref, o_ref, *, hc, final_norm):
    h = h_ref[...]
    xn = _rms(h, g_ref[...]).astype(BF16)
    acc = h
    for c in range(wu_ref.shape[1] // hc):
        a = jnp.maximum(_dot(xn, wu_ref[:, c * hc:(c + 1) * hc]), 0.0)
        acc = acc + _dot(a * a, wd_ref[c * hc:(c + 1) * hc, :])
    if final_norm:
        acc = _rms(acc, gf_ref[...])
    o_ref[...] = acc


def _mlp(h, g, wu, wd, gf, final_norm, tm=512, hc=1024):
    n, d = h.shape
    row = lambda i: (i, 0)
    return pl.pallas_call(
        functools.partial(_mlp_kernel, hc=hc, final_norm=final_norm),
        grid=(n // tm,),
        in_specs=[pl.BlockSpec((tm, d), row), _resident((1, d)), _resident(wu.shape), _resident(wd.shape),
                  _resident((1, d))],
        out_specs=pl.BlockSpec((tm, d), row),
        out_shape=jax.ShapeDtypeStruct((n, d), F32),
        compiler_params=_cparams("parallel"),
        name="mlp",
    )(h, g, wu, wd, gf)


def _head_sum(x, sel_ref, selt_ref):
    return _dot(_dot(x, sel_ref[...]), selt_ref[...])


def _rwkv_prep_kernel(h_ref, hp_ref, hn_ref, g_ref, mu_ref, wr_ref, wk_ref, wv_ref, w1_ref, w2_ref, w0_ref,
                      a1_ref, a2_ref, a0_ref, g1_ref, g2_ref, kk_ref, ka_ref, rk_ref, sel_ref, selt_ref,
                      r_out, v_out, kn_out, gate_out, bonus_out, kd_out, cum_out, b_out):
    i = pl.program_id(1)
    last = pl.num_programs(1) - 1
    tm = h_ref.shape[1]
    g = g_ref[...]
    x = _rms(h_ref[0], g)
    prev_row = _rms(hp_ref[0], g)[7:8, :] * jnp.where(i > 0, 1.0, 0.0)
    next_row = _rms(hn_ref[0], g)[0:1, :] * jnp.where(i < last, 1.0, 0.0)
    rowid = lax.broadcasted_iota(jnp.int32, x.shape, 0)
    x_prev = jnp.where(rowid == 0, prev_row, pltpu.roll(x, 1, 0))
    x_next = jnp.where(rowid == tm - 1, next_row, pltpu.roll(x, tm - 1, 0))
    hh = 0.5 * (x_prev + x_next) - x
    xr, xw, xk, xv, xa, xg = [x + hh * mu_ref[j:j + 1, :] for j in range(6)]
    r = _dot(xr, wr_ref[...])
    k = _dot(xk, wk_ref[...])
    v = _dot(xv, wv_ref[...])
    gate_out[0] = _dot(_sigmoid(_dot(xg, g1_ref[...])), g2_ref[...])
    lw = jnp.tanh(_dot(xw, w1_ref[...]))
    la = _dot(xa, a1_ref[...])
    kk = k * kk_ref[...]
    kn = kk / jnp.maximum(jnp.sqrt(_head_sum(kk * kk, sel_ref, selt_ref)), 1e-12)
    r_out[0] = r
    v_out[0] = v
    kn_out[0] = kn
    kd_sum = jnp.zeros_like(k)
    ti = lax.broadcasted_iota(jnp.int32, (tm, tm), 0)
    si = lax.broadcasted_iota(jnp.int32, (tm, tm), 1)
    same_chunk = (ti >> CHUNK_SHIFT) == (si >> CHUNK_SHIFT)
    before = (jnp.where(same_chunk, jnp.where(si <= ti, 1.0, 0.0), 0.0).astype(BF16),
              jnp.where(same_chunk, jnp.where(si >= ti, 1.0, 0.0), 0.0).astype(BF16))
    for z in range(2):
        w_pre = w0_ref[z:z + 1, :] + _dot(lw, w2_ref[z])
        softplus = jnp.maximum(-w_pre, 0.0) + jnp.log(1.0 + jnp.exp(-jnp.abs(w_pre)))
        cum_out[z, 0] = _split_dot(before[z], -jnp.exp(-softplus - 0.5))
        rate = _sigmoid(a0_ref[z:z + 1, :] + _dot(la, a2_ref[z]))
        kd = k * (1.0 + (rate - 1.0) * ka_ref[...])
        kd_out[z, 0] = kd
        b_out[z, 0] = kn * rate
        kd_sum = kd_sum + kd
    bonus_out[0] = _head_sum(r * kd_sum * rk_ref[...], sel_ref, selt_ref) * v


def _rwkv_prep(h, g, mu, wr, wk, wv, w1, w2, w0, a1, a2, a0, g1, g2, kk, ka, rk, sel, selt, tm=256):
    bsz, t, d = h.shape
    nb = tm // 8
    tile = pl.BlockSpec((1, tm, d), lambda b, i: (b, i, 0))
    tile2 = pl.BlockSpec((2, 1, tm, d), lambda b, i: (0, b, i, 0))
    one = jax.ShapeDtypeStruct((bsz, t, d), F32)
    two = jax.ShapeDtypeStruct((2, bsz, t, d), F32)
    consts = [g, mu, wr, wk, wv, w1, w2, w0, a1, a2, a0, g1, g2, kk, ka, rk, sel, selt]
    return pl.pallas_call(
        _rwkv_prep_kernel,
        grid=(bsz, t // tm),
        in_specs=[tile,
                  pl.BlockSpec((1, 8, d), lambda b, i: (b, jnp.maximum(i * nb - 1, 0), 0)),
                  pl.BlockSpec((1, 8, d), lambda b, i: (b, jnp.minimum((i + 1) * nb, t // 8 - 1), 0))]
                 + [_resident(c.shape) for c in consts],
        out_specs=[tile, tile, tile, tile, tile, tile2, tile2, tile2],
        out_shape=[one, one, one, one, one, two, two, two],
        compiler_params=_cparams("parallel", "parallel"),
        name="l1_rwkv_prep",
    )(h, h, h, *consts)


def _blockdiag(x):
    lane = lax.broadcasted_iota(jnp.int32, x.shape, 1)
    head0 = (lane & (PAIR - 1)) < RWKV_HEAD_DIM
    zero = jnp.zeros_like(x)
    return jnp.concatenate([jnp.where(head0, x, zero), jnp.where(head0, zero, x)], axis=0)


def _chunk_local(r, k, v, kn, cum, b, rev):
    c = CHUNK
    bd = _blockdiag
    row = lax.broadcasted_iota(jnp.int32, (c, PAIR), 0)
    lane = lax.broadcasted_iota(jnp.int32, (c, PAIR), 1)
    s_idx = lane & (c - 1)
    if rev:
        cum_prev = jnp.where(row == c - 1, 0.0, pltpu.roll(cum, c - 1, 0))
        tot = cum[0:1, :]
        strict = s_idx > row
        incl = s_idx >= row
    else:
        cum_prev = jnp.where(row == 0, 0.0, pltpu.roll(cum, 1, 0))
        tot = cum[c - 1:c, :]
        strict = s_idx < row
        incl = s_idx <= row
    w_incl = jnp.exp(cum)
    w_excl = jnp.exp(cum_prev)
    w_inv = jnp.exp(-cum)
    w_tot = jnp.exp(tot)
    w_rest = jnp.exp(tot - cum)
    a_t = -kn * w_excl
    r_t = r * w_incl
    b_t = b * w_inv
    k_t = k * w_inv
    b_h = b * w_rest
    k_h = k * w_rest
    same_blk = (s_idx >> 4) == (row >> 4)

    sc = _dot_nt(jnp.concatenate([a_t, r_t], axis=0), jnp.concatenate([bd(b_t), bd(k_t)], axis=0))
    yield
    p_ab = jnp.where(strict, sc[:c, :PAIR], 0.0)
    p_ak = jnp.where(strict, sc[:c, PAIR:], 0.0)
    p_rb = jnp.where(incl, sc[c:, :PAIR], 0.0)
    p_rk = jnp.where(incl, sc[c:, PAIR:], 0.0)
    dm = jnp.where(same_blk, p_ab, 0.0)
    em = p_ab - dm
    x2 = _dot(dm, bd(dm))
    av = _dot(p_ak, bd(v))
    yield
    td = jnp.where(s_idx == row, 1.0, 0.0) + dm
    x4 = _dot(x2, bd(x2))
    td = td + _dot(td, bd(x2))
    yield
    x8 = _dot(x4, bd(x4))
    td = td + _dot(td, bd(x4))
    yield
    td = td + _dot(td, bd(x8))
    yield
    ty = _dot(td, bd(jnp.concatenate([a_t, av, em], axis=1)))
    yield
    au, f1 = ty[:, :2 * PAIR], ty[:, 2 * PAIR:]
    f2 = _dot(f1, bd(f1))
    au = au + _dot(f1, bd(au))
    yield
    au = au + _dot(f2, bd(au))
    yield
    rhs = jnp.concatenate([bd(au), jnp.concatenate([jnp.zeros((PAIR, PAIR), F32), bd(v)], axis=1)], axis=0)
    ry = _dot(jnp.concatenate([p_rb, p_rk], axis=1), rhs)
    rhs2 = jnp.concatenate([au, jnp.concatenate([jnp.zeros((c, PAIR), F32), v], axis=1)], axis=0)
    mg = _dot(jnp.concatenate([b_h, k_h], axis=0).T, rhs2)
    yield
    rbar = r_t + ry[:, :PAIR]
    yloc = ry[:, PAIR:]
    r2 = lax.broadcasted_iota(jnp.int32, (PAIR, PAIR), 0)
    l2 = lax.broadcasted_iota(jnp.int32, (PAIR, PAIR), 1)
    same_head = (r2 >> 6) == (l2 >> 6)
    m = jnp.where(same_head, mg[:, :PAIR], 0.0) + jnp.where(r2 == l2, w_tot, 0.0)
    gg = jnp.where(same_head, mg[:, PAIR:], 0.0)
    return rbar, yloc, m, gg


def _scan_kernel(r_ref, v_ref, kn_ref, kd_ref, cum_ref, b_ref, yf_ref, yb_ref, ds_ref, *, unroll):
    nc = r_ref.shape[1] // CHUNK
    ds_ref[...] = jnp.zeros(ds_ref.shape, F32)
    y_refs = (yf_ref, yb_ref)

    def body(it, carry):
        where = []
        gens = []
        for u in range(unroll):
            for z in range(2):
                cidx = it * unroll + u
                if z == 1:
                    cidx = nc - 1 - cidx
                sl = pl.ds(pl.multiple_of(cidx * CHUNK, CHUNK), CHUNK)
                where.append((z, sl))
                gens.append(_chunk_local(r_ref[0, sl, :], kd_ref[z, 0, sl, :], v_ref[0, sl, :], kn_ref[0, sl, :],
                                         cum_ref[z, 0, sl, :], b_ref[z, 0, sl, :], rev=(z == 1)))
        ds = [ds_ref[0], ds_ref[1]]
        for (z, sl), (rbar, yloc, m, gg) in zip(where, _run_staggered(gens, 0)):
            y_refs[z][0, sl, :] = _dot(rbar, ds[z]) + yloc
            ds[z] = _dot(m, ds[z]) + gg
        ds_ref[0] = ds[0]
        ds_ref[1] = ds[1]
        return carry

    lax.fori_loop(0, nc // unroll, body, 0)


def _wkv7_scan(r, v, kn, kd, cum, b, unroll=8):
    bsz, t, d = r.shape
    one = pl.BlockSpec((1, t, PAIR), lambda bb, p: (bb, 0, p))
    two = pl.BlockSpec((2, 1, t, PAIR), lambda bb, p: (0, bb, 0, p))
    out = jax.ShapeDtypeStruct((bsz, t, d), F32)
    return pl.pallas_call(
        functools.partial(_scan_kernel, unroll=unroll),
        grid=(bsz, d // PAIR),
        in_specs=[one, one, one, two, two, two],
        out_specs=[one, one],
        out_shape=[out, out],
        scratch_shapes=[pltpu.VMEM((2, PAIR, PAIR), F32)],
        compiler_params=_cparams("parallel", "parallel"),
        name="l1_wkv7_scan",
    )(r, v, kn, kd, cum, b)


def _rwkv_post_kernel(yf_ref, yb_ref, bonus_ref, gate_ref, h_ref, lg_ref, lb_ref, wo_ref, sel_ref, selt_ref,
                      o_ref):
    y = yf_ref[...] + yb_ref[...]
    inv_n = 1.0 / RWKV_HEAD_DIM
    mu = _head_sum(y, sel_ref, selt_ref) * inv_n
    yc = y - mu
    var = _head_sum(yc * yc, sel_ref, selt_ref) * inv_n
    yn = yc * lax.rsqrt(var + GN_EPS) * lg_ref[...] + lb_ref[...]
    o_ref[...] = h_ref[...] + _dot((yn + bonus_ref[...]) * gate_ref[...], wo_ref[...])


def _rwkv_post(yf, yb, bonus, gate, h, lg, lb, wo, sel, selt, tm=512):
    n, d = h.shape
    row = pl.BlockSpec((tm, d), lambda i: (i, 0))
    return pl.pallas_call(
        _rwkv_post_kernel,
        grid=(n // tm,),
        in_specs=[row, row, row, row, row, _resident((1, d)), _resident((1, d)), _resident(wo.shape),
                  _resident(sel.shape), _resident(selt.shape)],
        out_specs=row,
        out_shape=jax.ShapeDtypeStruct((n, d), F32),
        compiler_params=_cparams("parallel"),
        name="l1_rwkv_post",
    )(yf, yb, bonus, gate, h, lg, lb, wo, sel, selt)


def _pad_lora_out(w2):
    zero = jnp.zeros_like(w2[0])
    return jnp.stack([jnp.concatenate([w2[0], zero], axis=0), jnp.concatenate([zero, w2[1]], axis=0)])


def kernel(x, mem, rel_bias_table, norm_mix, norm_xattn, norm_mem, norm_ffn, norm_final, ab_w_in, ab_w_out, conv_w, conv_b, conv_ln_g, conv_ln_b, diff_lq1, diff_lk1, diff_lq2, diff_lk2, diff_subln_g, rwkv_mu, rwkv_w_r, rwkv_w_k, rwkv_w_v, rwkv_w_o, rwkv_w0, rwkv_w1, rwkv_w2, rwkv_a0, rwkv_a1, rwkv_a2, rwkv_g1, rwkv_g2, rwkv_k_k, rwkv_k_a, rwkv_r_k, rwkv_ln_g, rwkv_ln_b, xattn_w_q, xattn_w_kv, xattn_w_o, ffn_w_up, ffn_w_down):
    bsz, t, d = x.shape
    n = bsz * t
    depth = norm_mix.shape[0]
    n_mem = mem.shape[1]
    cc = conv_w.shape[-1]
    qk = DIFF_HEADS * 2 * DIFF_HEAD_DIM
    vw = ab_w_in.shape[-1] - 2 * cc - 2 * qk
    bf = lambda w: w.astype(BF16)
    row = lambda w: w.reshape(1, -1)

    heads = d // RWKV_HEAD_DIM
    head_of = jnp.arange(d, dtype=jnp.int32) // RWKV_HEAD_DIM
    sel = (head_of[:, None] == jnp.arange(128, dtype=jnp.int32)[None, :]).astype(BF16)
    selt = sel.T
    assert heads <= 128

    h = x.reshape(n, d)
    mem2 = mem.reshape(bsz * n_mem, d)
    for i in range(depth):
        j = i // 2
        if i % 2 == 0:
            lam_init = 0.8 - 0.6 * math.exp(-0.3 * i)
            u, q, k, v = _inproj(h, row(norm_mix[i]), bf(ab_w_in[j]), cc, qk, vw)
            u = _conv_module(u.reshape(bsz, t, cc), conv_w[j], row(conv_b[j]), row(conv_ln_g[j]),
                             row(conv_ln_b[j]))
            lq = jnp.stack([diff_lq1[j], diff_lk1[j], diff_lq2[j], diff_lk2[j]])
            o = _diff_attention(q.reshape(bsz, t, qk), k.reshape(bsz, t, qk), v.reshape(bsz, t, vw),
                                rel_bias_table.reshape(-1), lq, row(diff_subln_g[j]), lam_init)
            h = _outproj(u.reshape(n, cc), o.reshape(n, vw), bf(ab_w_out[j]), h)
        else:
            h3 = h.reshape(bsz, t, d)
            w1 = bf(jnp.concatenate([rwkv_w1[j, 0], rwkv_w1[j, 1]], axis=1))
            a1 = bf(jnp.concatenate([rwkv_a1[j, 0], rwkv_a1[j, 1]], axis=1))
            r, v, kn, gate, bonus, kd, cum, b = _rwkv_prep(
                h3, row(norm_mix[i]), rwkv_mu[j], bf(rwkv_w_r[j]), bf(rwkv_w_k[j]), bf(rwkv_w_v[j]),
                w1, bf(_pad_lora_out(rwkv_w2[j])), rwkv_w0[j], a1, bf(_pad_lora_out(rwkv_a2[j])), rwkv_a0[j],
                bf(rwkv_g1[j]), bf(rwkv_g2[j]), row(rwkv_k_k[j]), row(rwkv_k_a[j]), row(rwkv_r_k[j]),
                sel, selt)
            yf, yb = _wkv7_scan(r, v, kn, kd, cum, b)
            h = _rwkv_post(yf.reshape(n, d), yb.reshape(n, d), bonus.reshape(n, d), gate.reshape(n, d), h,
                           row(rwkv_ln_g[j]), row(rwkv_ln_b[j]), bf(rwkv_w_o[j]), sel, selt)
        kv = _norm_linear(mem2, row(norm_mem[i]), bf(xattn_w_kv[i]), BF16)
        h = _cross_attention(h.reshape(bsz, t, d), kv.reshape(bsz, n_mem, 2 * d), row(norm_xattn[i]),
                             bf(xattn_w_q[i]), bf(xattn_w_o[i])).reshape(n, d)
        h = _mlp(h, row(norm_ffn[i]), bf(ffn_w_up[i]), bf(ffn_w_down[i]), row(norm_final),
                 final_norm=(i == depth - 1))
    return h.reshape(bsz, t, d)
```

```python
import functools
import math

import jax
import jax.numpy as jnp
from jax import lax
from jax.experimental import pallas as pl
from jax.experimental.pallas import tpu as pltpu

F32 = jnp.float32
BF16 = jnp.bfloat16

V7X_VMEM_BYTES = 64 * 1024 * 1024
VMEM_LIMIT_BYTES = V7X_VMEM_BYTES - 8 * 1024 * 1024

LOG2E = math.log2(math.e)
NORM_EPS = 1e-6
CONV_LN_EPS = 1e-5
GN_EPS = 64e-5
CONV_WIDTH = 31
CONV_PAD = CONV_WIDTH // 2
CONV_HALO = 16
DIFF_HEADS = 4
DIFF_HEAD_DIM = 64
REL_BUCKETS = 32
REL_MAX_DIST = 128
XATTN_HEADS = 4
RWKV_HEAD_DIM = 64
SUBLANES = 8
LANES = 128
CHUNK = 64
CHUNK_SHIFT = CHUNK.bit_length() - 1
PAIR = 2 * RWKV_HEAD_DIM


def _cparams(*sem):
    return pltpu.CompilerParams(dimension_semantics=sem, vmem_limit_bytes=VMEM_LIMIT_BYTES)


def _resident(shape):
    nd = len(shape)
    return pl.BlockSpec(shape, lambda *_: (0,) * nd, pipeline_mode=pl.Buffered(1))


def _rms(x, g):
    ms = jnp.mean(x * x, axis=-1, keepdims=True)
    return x * lax.rsqrt(ms + NORM_EPS) * g


def _sigmoid(x):
    return 1.0 / (1.0 + jnp.exp(-x))


def _dot(a, b):
    return jnp.dot(a.astype(BF16), b.astype(BF16), preferred_element_type=F32)


def _dot_nt(a, b):
    return lax.dot_general(a.astype(BF16), b.astype(BF16), (((1,), (1,)), ((), ())),
                           preferred_element_type=F32)


def _split_dot(mat, x):
    hi = x.astype(BF16)
    lo = (x - hi.astype(F32)).astype(BF16)
    return (jnp.dot(mat, hi, preferred_element_type=F32) + jnp.dot(mat, lo, preferred_element_type=F32))


def _run_staggered(gens, offset):
    results = [None] * len(gens)
    done = 0
    rnd = 0
    while done < len(gens):
        for i, gen in enumerate(gens):
            if rnd >= i * offset and results[i] is None:
                try:
                    next(gen)
                except StopIteration as stop:
                    results[i] = stop.value
                    done += 1
        rnd += 1
    return results


def _inproj_kernel(h_ref, g_ref, w_ref, u_ref, q_ref, k_ref, v_ref, *, cc, qk, scale):
    xn = _rms(h_ref[...], g_ref[...])
    p = _dot(xn, w_ref[...])
    u_ref[...] = p[:, :cc] * _sigmoid(p[:, cc:2 * cc])
    o = 2 * cc
    q_ref[...] = (p[:, o:o + qk] * scale).astype(BF16)
    k_ref[...] = p[:, o + qk:o + 2 * qk].astype(BF16)
    v_ref[...] = p[:, o + 2 * qk:].astype(BF16)


def _inproj(h, g, w, cc, qk, vw, tm=512):
    n, d = h.shape
    row = lambda i: (i, 0)
    return pl.pallas_call(
        functools.partial(_inproj_kernel, cc=cc, qk=qk, scale=DIFF_HEAD_DIM ** -0.5 * LOG2E),
        grid=(n // tm,),
        in_specs=[pl.BlockSpec((tm, d), row), _resident((1, d)), _resident(w.shape)],
        out_specs=[pl.BlockSpec((tm, cc), row), pl.BlockSpec((tm, qk), row),
                   pl.BlockSpec((tm, qk), row), pl.BlockSpec((tm, vw), row)],
        out_shape=[jax.ShapeDtypeStruct((n, cc), F32), jax.ShapeDtypeStruct((n, qk), BF16),
                   jax.ShapeDtypeStruct((n, qk), BF16), jax.ShapeDtypeStruct((n, vw), BF16)],
        compiler_params=_cparams("parallel"),
        name="l0_inproj",
    )(h, g, w)


def _conv_kernel(u_ref, w_ref, b_ref, g_ref, beta_ref, o_ref, xp_ref, *, t, rows):
    c = u_ref.shape[-1]
    zeros = jnp.zeros((CONV_HALO, c), F32)
    xp_ref[0:CONV_HALO, :] = zeros
    xp_ref[CONV_HALO + t:, :] = zeros
    xp_ref[CONV_HALO:CONV_HALO + t, :] = u_ref[0]
    off = CONV_HALO - CONV_PAD

    def body(i, carry):
        base = pl.multiple_of(i * rows, rows)
        span = rows + 2 * CONV_HALO
        pieces = []
        for c0 in range(0, c, LANES):
            win = xp_ref[pl.ds(base, span), c0:c0 + LANES]
            acc = jnp.zeros((rows, LANES), F32)
            for rem in range(SUBLANES):
                shifted = win if rem == 0 else pltpu.roll(win, span - rem, 0)
                for start in range(0, 2 * CONV_HALO, SUBLANES):
                    k = start + rem - off
                    if 0 <= k < CONV_WIDTH:
                        acc = acc + shifted[start:start + rows, :] * w_ref[k:k + 1, c0:c0 + LANES]
            pieces.append(acc)
        y = jnp.concatenate(pieces, axis=1) + b_ref[...]
        mu = jnp.mean(y, axis=-1, keepdims=True)
        yc = y - mu
        var = jnp.mean(yc * yc, axis=-1, keepdims=True)
        yn = yc * lax.rsqrt(var + CONV_LN_EPS) * g_ref[...] + beta_ref[...]
        o_ref[0, pl.ds(base, rows), :] = (yn * _sigmoid(yn)).astype(BF16)
        return carry

    lax.fori_loop(0, t // rows, body, 0)


def _conv_module(u, w, b, g, beta, rows=64):
    bsz, t, c = u.shape
    return pl.pallas_call(
        functools.partial(_conv_kernel, t=t, rows=rows),
        grid=(bsz,),
        in_specs=[pl.BlockSpec((1, t, c), lambda i: (i, 0, 0)), _resident(w.shape),
                  _resident((1, c)), _resident((1, c)), _resident((1, c))],
        out_specs=pl.BlockSpec((1, t, c), lambda i: (i, 0, 0)),
        out_shape=jax.ShapeDtypeStruct((bsz, t, c), BF16),
        scratch_shapes=[pltpu.VMEM((t + 2 * CONV_HALO, c), F32)],
        compiler_params=_cparams("parallel"),
        name="l0_conv",
    )(u, w, b, g, beta)


def _t5_bucket(rel):
    nb = REL_BUCKETS // 2
    max_exact = nb // 2
    n = jnp.abs(rel)
    large = jnp.full(rel.shape, max_exact, jnp.int32)
    steps = nb - max_exact
    for m in range(1, steps):
        thr = math.ceil(max_exact * (REL_MAX_DIST / max_exact) ** (m / steps) - 1e-9)
        large = large + jnp.where(n >= thr, 1, 0)
    mag = jnp.where(n < max_exact, n, large)
    return mag + jnp.where(rel > 0, nb, 0)


def _diffattn_kernel(tbl_ref, lq_ref, sg_ref, q_ref, k_ref, v_ref, o_ref, bias_ref, *, tq, ts, kb, lam_init):
    h = pl.program_id(0)
    qi = pl.program_id(1)
    b = pl.program_id(2)
    t = k_ref.shape[1]

    @pl.when(b == 0)
    def _():
        u = lax.broadcasted_iota(jnp.int32, (1, t + tq), 1)
        bucket = _t5_bucket(u - (tq - 1) - qi * tq)
        line = jnp.zeros((1, t + tq), F32)
        for i in range(REL_BUCKETS):
            line = jnp.where(bucket == i, tbl_ref[i * DIFF_HEADS + h], line)
        rows = pltpu.roll(jnp.broadcast_to(line * LOG2E, (tq, t + tq)), 1, 1, stride=1, stride_axis=0)
        bias_ref[...] = rows[:, tq:]

    lq = lq_ref[...]
    lam = (jnp.exp(jnp.sum(lq[0:1] * lq[1:2], axis=-1, keepdims=True))
           - jnp.exp(jnp.sum(lq[2:3] * lq[3:4], axis=-1, keepdims=True)) + lam_init)
    hw = q_ref.shape[-1]
    first = lax.broadcasted_iota(jnp.int32, (ts, hw), 1) < DIFF_HEAD_DIM
    nkb = t // kb

    def softmax_v(r0, comp):
        q = q_ref[0, r0:r0 + ts, :]
        qc = jnp.where(first, q, jnp.zeros_like(q)) if comp == 0 else jnp.where(first, jnp.zeros_like(q), q)
        s = []
        mx = None
        for j in range(nkb):
            sj = _dot_nt(qc, k_ref[0, j * kb:(j + 1) * kb, :]) + bias_ref[r0:r0 + ts, j * kb:(j + 1) * kb]
            mj = jnp.max(sj, axis=-1, keepdims=True)
            mx = mj if mx is None else jnp.maximum(mx, mj)
            s.append(sj)
            yield
        pv = None
        for j in range(nkb):
            vj = v_ref[0, j * kb:(j + 1) * kb, :]
            dj = _dot(jnp.exp2(s[j] - mx), jnp.concatenate([vj, jnp.ones_like(vj)], axis=1))
            pv = dj if pv is None else pv + dj
            yield
        return pv[:, :hw] / pv[:, hw:]

    starts = range(0, tq, ts)
    maps = _run_staggered([softmax_v(r0, comp) for r0 in starts for comp in range(2)], nkb)
    for i, r0 in enumerate(starts):
        o = maps[2 * i] - lam * maps[2 * i + 1]
        o = o * lax.rsqrt(jnp.mean(o * o, axis=-1, keepdims=True) + NORM_EPS) * sg_ref[...] * (1.0 - lam_init)
        o_ref[0, r0:r0 + ts, :] = o.astype(BF16)


def _diff_attention(q, k, v, tbl, lq, sg, lam_init, tq=512, ts=256, kb=512):
    bsz, t, _ = q.shape
    hw = 2 * DIFF_HEAD_DIM
    return pl.pallas_call(
        functools.partial(_diffattn_kernel, tq=tq, ts=ts, kb=kb, lam_init=lam_init),
        grid=(DIFF_HEADS, t // tq, bsz),
        in_specs=[pl.BlockSpec(memory_space=pltpu.SMEM), _resident(lq.shape), _resident(sg.shape),
                  pl.BlockSpec((1, tq, hw), lambda h, i, b: (b, i, h)),
                  pl.BlockSpec((1, t, hw), lambda h, i, b: (b, 0, h)),
                  pl.BlockSpec((1, t, hw), lambda h, i, b: (b, 0, h))],
        out_specs=pl.BlockSpec((1, tq, hw), lambda h, i, b: (b, i, h)),
        out_shape=jax.ShapeDtypeStruct(q.shape, BF16),
        scratch_shapes=[pltpu.VMEM((tq, t), F32)],
        compiler_params=_cparams("parallel", "parallel", "arbitrary"),
        name="l0_diffattn",
    )(tbl, lq, sg, q, k, v)


def _outproj_kernel(u_ref, o_ref, w_ref, h_ref, out_ref):
    cc = u_ref.shape[-1]
    out_ref[...] = (h_ref[...] + _dot(u_ref[...], w_ref[:cc, :]) + _dot(o_ref[...], w_ref[cc:, :]))


def _outproj(u, o, w, h, tm=512):
    n, d = h.shape
    row = lambda i: (i, 0)
    return pl.pallas_call(
        _outproj_kernel,
        grid=(n // tm,),
        in_specs=[pl.BlockSpec((tm, u.shape[1]), row), pl.BlockSpec((tm, o.shape[1]), row),
                  _resident(w.shape), pl.BlockSpec((tm, d), row)],
        out_specs=pl.BlockSpec((tm, d), row),
        out_shape=jax.ShapeDtypeStruct((n, d), F32),
        compiler_params=_cparams("parallel"),
        name="l0_outproj",
    )(u, o, w, h)


def _norm_linear_kernel(x_ref, g_ref, w_ref, o_ref):
    o_ref[...] = _dot(_rms(x_ref[...], g_ref[...]), w_ref[...]).astype(o_ref.dtype)


def _norm_linear(x, g, w, out_dtype, tm=512):
    n, d = x.shape
    m = w.shape[1]
    return pl.pallas_call(
        _norm_linear_kernel,
        grid=(n // tm,),
        in_specs=[pl.BlockSpec((tm, d), lambda i: (i, 0)), _resident((1, d)), _resident(w.shape)],
        out_specs=pl.BlockSpec((tm, m), lambda i: (i, 0)),
        out_shape=jax.ShapeDtypeStruct((n, m), out_dtype),
        compiler_params=_cparams("parallel"),
        name="norm_linear",
    )(x, g, w)


def _xattn_kernel(h_ref, g_ref, wq_ref, kv_ref, wo_ref, o_ref):
    d = h_ref.shape[-1]
    hd = d // XATTN_HEADS
    h = h_ref[0]
    q = (_dot(_rms(h, g_ref[...]), wq_ref[...]) * (hd ** -0.5)).astype(BF16)
    outs = []
    for i in range(XATTN_HEADS):
        kh = kv_ref[0, :, i * hd:(i + 1) * hd]
        vh = kv_ref[0, :, d + i * hd:d + (i + 1) * hd]
        s = _dot_nt(q[:, i * hd:(i + 1) * hd], kh)
        m = jnp.max(s, axis=-1, keepdims=True)
        e = jnp.exp(s - m)
        p = e / jnp.sum(e, axis=-1, keepdims=True)
        outs.append(_dot(p, vh).astype(BF16))
    o_ref[0] = h + _dot(jnp.concatenate(outs, axis=1), wo_ref[...])


def _cross_attention(h, kv, g, wq, wo, tq=512):
    bsz, t, d = h.shape
    m = kv.shape[1]
    return pl.pallas_call(
        _xattn_kernel,
        grid=(bsz, t // tq),
        in_specs=[pl.BlockSpec((1, tq, d), lambda b, i: (b, i, 0)), _resident((1, d)), _resident(wq.shape),
                  pl.BlockSpec((1, m, 2 * d), lambda b, i: (b, 0, 0)), _resident(wo.shape)],
        out_specs=pl.BlockSpec((1, tq, d), lambda b, i: (b, i, 0)),
        out_shape=jax.ShapeDtypeStruct(h.shape, F32),
        compiler_params=_cparams("parallel", "parallel"),
        name="xattn",
    )(h, g, wq, kv, wo)


def _mlp_kernel(h_ref, g_ref, wu_ref, wd_ref, gf_ref, o_ref, *, hc, final_norm):
    h = h_ref[...]
    xn = _rms(h, g_ref[...]).astype(BF16)
    acc = h
    for c in range(wu_ref.shape[1] // hc):
        a = jnp.maximum(_dot(xn, wu_ref[:, c * hc:(c + 1) * hc]), 0.0)
        acc = acc + _dot(a * a, wd_ref[c * hc:(c + 1) * hc, :])
    if final_norm:
        acc = _rms(acc, gf_ref[...])
    o_ref[...] = acc


def _mlp(h, g, wu, wd, gf, final_norm, tm=512, hc=1024):
    n, d = h.shape
    row = lambda i: (i, 0)
    return pl.pallas_call(
        functools.partial(_mlp_kernel, hc=hc, final_norm=final_norm),
        grid=(n // tm,),
        in_specs=[pl.BlockSpec((tm, d), row), _resident((1, d)), _resident(wu.shape), _resident(wd.shape),
                  _resident((1, d))],
        out_specs=pl.BlockSpec((tm, d), row),
        out_shape=jax.ShapeDtypeStruct((n, d), F32),
        compiler_params=_cparams("parallel"),
        name="mlp",
    )(h, g, wu, wd, gf)


def _head_sum(x, sel_ref, selt_ref):
    return _dot(_dot(x, sel_ref[...]), selt_ref[...])


def _rwkv_prep_kernel(h_ref, hp_ref, hn_ref, g_ref, mu_ref, wr_ref, wk_ref, wv_ref, w1_ref, w2_ref, w0_ref,
                      a1_ref, a2_ref, a0_ref, g1_ref, g2_ref, kk_ref, ka_ref, rk_ref, sel_ref, selt_ref,
                      r_out, v_out, kn_out, gate_out, bonus_out, kd_out, cum_out, b_out, *, ts):
    i = pl.program_id(1)
    last = pl.num_programs(1) - 1
    tm = h_ref.shape[1]
    g = g_ref[...]
    ti = lax.broadcasted_iota(jnp.int32, (ts, ts), 0)
    si = lax.broadcasted_iota(jnp.int32, (ts, ts), 1)
    same_chunk = (ti >> CHUNK_SHIFT) == (si >> CHUNK_SHIFT)
    before = (jnp.where(same_chunk, jnp.where(si <= ti, 1.0, 0.0), 0.0).astype(BF16),
              jnp.where(same_chunk, jnp.where(si >= ti, 1.0, 0.0), 0.0).astype(BF16))

    def rows(r0):
        x = _rms(h_ref[0, r0:r0 + ts, :], g)
        if r0 == 0:
            prev_row = _rms(hp_ref[0], g)[SUBLANES - 1:, :] * jnp.where(i > 0, 1.0, 0.0)
        else:
            prev_row = _rms(h_ref[0, r0 - SUBLANES:r0, :], g)[SUBLANES - 1:, :]
        if r0 + ts == tm:
            next_row = _rms(hn_ref[0], g)[0:1, :] * jnp.where(i < last, 1.0, 0.0)
        else:
            next_row = _rms(h_ref[0, r0 + ts:r0 + ts + SUBLANES, :], g)[0:1, :]
        rowid = lax.broadcasted_iota(jnp.int32, x.shape, 0)
        x_prev = jnp.where(rowid == 0, prev_row, pltpu.roll(x, 1, 0))
        x_next = jnp.where(rowid == ts - 1, next_row, pltpu.roll(x, ts - 1, 0))
        hh = 0.5 * (x_prev + x_next) - x
        mix = lambda j: x + hh * mu_ref[j:j + 1, :]
        r = _dot(mix(0), wr_ref[...])
        k = _dot(mix(2), wk_ref[...])
        v = _dot(mix(3), wv_ref[...])
        yield
        gate_in = _dot(mix(5), g1_ref[...])
        lw = _dot(mix(1), w1_ref[...])
        la = _dot(mix(4), a1_ref[...])
        yield
        gate = _dot(_sigmoid(gate_in), g2_ref[...])
        lw = jnp.tanh(lw)
        w_pre = [w0_ref[z:z + 1, :] + _dot(lw, w2_ref[z]) for z in range(2)]
        a_pre = [a0_ref[z:z + 1, :] + _dot(la, a2_ref[z]) for z in range(2)]
        kk = k * kk_ref[...]
        ss = _dot(kk * kk, sel_ref[...])
        yield
        kn = kk * lax.rsqrt(jnp.maximum(_dot(ss, selt_ref[...]), 1e-24))
        kka = k * ka_ref[...]
        kd_sum = jnp.zeros_like(k)
        cum = []
        for z in range(2):
            cum.append(_split_dot(before[z], _sigmoid(w_pre[z]) * (-math.exp(-0.5))))
            rate = _sigmoid(a_pre[z])
            kd = k + kka * (rate - 1.0)
            kd_out[z, 0, r0:r0 + ts, :] = kd.astype(kd_out.dtype)
            b_out[z, 0, r0:r0 + ts, :] = (kn * rate).astype(b_out.dtype)
            kd_sum = kd_sum + kd
        bs = _dot(r * kd_sum * rk_ref[...], sel_ref[...])
        yield
        r_out[0, r0:r0 + ts, :] = r.astype(r_out.dtype)
        v_out[0, r0:r0 + ts, :] = v.astype(v_out.dtype)
        kn_out[0, r0:r0 + ts, :] = kn.astype(kn_out.dtype)
        gate_out[0, r0:r0 + ts, :] = gate.astype(gate_out.dtype)
        bonus_out[0, r0:r0 + ts, :] = (_dot(bs, selt_ref[...]) * v).astype(bonus_out.dtype)
        for z in range(2):
            cum_out[z, 0, r0:r0 + ts, :] = cum[z]

    _run_staggered([rows(r0) for r0 in range(0, tm, ts)], 1)


def _rwkv_prep(h, g, mu, wr, wk, wv, w1, w2, w0, a1, a2, a0, g1, g2, kk, ka, rk, sel, selt, tm=256, ts=128):
    bsz, t, d = h.shape
    nb = tm // SUBLANES
    tile = pl.BlockSpec((1, tm, d), lambda b, i: (b, i, 0))
    tile2 = pl.BlockSpec((2, 1, tm, d), lambda b, i: (0, b, i, 0))
    one = jax.ShapeDtypeStruct((bsz, t, d), BF16)
    two = jax.ShapeDtypeStruct((2, bsz, t, d), BF16)
    consts = [g, mu, wr, wk, wv, w1, w2, w0, a1, a2, a0, g1, g2, kk, ka, rk, sel, selt]
    return pl.pallas_call(
        functools.partial(_rwkv_prep_kernel, ts=ts),
        grid=(bsz, t // tm),
        in_specs=[tile,
                  pl.BlockSpec((1, SUBLANES, d), lambda b, i: (b, jnp.maximum(i * nb - 1, 0), 0)),
                  pl.BlockSpec((1, SUBLANES, d),
                               lambda b, i: (b, jnp.minimum((i + 1) * nb, t // SUBLANES - 1), 0))]
                 + [_resident(c.shape) for c in consts],
        out_specs=[tile, tile, tile, tile, tile, tile2, tile2, tile2],
        out_shape=[one, one, one, one, one, two, jax.ShapeDtypeStruct((2, bsz, t, d), F32), two],
        compiler_params=_cparams("parallel", "parallel"),
        name="l1_rwkv_prep",
    )(h, h, h, *consts)


def _blockdiag(x):
    lane = lax.broadcasted_iota(jnp.int32, x.shape, 1)
    head0 = (lane & (PAIR - 1)) < RWKV_HEAD_DIM
    zero = jnp.zeros_like(x)
    return jnp.concatenate([jnp.where(head0, x, zero), jnp.where(head0, zero, x)], axis=0)


def _chunk_local(r, k, v, kn, cum, b, rev):
    c = CHUNK
    bd = _blockdiag
    row = lax.broadcasted_iota(jnp.int32, (c, PAIR), 0)
    lane = lax.broadcasted_iota(jnp.int32, (c, PAIR), 1)
    s_idx = lane & (c - 1)
    if rev:
        cum_prev = jnp.where(row == c - 1, 0.0, pltpu.roll(cum, c - 1, 0))
        tot = cum[0:1, :]
        strict = s_idx > row
        incl = s_idx >= row
    else:
        cum_prev = jnp.where(row == 0, 0.0, pltpu.roll(cum, 1, 0))
        tot = cum[c - 1:c, :]
        strict = s_idx < row
        incl = s_idx <= row
    w_incl = jnp.exp(cum)
    w_excl = jnp.exp(cum_prev)
    w_inv = jnp.exp(-cum)
    w_tot = jnp.exp(tot)
    w_rest = jnp.exp(tot - cum)
    a_t = -kn * w_excl
    r_t = r * w_incl
    b_t = b * w_inv
    k_t = k * w_inv
    b_h = b * w_rest
    k_h = k * w_rest
    same_blk = (s_idx >> 4) == (row >> 4)

    sc = _dot_nt(jnp.concatenate([a_t, r_t], axis=0), jnp.concatenate([bd(b_t), bd(k_t)], axis=0))
    yield
    p_ab = jnp.where(strict, sc[:c, :PAIR], 0.0)
    p_ak = jnp.where(strict, sc[:c, PAIR:], 0.0)
    p_rb = jnp.where(incl, sc[c:, :PAIR], 0.0)
    p_rk = jnp.where(incl, sc[c:, PAIR:], 0.0)
    dm = jnp.where(same_blk, p_ab, 0.0)
    em = p_ab - dm
    x2 = _dot(dm, bd(dm))
    av = _dot(p_ak, bd(v))
    yield
    td = jnp.where(s_idx == row, 1.0, 0.0) + dm
    x4 = _dot(x2, bd(x2))
    td = td + _dot(td, bd(x2))
    yield
    x8 = _dot(x4, bd(x4))
    td = td + _dot(td, bd(x4))
    yield
    td = td + _dot(td, bd(x8))
    yield
    ty = _dot(td, bd(jnp.concatenate([a_t, av, em], axis=1)))
    yield
    au, f1 = ty[:, :2 * PAIR], ty[:, 2 * PAIR:]
    f2 = _dot(f1, bd(f1))
    au = au + _dot(f1, bd(au))
    yield
    au = au + _dot(f2, bd(au))
    yield
    rhs = jnp.concatenate([bd(au), jnp.concatenate([jnp.zeros((PAIR, PAIR), F32), bd(v)], axis=1)], axis=0)
    ry = _dot(jnp.concatenate([p_rb, p_rk], axis=1), rhs)
    rhs2 = jnp.concatenate([au, jnp.concatenate([jnp.zeros((c, PAIR), F32), v], axis=1)], axis=0)
    mg = _dot(jnp.concatenate([b_h, k_h], axis=0).T, rhs2)
    yield
    rbar = r_t + ry[:, :PAIR]
    yloc = ry[:, PAIR:]
    r2 = lax.broadcasted_iota(jnp.int32, (PAIR, PAIR), 0)
    l2 = lax.broadcasted_iota(jnp.int32, (PAIR, PAIR), 1)
    same_head = (r2 >> 6) == (l2 >> 6)
    m = jnp.where(same_head, mg[:, :PAIR], 0.0) + jnp.where(r2 == l2, w_tot, 0.0)
    gg = jnp.where(same_head, mg[:, PAIR:], 0.0)
    return rbar, yloc, m, gg


def _scan_kernel(r_ref, v_ref, kn_ref, kd_ref, cum_ref, b_ref, yf_ref, yb_ref, ds_ref, *, unroll):
    nc = r_ref.shape[1] // CHUNK
    ds_ref[...] = jnp.zeros(ds_ref.shape, F32)
    y_refs = (yf_ref, yb_ref)

    def body(it, carry):
        where = []
        gens = []
        for u in range(unroll):
            for z in range(2):
                cidx = it * unroll + u
                if z == 1:
                    cidx = nc - 1 - cidx
                sl = pl.ds(pl.multiple_of(cidx * CHUNK, CHUNK), CHUNK)
                where.append((z, sl))
                f32 = lambda ref, *idx: ref[idx].astype(F32)
                gens.append(_chunk_local(f32(r_ref, 0, sl), f32(kd_ref, z, 0, sl), f32(v_ref, 0, sl),
                                         f32(kn_ref, 0, sl), cum_ref[z, 0, sl, :], f32(b_ref, z, 0, sl),
                                         rev=(z == 1)))
        ds = [ds_ref[0], ds_ref[1]]
        for (z, sl), (rbar, yloc, m, gg) in zip(where, _run_staggered(gens, 0)):
            y_refs[z][0, sl, :] = _dot(rbar, ds[z]) + yloc
            ds[z] = _dot(m, ds[z]) + gg
        ds_ref[0] = ds[0]
        ds_ref[1] = ds[1]
        return carry

    lax.fori_loop(0, nc // unroll, body, 0)


def _wkv7_scan(r, v, kn, kd, cum, b, unroll=8):
    bsz, t, d = r.shape
    one = pl.BlockSpec((1, t, PAIR), lambda bb, p: (bb, 0, p))
    two = pl.BlockSpec((2, 1, t, PAIR), lambda bb, p: (0, bb, 0, p))
    out = jax.ShapeDtypeStruct((bsz, t, d), F32)
    return pl.pallas_call(
        functools.partial(_scan_kernel, unroll=unroll),
        grid=(bsz, d // PAIR),
        in_specs=[one, one, one, two, two, two],
        out_specs=[one, one],
        out_shape=[out, out],
        scratch_shapes=[pltpu.VMEM((2, PAIR, PAIR), F32)],
        compiler_params=_cparams("parallel", "parallel"),
        name="l1_wkv7_scan",
    )(r, v, kn, kd, cum, b)


def _rwkv_post_kernel(yf_ref, yb_ref, bonus_ref, gate_ref, h_ref, lg_ref, lb_ref, wo_ref, sel_ref, selt_ref,
                      o_ref):
    y = yf_ref[...] + yb_ref[...]
    inv_n = 1.0 / RWKV_HEAD_DIM
    mu = _head_sum(y, sel_ref, selt_ref) * inv_n
    yc = y - mu
    var = _head_sum(yc * yc, sel_ref, selt_ref) * inv_n
    yn = yc * lax.rsqrt(var + GN_EPS) * lg_ref[...] + lb_ref[...]
    o_ref[...] = h_ref[...] + _dot((yn + bonus_ref[...]) * gate_ref[...], wo_ref[...])


def _rwkv_post(yf, yb, bonus, gate, h, lg, lb, wo, sel, selt, tm=512):
    n, d = h.shape
    row = pl.BlockSpec((tm, d), lambda i: (i, 0))
    return pl.pallas_call(
        _rwkv_post_kernel,
        grid=(n // tm,),
        in_specs=[row, row, row, row, row, _resident((1, d)), _resident((1, d)), _resident(wo.shape),
                  _resident(sel.shape), _resident(selt.shape)],
        out_specs=row,
        out_shape=jax.ShapeDtypeStruct((n, d), F32),
        compiler_params=_cparams("parallel"),
        name="l1_rwkv_post",
    )(yf, yb, bonus, gate, h, lg, lb, wo, sel, selt)


def _pad_lora_out(w2):
    zero = jnp.zeros_like(w2[0])
    return jnp.stack([jnp.concatenate([w2[0], zero], axis=0), jnp.concatenate([zero, w2[1]], axis=0)])


def kernel(x, mem, rel_bias_table, norm_mix, norm_xattn, norm_mem, norm_ffn, norm_final, ab_w_in, ab_w_out, conv_w, conv_b, conv_ln_g, conv_ln_b, diff_lq1, diff_lk1, diff_lq2, diff_lk2, diff_subln_g, rwkv_mu, rwkv_w_r, rwkv_w_k, rwkv_w_v, rwkv_w_o, rwkv_w0, rwkv_w1, rwkv_w2, rwkv_a0, rwkv_a1, rwkv_a2, rwkv_g1, rwkv_g2, rwkv_k_k, rwkv_k_a, rwkv_r_k, rwkv_ln_g, rwkv_ln_b, xattn_w_q, xattn_w_kv, xattn_w_o, ffn_w_up, ffn_w_down):
    bsz, t, d = x.shape
    n = bsz * t
    depth = norm_mix.shape[0]
    n_mem = mem.shape[1]
    cc = conv_w.shape[-1]
    qk = DIFF_HEADS * 2 * DIFF_HEAD_DIM
    vw = ab_w_in.shape[-1] - 2 * cc - 2 * qk
    bf = lambda w: w.astype(BF16)
    row = lambda w: w.reshape(1, -1)

    heads = d // RWKV_HEAD_DIM
    head_of = jnp.arange(d, dtype=jnp.int32) // RWKV_HEAD_DIM
    sel = (head_of[:, None] == jnp.arange(128, dtype=jnp.int32)[None, :]).astype(BF16)
    selt = sel.T
    assert heads <= 128

    h = x.reshape(n, d)
    mem2 = mem.reshape(bsz * n_mem, d)
    for i in range(depth):
        j = i // 2
        if i % 2 == 0:
            lam_init = 0.8 - 0.6 * math.exp(-0.3 * i)
            u, q, k, v = _inproj(h, row(norm_mix[i]), bf(ab_w_in[j]), cc, qk, vw)
            u = _conv_module(u.reshape(bsz, t, cc), conv_w[j], row(conv_b[j]), row(conv_ln_g[j]),
                             row(conv_ln_b[j]))
            lq = jnp.stack([diff_lq1[j], diff_lk1[j], diff_lq2[j], diff_lk2[j]])
            o = _diff_attention(q.reshape(bsz, t, qk), k.reshape(bsz, t, qk), v.reshape(bsz, t, vw),
                                rel_bias_table.reshape(-1), lq, row(diff_subln_g[j]), lam_init)
            h = _outproj(u.reshape(n, cc), o.reshape(n, vw), bf(ab_w_out[j]), h)
        else:
            h3 = h.reshape(bsz, t, d)
            w1 = bf(jnp.concatenate([rwkv_w1[j, 0], rwkv_w1[j, 1]], axis=1))
            a1 = bf(jnp.concatenate([rwkv_a1[j, 0], rwkv_a1[j, 1]], axis=1))
            r, v, kn, gate, bonus, kd, cum, b = _rwkv_prep(
                h3, row(norm_mix[i]), rwkv_mu[j], bf(rwkv_w_r[j]), bf(rwkv_w_k[j]), bf(rwkv_w_v[j]),
                w1, bf(_pad_lora_out(rwkv_w2[j])), rwkv_w0[j], a1, bf(_pad_lora_out(rwkv_a2[j])), rwkv_a0[j],
                bf(rwkv_g1[j]), bf(rwkv_g2[j]), row(rwkv_k_k[j]), row(rwkv_k_a[j]), row(rwkv_r_k[j]),
                sel, selt)
            yf, yb = _wkv7_scan(r, v, kn, kd, cum, b)
            h = _rwkv_post(yf.reshape(n, d), yb.reshape(n, d), bonus.reshape(n, d), gate.reshape(n, d), h,
                           row(rwkv_ln_g[j]), row(rwkv_ln_b[j]), bf(rwkv_w_o[j]), sel, selt)
        kv = _norm_linear(mem2, row(norm_mem[i]), bf(xattn_w_kv[i]), BF16)
        h = _cross_attention(h.reshape(bsz, t, d), kv.reshape(bsz, n_mem, 2 * d), row(norm_xattn[i]),
                             bf(xattn_w_q[i]), bf(xattn_w_o[i])).reshape(n, d)
        h = _mlp(h, row(norm_ffn[i]), bf(ffn_w_up[i]), bf(ffn_w_down[i]), row(norm_final),
                 final_norm=(i == depth - 1))
    return h.reshape(bsz, t, d)
```

```python
import functools
import math

import jax
import jax.numpy as jnp
from jax import lax
from jax.experimental import pallas as pl
from jax.experimental.pallas import tpu as pltpu

F32 = jnp.float32
BF16 = jnp.bfloat16

V7X_VMEM_BYTES = 64 * 1024 * 1024
VMEM_LIMIT_BYTES = V7X_VMEM_BYTES - 8 * 1024 * 1024

LOG2E = math.log2(math.e)
NORM_EPS = 1e-6
CONV_LN_EPS = 1e-5
GN_EPS = 64e-5
CONV_WIDTH = 31
CONV_PAD = CONV_WIDTH // 2
CONV_HALO = 16
DIFF_HEADS = 4
DIFF_HEAD_DIM = 64
REL_BUCKETS = 32
REL_MAX_DIST = 128
XATTN_HEADS = 4
RWKV_HEAD_DIM = 64
SUBLANES = 8
LANES = 128
CHUNK = 64
CHUNK_SHIFT = CHUNK.bit_length() - 1
PAIR = 2 * RWKV_HEAD_DIM


def _cparams(*sem):
    return pltpu.CompilerParams(dimension_semantics=sem, vmem_limit_bytes=VMEM_LIMIT_BYTES)


def _resident(shape):
    nd = len(shape)
    return pl.BlockSpec(shape, lambda *_: (0,) * nd, pipeline_mode=pl.Buffered(1))


def _rms(x, g):
    ms = jnp.mean(x * x, axis=-1, keepdims=True)
    return x * lax.rsqrt(ms + NORM_EPS) * g


def _sigmoid(x):
    return 1.0 / (1.0 + jnp.exp(-x))


def _dot(a, b):
    return jnp.dot(a.astype(BF16), b.astype(BF16), preferred_element_type=F32)


def _dot_nt(a, b):
    return lax.dot_general(a.astype(BF16), b.astype(BF16), (((1,), (1,)), ((), ())),
                           preferred_element_type=F32)


def _split_dot(mat, x):
    hi = x.astype(BF16)
    lo = (x - hi.astype(F32)).astype(BF16)
    return (jnp.dot(mat, hi, preferred_element_type=F32) + jnp.dot(mat, lo, preferred_element_type=F32))


def _run_staggered(gens, offset):
    results = [None] * len(gens)
    running = [True] * len(gens)
    rnd = 0
    while any(running):
        for i, gen in enumerate(gens):
            if rnd >= i * offset and running[i]:
                try:
                    next(gen)
                except StopIteration as stop:
                    results[i] = stop.value
                    running[i] = False
        rnd += 1
    return results


def _inproj_kernel(h_ref, g_ref, w_ref, u_ref, q_ref, k_ref, v_ref, *, cc, qk, scale):
    xn = _rms(h_ref[...], g_ref[...])
    p = _dot(xn, w_ref[...])
    u_ref[...] = p[:, :cc] * _sigmoid(p[:, cc:2 * cc])
    o = 2 * cc
    q_ref[...] = (p[:, o:o + qk] * scale).astype(BF16)
    k_ref[...] = p[:, o + qk:o + 2 * qk].astype(BF16)
    v_ref[...] = p[:, o + 2 * qk:].astype(BF16)


def _inproj(h, g, w, cc, qk, vw, tm=512):
    n, d = h.shape
    row = lambda i: (i, 0)
    return pl.pallas_call(
        functools.partial(_inproj_kernel, cc=cc, qk=qk, scale=DIFF_HEAD_DIM ** -0.5 * LOG2E),
        grid=(n // tm,),
        in_specs=[pl.BlockSpec((tm, d), row), _resident((1, d)), _resident(w.shape)],
        out_specs=[pl.BlockSpec((tm, cc), row), pl.BlockSpec((tm, qk), row),
                   pl.BlockSpec((tm, qk), row), pl.BlockSpec((tm, vw), row)],
        out_shape=[jax.ShapeDtypeStruct((n, cc), F32), jax.ShapeDtypeStruct((n, qk), BF16),
                   jax.ShapeDtypeStruct((n, qk), BF16), jax.ShapeDtypeStruct((n, vw), BF16)],
        compiler_params=_cparams("parallel"),
        name="l0_inproj",
    )(h, g, w)


def _conv_kernel(u_ref, w_ref, b_ref, g_ref, beta_ref, o_ref, xp_ref, *, t, rows):
    c = u_ref.shape[-1]
    zeros = jnp.zeros((CONV_HALO, c), F32)
    xp_ref[0:CONV_HALO, :] = zeros
    xp_ref[CONV_HALO + t:, :] = zeros
    xp_ref[CONV_HALO:CONV_HALO + t, :] = u_ref[0]
    off = CONV_HALO - CONV_PAD

    def body(i, carry):
        base = pl.multiple_of(i * rows, rows)
        span = rows + 2 * CONV_HALO
        pieces = []
        for c0 in range(0, c, LANES):
            win = xp_ref[pl.ds(base, span), c0:c0 + LANES]
            acc = jnp.zeros((rows, LANES), F32)
            for rem in range(SUBLANES):
                shifted = win if rem == 0 else pltpu.roll(win, span - rem, 0)
                for start in range(0, 2 * CONV_HALO, SUBLANES):
                    k = start + rem - off
                    if 0 <= k < CONV_WIDTH:
                        acc = acc + shifted[start:start + rows, :] * w_ref[k:k + 1, c0:c0 + LANES]
            pieces.append(acc)
        y = jnp.concatenate(pieces, axis=1) + b_ref[...]
        mu = jnp.mean(y, axis=-1, keepdims=True)
        yc = y - mu
        var = jnp.mean(yc * yc, axis=-1, keepdims=True)
        yn = yc * lax.rsqrt(var + CONV_LN_EPS) * g_ref[...] + beta_ref[...]
        o_ref[0, pl.ds(base, rows), :] = (yn * _sigmoid(yn)).astype(BF16)
        return carry

    lax.fori_loop(0, t // rows, body, 0)


def _conv_module(u, w, b, g, beta, rows=64):
    bsz, t, c = u.shape
    return pl.pallas_call(
        functools.partial(_conv_kernel, t=t, rows=rows),
        grid=(bsz,),
        in_specs=[pl.BlockSpec((1, t, c), lambda i: (i, 0, 0)), _resident(w.shape),
                  _resident((1, c)), _resident((1, c)), _resident((1, c))],
        out_specs=pl.BlockSpec((1, t, c), lambda i: (i, 0, 0)),
        out_shape=jax.ShapeDtypeStruct((bsz, t, c), BF16),
        scratch_shapes=[pltpu.VMEM((t + 2 * CONV_HALO, c), F32)],
        compiler_params=_cparams("parallel"),
        name="l0_conv",
    )(u, w, b, g, beta)


def _t5_bucket(rel):
    nb = REL_BUCKETS // 2
    max_exact = nb // 2
    n = jnp.abs(rel)
    large = jnp.full(rel.shape, max_exact, jnp.int32)
    steps = nb - max_exact
    for m in range(1, steps):
        thr = math.ceil(max_exact * (REL_MAX_DIST / max_exact) ** (m / steps) - 1e-9)
        large = large + jnp.where(n >= thr, 1, 0)
    mag = jnp.where(n < max_exact, n, large)
    return mag + jnp.where(rel > 0, nb, 0)


def _diffattn_kernel(tbl_ref, lq_ref, sg_ref, q_ref, k_ref, v_ref, o_ref, bias_ref, *, tq, ts, kb, lam_init):
    h = pl.program_id(0)
    qi = pl.program_id(1)
    b = pl.program_id(2)
    t = k_ref.shape[1]

    @pl.when(b == 0)
    def _():
        u = lax.broadcasted_iota(jnp.int32, (1, t + tq), 1)
        bucket = _t5_bucket(u - (tq - 1) - qi * tq)
        line = jnp.zeros((1, t + tq), F32)
        for i in range(REL_BUCKETS):
            line = jnp.where(bucket == i, tbl_ref[i * DIFF_HEADS + h], line)
        rows = pltpu.roll(jnp.broadcast_to(line * LOG2E, (tq, t + tq)), 1, 1, stride=1, stride_axis=0)
        bias_ref[...] = rows[:, tq:]

    lq = lq_ref[...]
    lam = (jnp.exp(jnp.sum(lq[0:1] * lq[1:2], axis=-1, keepdims=True))
           - jnp.exp(jnp.sum(lq[2:3] * lq[3:4], axis=-1, keepdims=True)) + lam_init)
    hw = q_ref.shape[-1]
    first = lax.broadcasted_iota(jnp.int32, (ts, hw), 1) < DIFF_HEAD_DIM
    nkb = t // kb

    def softmax_v(r0, comp):
        q = q_ref[0, r0:r0 + ts, :]
        qc = jnp.where(first, q, jnp.zeros_like(q)) if comp == 0 else jnp.where(first, jnp.zeros_like(q), q)
        s = []
        mx = None
        for j in range(nkb):
            sj = _dot_nt(qc, k_ref[0, j * kb:(j + 1) * kb, :]) + bias_ref[r0:r0 + ts, j * kb:(j + 1) * kb]
            mj = jnp.max(sj, axis=-1, keepdims=True)
            mx = mj if mx is None else jnp.maximum(mx, mj)
            s.append(sj)
            yield
        pv = None
        for j in range(nkb):
            vj = v_ref[0, j * kb:(j + 1) * kb, :]
            dj = _dot(jnp.exp2(s[j] - mx), jnp.concatenate([vj, jnp.ones_like(vj)], axis=1))
            pv = dj if pv is None else pv + dj
            yield
        return pv[:, :hw] / pv[:, hw:]

    starts = range(0, tq, ts)
    maps = _run_staggered([softmax_v(r0, comp) for r0 in starts for comp in range(2)], nkb)
    for i, r0 in enumerate(starts):
        o = maps[2 * i] - lam * maps[2 * i + 1]
        o = o * lax.rsqrt(jnp.mean(o * o, axis=-1, keepdims=True) + NORM_EPS) * sg_ref[...] * (1.0 - lam_init)
        o_ref[0, r0:r0 + ts, :] = o.astype(BF16)


def _diff_attention(q, k, v, tbl, lq, sg, lam_init, tq=512, ts=256, kb=512):
    bsz, t, _ = q.shape
    hw = 2 * DIFF_HEAD_DIM
    return pl.pallas_call(
        functools.partial(_diffattn_kernel, tq=tq, ts=ts, kb=kb, lam_init=lam_init),
        grid=(DIFF_HEADS, t // tq, bsz),
        in_specs=[pl.BlockSpec(memory_space=pltpu.SMEM), _resident(lq.shape), _resident(sg.shape),
                  pl.BlockSpec((1, tq, hw), lambda h, i, b: (b, i, h)),
                  pl.BlockSpec((1, t, hw), lambda h, i, b: (b, 0, h)),
                  pl.BlockSpec((1, t, hw), lambda h, i, b: (b, 0, h))],
        out_specs=pl.BlockSpec((1, tq, hw), lambda h, i, b: (b, i, h)),
        out_shape=jax.ShapeDtypeStruct(q.shape, BF16),
        scratch_shapes=[pltpu.VMEM((tq, t), F32)],
        compiler_params=_cparams("parallel", "parallel", "arbitrary"),
        name="l0_diffattn",
    )(tbl, lq, sg, q, k, v)


def _outproj_kernel(u_ref, o_ref, w_ref, h_ref, out_ref):
    cc = u_ref.shape[-1]
    out_ref[...] = (h_ref[...] + _dot(u_ref[...], w_ref[:cc, :]) + _dot(o_ref[...], w_ref[cc:, :]))


def _outproj(u, o, w, h, tm=512):
    n, d = h.shape
    row = lambda i: (i, 0)
    return pl.pallas_call(
        _outproj_kernel,
        grid=(n // tm,),
        in_specs=[pl.BlockSpec((tm, u.shape[1]), row), pl.BlockSpec((tm, o.shape[1]), row),
                  _resident(w.shape), pl.BlockSpec((tm, d), row)],
        out_specs=pl.BlockSpec((tm, d), row),
        out_shape=jax.ShapeDtypeStruct((n, d), F32),
        compiler_params=_cparams("parallel"),
        name="l0_outproj",
    )(u, o, w, h)


def _norm_linear_kernel(x_ref, g_ref, w_ref, o_ref):
    o_ref[...] = _dot(_rms(x_ref[...], g_ref[...]), w_ref[...]).astype(o_ref.dtype)


def _norm_linear(x, g, w, out_dtype, tm=512):
    n, d = x.shape
    m = w.shape[1]
    return pl.pallas_call(
        _norm_linear_kernel,
        grid=(n // tm,),
        in_specs=[pl.BlockSpec((tm, d), lambda i: (i, 0)), _resident((1, d)), _resident(w.shape)],
        out_specs=pl.BlockSpec((tm, m), lambda i: (i, 0)),
        out_shape=jax.ShapeDtypeStruct((n, m), out_dtype),
        compiler_params=_cparams("parallel"),
        name="norm_linear",
    )(x, g, w)


def _xattn_kernel(h_ref, g_ref, wq_ref, kv_ref, wo_ref, o_ref):
    d = h_ref.shape[-1]
    hd = d // XATTN_HEADS
    h = h_ref[0]
    q = (_dot(_rms(h, g_ref[...]), wq_ref[...]) * (hd ** -0.5)).astype(BF16)
    outs = []
    for i in range(XATTN_HEADS):
        kh = kv_ref[0, :, i * hd:(i + 1) * hd]
        vh = kv_ref[0, :, d + i * hd:d + (i + 1) * hd]
        s = _dot_nt(q[:, i * hd:(i + 1) * hd], kh)
        m = jnp.max(s, axis=-1, keepdims=True)
        e = jnp.exp(s - m)
        p = e / jnp.sum(e, axis=-1, keepdims=True)
        outs.append(_dot(p, vh).astype(BF16))
    o_ref[0] = h + _dot(jnp.concatenate(outs, axis=1), wo_ref[...])


def _cross_attention(h, kv, g, wq, wo, tq=512):
    bsz, t, d = h.shape
    m = kv.shape[1]
    return pl.pallas_call(
        _xattn_kernel,
        grid=(bsz, t // tq),
        in_specs=[pl.BlockSpec((1, tq, d), lambda b, i: (b, i, 0)), _resident((1, d)), _resident(wq.shape),
                  pl.BlockSpec((1, m, 2 * d), lambda b, i: (b, 0, 0)), _resident(wo.shape)],
        out_specs=pl.BlockSpec((1, tq, d), lambda b, i: (b, i, 0)),
        out_shape=jax.ShapeDtypeStruct(h.shape, F32),
        compiler_params=_cparams("parallel", "parallel"),
        name="xattn",
    )(h, g, wq, kv, wo)


def _mlp_kernel(h_ref, g_ref, wu_ref, wd_ref, gf_ref, o_ref, *, hc, final_norm):
    h = h_ref[...]
    xn = _rms(h, g_ref[...]).astype(BF16)
    acc = h
    for c in range(wu_ref.shape[1] // hc):
        a = jnp.maximum(_dot(xn, wu_ref[:, c * hc:(c + 1) * hc]), 0.0)
        acc = acc + _dot(a * a, wd_ref[c * hc:(c + 1) * hc, :])
    if final_norm:
        acc = _rms(acc, gf_ref[...])
    o_ref[...] = acc


def _mlp(h, g, wu, wd, gf, final_norm, tm=512, hc=1024):
    n, d = h.shape
    row = lambda i: (i, 0)
    return pl.pallas_call(
        functools.partial(_mlp_kernel, hc=hc, final_norm=final_norm),
        grid=(n // tm,),
        in_specs=[pl.BlockSpec((tm, d), row), _resident((1, d)), _resident(wu.shape), _resident(wd.shape),
                  _resident((1, d))],
        out_specs=pl.BlockSpec((tm, d), row),
        out_shape=jax.ShapeDtypeStruct((n, d), F32),
        compiler_params=_cparams("parallel"),
        name="mlp",
    )(h, g, wu, wd, gf)


def _head_sum(x, sel_ref, selt_ref):
    return _dot(_dot(x, sel_ref[...]), selt_ref[...])


def _rwkv_prep_kernel(h_ref, hp_ref, hn_ref, g_ref, mu_ref, wr_ref, wk_ref, wv_ref, w1_ref, w2_ref, w0_ref,
                      a1_ref, a2_ref, a0_ref, g1_ref, g2_ref, kk_ref, ka_ref, rk_ref, sel_ref, selt_ref,
                      r_out, v_out, kn_out, gate_out, bonus_out, kd_out, cum_out, b_out, *, ts):
    i = pl.program_id(1)
    last = pl.num_programs(1) - 1
    tm = h_ref.shape[1]
    g = g_ref[...]
    ti = lax.broadcasted_iota(jnp.int32, (ts, ts), 0)
    si = lax.broadcasted_iota(jnp.int32, (ts, ts), 1)
    same_chunk = (ti >> CHUNK_SHIFT) == (si >> CHUNK_SHIFT)
    before = (jnp.where(same_chunk, jnp.where(si <= ti, 1.0, 0.0), 0.0).astype(BF16),
              jnp.where(same_chunk, jnp.where(si >= ti, 1.0, 0.0), 0.0).astype(BF16))

    def rows(r0):
        x = _rms(h_ref[0, r0:r0 + ts, :], g)
        if r0 == 0:
            prev_row = _rms(hp_ref[0], g)[SUBLANES - 1:, :] * jnp.where(i > 0, 1.0, 0.0)
        else:
            prev_row = _rms(h_ref[0, r0 - SUBLANES:r0, :], g)[SUBLANES - 1:, :]
        if r0 + ts == tm:
            next_row = _rms(hn_ref[0], g)[0:1, :] * jnp.where(i < last, 1.0, 0.0)
        else:
            next_row = _rms(h_ref[0, r0 + ts:r0 + ts + SUBLANES, :], g)[0:1, :]
        rowid = lax.broadcasted_iota(jnp.int32, x.shape, 0)
        x_prev = jnp.where(rowid == 0, prev_row, pltpu.roll(x, 1, 0))
        x_next = jnp.where(rowid == ts - 1, next_row, pltpu.roll(x, ts - 1, 0))
        hh = 0.5 * (x_prev + x_next) - x
        mix = lambda j: x + hh * mu_ref[j:j + 1, :]
        r = _dot(mix(0), wr_ref[...])
        k = _dot(mix(2), wk_ref[...])
        v = _dot(mix(3), wv_ref[...])
        yield
        gate_in = _dot(mix(5), g1_ref[...])
        lw = _dot(mix(1), w1_ref[...])
        la = _dot(mix(4), a1_ref[...])
        yield
        gate = _dot(_sigmoid(gate_in), g2_ref[...])
        lw = jnp.tanh(lw)
        w_pre = [w0_ref[z:z + 1, :] + _dot(lw, w2_ref[z]) for z in range(2)]
        a_pre = [a0_ref[z:z + 1, :] + _dot(la, a2_ref[z]) for z in range(2)]
        kk = k * kk_ref[...]
        ss = _dot(kk * kk, sel_ref[...])
        yield
        kn = kk * lax.rsqrt(jnp.maximum(_dot(ss, selt_ref[...]), 1e-24))
        kka = k * ka_ref[...]
        kd_sum = jnp.zeros_like(k)
        cum = []
        for z in range(2):
            cum.append(_split_dot(before[z], _sigmoid(w_pre[z]) * (-math.exp(-0.5))))
            rate = _sigmoid(a_pre[z])
            kd = k + kka * (rate - 1.0)
            kd_out[z, 0, r0:r0 + ts, :] = kd.astype(kd_out.dtype)
            b_out[z, 0, r0:r0 + ts, :] = (kn * rate).astype(b_out.dtype)
            kd_sum = kd_sum + kd
        bs = _dot(r * kd_sum * rk_ref[...], sel_ref[...])
        yield
        r_out[0, r0:r0 + ts, :] = r.astype(r_out.dtype)
        v_out[0, r0:r0 + ts, :] = v.astype(v_out.dtype)
        kn_out[0, r0:r0 + ts, :] = kn.astype(kn_out.dtype)
        gate_out[0, r0:r0 + ts, :] = gate.astype(gate_out.dtype)
        bonus_out[0, r0:r0 + ts, :] = (_dot(bs, selt_ref[...]) * v).astype(bonus_out.dtype)
        for z in range(2):
            cum_out[z, 0, r0:r0 + ts, :] = cum[z]

    _run_staggered([rows(r0) for r0 in range(0, tm, ts)], 1)


def _rwkv_prep(h, g, mu, wr, wk, wv, w1, w2, w0, a1, a2, a0, g1, g2, kk, ka, rk, sel, selt, tm=256, ts=128):
    bsz, t, d = h.shape
    nb = tm // SUBLANES
    tile = pl.BlockSpec((1, tm, d), lambda b, i: (b, i, 0))
    tile2 = pl.BlockSpec((2, 1, tm, d), lambda b, i: (0, b, i, 0))
    one = jax.ShapeDtypeStruct((bsz, t, d), BF16)
    two = jax.ShapeDtypeStruct((2, bsz, t, d), BF16)
    consts = [g, mu, wr, wk, wv, w1, w2, w0, a1, a2, a0, g1, g2, kk, ka, rk, sel, selt]
    return pl.pallas_call(
        functools.partial(_rwkv_prep_kernel, ts=ts),
        grid=(bsz, t // tm),
        in_specs=[tile,
                  pl.BlockSpec((1, SUBLANES, d), lambda b, i: (b, jnp.maximum(i * nb - 1, 0), 0)),
                  pl.BlockSpec((1, SUBLANES, d),
                               lambda b, i: (b, jnp.minimum((i + 1) * nb, t // SUBLANES - 1), 0))]
                 + [_resident(c.shape) for c in consts],
        out_specs=[tile, tile, tile, tile, tile, tile2, tile2, tile2],
        out_shape=[one, one, one, one, one, two, jax.ShapeDtypeStruct((2, bsz, t, d), F32), two],
        compiler_params=_cparams("parallel", "parallel"),
        name="l1_rwkv_prep",
    )(h, h, h, *consts)


def _blockdiag(x):
    lane = lax.broadcasted_iota(jnp.int32, x.shape, 1)
    head0 = (lane & (PAIR - 1)) < RWKV_HEAD_DIM
    zero = jnp.zeros_like(x)
    return jnp.concatenate([jnp.where(head0, x, zero), jnp.where(head0, zero, x)], axis=0)


def _chunk_local(r, k, v, kn, cum, b, rev):
    c = CHUNK
    bd = _blockdiag
    row = lax.broadcasted_iota(jnp.int32, (c, PAIR), 0)
    lane = lax.broadcasted_iota(jnp.int32, (c, PAIR), 1)
    s_idx = lane & (c - 1)
    if rev:
        cum_prev = jnp.where(row == c - 1, 0.0, pltpu.roll(cum, c - 1, 0))
        tot = cum[0:1, :]
        strict = s_idx > row
        incl = s_idx >= row
    else:
        cum_prev = jnp.where(row == 0, 0.0, pltpu.roll(cum, 1, 0))
        tot = cum[c - 1:c, :]
        strict = s_idx < row
        incl = s_idx <= row
    w_incl = jnp.exp(cum)
    w_excl = jnp.exp(cum_prev)
    w_inv = jnp.exp(-cum)
    w_tot = jnp.exp(tot)
    w_rest = jnp.exp(tot - cum)
    a_t = -kn * w_excl
    r_t = r * w_incl
    b_t = b * w_inv
    k_t = k * w_inv
    b_h = b * w_rest
    k_h = k * w_rest
    same_blk = (s_idx >> 4) == (row >> 4)

    sc = _dot_nt(jnp.concatenate([a_t, r_t], axis=0), jnp.concatenate([bd(b_t), bd(k_t)], axis=0))
    yield
    p_ab = jnp.where(strict, sc[:c, :PAIR], 0.0)
    p_ak = jnp.where(strict, sc[:c, PAIR:], 0.0)
    p_rb = jnp.where(incl, sc[c:, :PAIR], 0.0)
    p_rk = jnp.where(incl, sc[c:, PAIR:], 0.0)
    dm = jnp.where(same_blk, p_ab, 0.0)
    em = p_ab - dm
    x2 = _dot(dm, bd(dm))
    av = _dot(p_ak, bd(v))
    yield
    td = jnp.where(s_idx == row, 1.0, 0.0) + dm
    both = _dot(jnp.concatenate([x2, td], axis=0), bd(x2))
    yield
    x4 = both[:c]
    td = td + both[c:]
    both = _dot(jnp.concatenate([x4, td], axis=0), bd(x4))
    yield
    td = td + both[c:]
    td = td + _dot(td, bd(both[:c]))
    yield
    ty = _dot(td, bd(jnp.concatenate([a_t, av, em], axis=1)))
    yield
    au, f1 = ty[:, :2 * PAIR], ty[:, 2 * PAIR:]
    both = _dot(f1, bd(jnp.concatenate([f1, au], axis=1)))
    f2 = both[:, :PAIR]
    au = au + both[:, PAIR:]
    yield
    au = au + _dot(f2, bd(au))
    yield
    rhs = jnp.concatenate([bd(au), jnp.concatenate([jnp.zeros((PAIR, PAIR), F32), bd(v)], axis=1)], axis=0)
    ry = _dot(jnp.concatenate([p_rb, p_rk], axis=1), rhs)
    rhs2 = jnp.concatenate([au, jnp.concatenate([jnp.zeros((c, PAIR), F32), v], axis=1)], axis=0)
    mg = _dot(jnp.concatenate([b_h, k_h], axis=0).T, rhs2)
    yield
    rbar = r_t + ry[:, :PAIR]
    yloc = ry[:, PAIR:]
    r2 = lax.broadcasted_iota(jnp.int32, (PAIR, PAIR), 0)
    l2 = lax.broadcasted_iota(jnp.int32, (PAIR, PAIR), 1)
    same_head = (r2 >> 6) == (l2 >> 6)
    m = jnp.where(same_head, mg[:, :PAIR], 0.0) + jnp.where(r2 == l2, w_tot, 0.0)
    gg = jnp.where(same_head, mg[:, PAIR:], 0.0)
    return rbar, yloc, m, gg


def _scan_kernel(r_ref, v_ref, kn_ref, kd_ref, cum_ref, b_ref, yf_ref, yb_ref, ds_ref, loc_a, loc_b, *, unroll):
    nc = r_ref.shape[1] // CHUNK
    groups = nc // unroll
    y_refs = (yf_ref, yb_ref)
    c = CHUNK
    ds_ref[...] = jnp.zeros(ds_ref.shape, F32)
    loc_b[...] = jnp.zeros(loc_b.shape, F32)

    def places(grp):
        out = []
        for u in range(unroll):
            for z in range(2):
                cidx = grp * unroll + u
                if z == 1:
                    cidx = nc - 1 - cidx
                out.append((z, pl.ds(pl.multiple_of(cidx * c, c), c)))
        return out

    def local_terms(grp, loc_ref):
        f32 = lambda ref, *idx: ref[idx].astype(F32)
        gens = [_chunk_local(f32(r_ref, 0, sl), f32(kd_ref, z, 0, sl), f32(v_ref, 0, sl), f32(kn_ref, 0, sl),
                             cum_ref[z, 0, sl, :], f32(b_ref, z, 0, sl), rev=(z == 1))
                for z, sl in places(grp)]

        def park(j, gen):
            rbar, yloc, m, gg = yield from gen
            loc_ref[j, 0:c, :] = rbar
            loc_ref[j, c:2 * c, :] = yloc
            loc_ref[j, 2 * c:2 * c + PAIR, :] = m
            loc_ref[j, 2 * c + PAIR:, :] = gg

        return [park(j, gen) for j, gen in enumerate(gens)]

    def recurrence(grp, loc_ref):
        ds = [ds_ref[0], ds_ref[1]]
        for j, (z, sl) in enumerate(places(grp)):
            both = _dot(jnp.concatenate([loc_ref[j, 0:c, :], loc_ref[j, 2 * c:2 * c + PAIR, :]], axis=0), ds[z])
            y_refs[z][0, sl, :] = both[:c] + loc_ref[j, c:2 * c, :]
            ds[z] = both[c:] + loc_ref[j, 2 * c + PAIR:, :]
            if z == 1:
                yield
        ds_ref[0] = ds[0]
        ds_ref[1] = ds[1]

    def body(it, carry):
        first = 2 * it
        _run_staggered(local_terms(first, loc_a) + [recurrence(jnp.maximum(first - 1, 0), loc_b)], 0)
        _run_staggered(local_terms(first + 1, loc_b) + [recurrence(first, loc_a)], 0)
        return carry

    lax.fori_loop(0, groups // 2, body, 0)
    _run_staggered([recurrence(groups - 1, loc_b)], 0)


def _wkv7_scan(r, v, kn, kd, cum, b, unroll=8):
    bsz, t, d = r.shape
    nc = t // CHUNK
    unroll = min(unroll, nc // 2)
    assert nc % (2 * unroll) == 0, "the scan kernel takes chunk groups in pairs"
    one = pl.BlockSpec((1, t, PAIR), lambda bb, p: (bb, 0, p))
    two = pl.BlockSpec((2, 1, t, PAIR), lambda bb, p: (0, bb, 0, p))
    out = jax.ShapeDtypeStruct((bsz, t, d), F32)
    return pl.pallas_call(
        functools.partial(_scan_kernel, unroll=unroll),
        grid=(bsz, d // PAIR),
        in_specs=[one, one, one, two, two, two],
        out_specs=[one, one],
        out_shape=[out, out],
        scratch_shapes=[pltpu.VMEM((2, PAIR, PAIR), F32)]
                       + [pltpu.VMEM((2 * unroll, 2 * CHUNK + 2 * PAIR, PAIR), F32)] * 2,
        compiler_params=_cparams("parallel", "parallel"),
        name="l1_wkv7_scan",
    )(r, v, kn, kd, cum, b)


def _rwkv_post_kernel(yf_ref, yb_ref, bonus_ref, gate_ref, h_ref, lg_ref, lb_ref, wo_ref, sel_ref, selt_ref,
                      o_ref):
    y = yf_ref[...] + yb_ref[...]
    inv_n = 1.0 / RWKV_HEAD_DIM
    mu = _head_sum(y, sel_ref, selt_ref) * inv_n
    yc = y - mu
    var = _head_sum(yc * yc, sel_ref, selt_ref) * inv_n
    yn = yc * lax.rsqrt(var + GN_EPS) * lg_ref[...] + lb_ref[...]
    o_ref[...] = h_ref[...] + _dot((yn + bonus_ref[...]) * gate_ref[...], wo_ref[...])


def _rwkv_post(yf, yb, bonus, gate, h, lg, lb, wo, sel, selt, tm=512):
    n, d = h.shape
    row = pl.BlockSpec((tm, d), lambda i: (i, 0))
    return pl.pallas_call(
        _rwkv_post_kernel,
        grid=(n // tm,),
        in_specs=[row, row, row, row, row, _resident((1, d)), _resident((1, d)), _resident(wo.shape),
                  _resident(sel.shape), _resident(selt.shape)],
        out_specs=row,
        out_shape=jax.ShapeDtypeStruct((n, d), F32),
        compiler_params=_cparams("parallel"),
        name="l1_rwkv_post",
    )(yf, yb, bonus, gate, h, lg, lb, wo, sel, selt)


def _pad_lora_out(w2):
    zero = jnp.zeros_like(w2[0])
    return jnp.stack([jnp.concatenate([w2[0], zero], axis=0), jnp.concatenate([zero, w2[1]], axis=0)])


def kernel(x, mem, rel_bias_table, norm_mix, norm_xattn, norm_mem, norm_ffn, norm_final, ab_w_in, ab_w_out, conv_w, conv_b, conv_ln_g, conv_ln_b, diff_lq1, diff_lk1, diff_lq2, diff_lk2, diff_subln_g, rwkv_mu, rwkv_w_r, rwkv_w_k, rwkv_w_v, rwkv_w_o, rwkv_w0, rwkv_w1, rwkv_w2, rwkv_a0, rwkv_a1, rwkv_a2, rwkv_g1, rwkv_g2, rwkv_k_k, rwkv_k_a, rwkv_r_k, rwkv_ln_g, rwkv_ln_b, xattn_w_q, xattn_w_kv, xattn_w_o, ffn_w_up, ffn_w_down):
    bsz, t, d = x.shape
    n = bsz * t
    depth = norm_mix.shape[0]
    n_mem = mem.shape[1]
    cc = conv_w.shape[-1]
    qk = DIFF_HEADS * 2 * DIFF_HEAD_DIM
    vw = ab_w_in.shape[-1] - 2 * cc - 2 * qk
    bf = lambda w: w.astype(BF16)
    row = lambda w: w.reshape(1, -1)

    heads = d // RWKV_HEAD_DIM
    head_of = jnp.arange(d, dtype=jnp.int32) // RWKV_HEAD_DIM
    sel = (head_of[:, None] == jnp.arange(128, dtype=jnp.int32)[None, :]).astype(BF16)
    selt = sel.T
    assert heads <= 128

    h = x.reshape(n, d)
    mem2 = mem.reshape(bsz * n_mem, d)
    for i in range(depth):
        j = i // 2
        if i % 2 == 0:
            lam_init = 0.8 - 0.6 * math.exp(-0.3 * i)
            u, q, k, v = _inproj(h, row(norm_mix[i]), bf(ab_w_in[j]), cc, qk, vw)
            u = _conv_module(u.reshape(bsz, t, cc), conv_w[j], row(conv_b[j]), row(conv_ln_g[j]),
                             row(conv_ln_b[j]))
            lq = jnp.stack([diff_lq1[j], diff_lk1[j], diff_lq2[j], diff_lk2[j]])
            o = _diff_attention(q.reshape(bsz, t, qk), k.reshape(bsz, t, qk), v.reshape(bsz, t, vw),
                                rel_bias_table.reshape(-1), lq, row(diff_subln_g[j]), lam_init)
            h = _outproj(u.reshape(n, cc), o.reshape(n, vw), bf(ab_w_out[j]), h)
        else:
            h3 = h.reshape(bsz, t, d)
            w1 = bf(jnp.concatenate([rwkv_w1[j, 0], rwkv_w1[j, 1]], axis=1))
            a1 = bf(jnp.concatenate([rwkv_a1[j, 0], rwkv_a1[j, 1]], axis=1))
            r, v, kn, gate, bonus, kd, cum, b = _rwkv_prep(
                h3, row(norm_mix[i]), rwkv_mu[j], bf(rwkv_w_r[j]), bf(rwkv_w_k[j]), bf(rwkv_w_v[j]),
                w1, bf(_pad_lora_out(rwkv_w2[j])), rwkv_w0[j], a1, bf(_pad_lora_out(rwkv_a2[j])), rwkv_a0[j],
                bf(rwkv_g1[j]), bf(rwkv_g2[j]), row(rwkv_k_k[j]), row(rwkv_k_a[j]), row(rwkv_r_k[j]),
                sel, selt)
            yf, yb = _wkv7_scan(r, v, kn, kd, cum, b)
            h = _rwkv_post(yf.reshape(n, d), yb.reshape(n, d), bonus.reshape(n, d), gate.reshape(n, d), h,
                           row(rwkv_ln_g[j]), row(rwkv_ln_b[j]), bf(rwkv_w_o[j]), sel, selt)
        kv = _norm_linear(mem2, row(norm_mem[i]), bf(xattn_w_kv[i]), BF16)
        h = _cross_attention(h.reshape(bsz, t, d), kv.reshape(bsz, n_mem, 2 * d), row(norm_xattn[i]),
                             bf(xattn_w_q[i]), bf(xattn_w_o[i])).reshape(n, d)
        h = _mlp(h, row(norm_ffn[i]), bf(ffn_w_up[i]), bf(ffn_w_down[i]), row(norm_final),
                 final_norm=(i == depth - 1))
    return h.reshape(bsz, t, d)
```

```python
import functools
import math

import jax
import jax.numpy as jnp
from jax import lax
from jax.experimental import pallas as pl
from jax.experimental.pallas import tpu as pltpu

F32 = jnp.float32
BF16 = jnp.bfloat16

V7X_VMEM_BYTES = 64 * 1024 * 1024
VMEM_LIMIT_BYTES = V7X_VMEM_BYTES - 8 * 1024 * 1024

LOG2E = math.log2(math.e)
NORM_EPS = 1e-6
CONV_LN_EPS = 1e-5
GN_EPS = 64e-5
CONV_WIDTH = 31
CONV_PAD = CONV_WIDTH // 2
CONV_HALO = 16
DIFF_HEADS = 4
DIFF_HEAD_DIM = 64
REL_BUCKETS = 32
REL_MAX_DIST = 128
XATTN_HEADS = 4
RWKV_HEAD_DIM = 64
SUBLANES = 8
LANES = 128
CHUNK = 64
CHUNK_SHIFT = CHUNK.bit_length() - 1
PAIR = 2 * RWKV_HEAD_DIM


def _cparams(*sem):
    return pltpu.CompilerParams(dimension_semantics=sem, vmem_limit_bytes=VMEM_LIMIT_BYTES)


def _resident(shape):
    nd = len(shape)
    return pl.BlockSpec(shape, lambda *_: (0,) * nd, pipeline_mode=pl.Buffered(1))


def _rms(x, g):
    ms = jnp.mean(x * x, axis=-1, keepdims=True)
    return x * lax.rsqrt(ms + NORM_EPS) * g


def _sigmoid(x):
    return 1.0 / (1.0 + jnp.exp(-x))


def _dot(a, b):
    return jnp.dot(a.astype(BF16), b.astype(BF16), preferred_element_type=F32)


def _dot_nt(a, b):
    return lax.dot_general(a.astype(BF16), b.astype(BF16), (((1,), (1,)), ((), ())),
                           preferred_element_type=F32)


def _split_dot(mat, x):
    hi = x.astype(BF16)
    lo = (x - hi.astype(F32)).astype(BF16)
    return (jnp.dot(mat, hi, preferred_element_type=F32) + jnp.dot(mat, lo, preferred_element_type=F32))


def _run_staggered(gens, offset):
    results = [None] * len(gens)
    running = [True] * len(gens)
    rnd = 0
    while any(running):
        for i, gen in enumerate(gens):
            if rnd >= i * offset and running[i]:
                try:
                    next(gen)
                except StopIteration as stop:
                    results[i] = stop.value
                    running[i] = False
        rnd += 1
    return results


def _inproj_kernel(h_ref, g_ref, w_ref, u_ref, q_ref, k_ref, v_ref, *, cc, qk, scale):
    xn = _rms(h_ref[...], g_ref[...])
    p = _dot(xn, w_ref[...])
    u_ref[...] = p[:, :cc] * _sigmoid(p[:, cc:2 * cc])
    o = 2 * cc
    q_ref[...] = (p[:, o:o + qk] * scale).astype(BF16)
    k_ref[...] = p[:, o + qk:o + 2 * qk].astype(BF16)
    v_ref[...] = p[:, o + 2 * qk:].astype(BF16)


def _inproj(h, g, w, cc, qk, vw, tm=512):
    n, d = h.shape
    row = lambda i: (i, 0)
    return pl.pallas_call(
        functools.partial(_inproj_kernel, cc=cc, qk=qk, scale=DIFF_HEAD_DIM ** -0.5 * LOG2E),
        grid=(n // tm,),
        in_specs=[pl.BlockSpec((tm, d), row), _resident((1, d)), _resident(w.shape)],
        out_specs=[pl.BlockSpec((tm, cc), row), pl.BlockSpec((tm, qk), row),
                   pl.BlockSpec((tm, qk), row), pl.BlockSpec((tm, vw), row)],
        out_shape=[jax.ShapeDtypeStruct((n, cc), F32), jax.ShapeDtypeStruct((n, qk), BF16),
                   jax.ShapeDtypeStruct((n, qk), BF16), jax.ShapeDtypeStruct((n, vw), BF16)],
        compiler_params=_cparams("parallel"),
        name="l0_inproj",
    )(h, g, w)


def _conv_rows(xp_ref, base, rows, w_ref, b_ref, g_ref, beta_ref):
    c = xp_ref.shape[-1]
    off = CONV_HALO - CONV_PAD
    span = rows + 2 * CONV_HALO
    pieces = []
    for c0 in range(0, c, LANES):
        win = xp_ref[base:base + span, c0:c0 + LANES]
        acc = jnp.zeros((rows, LANES), F32)
        for rem in range(SUBLANES):
            shifted = win if rem == 0 else pltpu.roll(win, span - rem, 0)
            for start in range(0, 2 * CONV_HALO, SUBLANES):
                k = start + rem - off
                if 0 <= k < CONV_WIDTH:
                    acc = acc + shifted[start:start + rows, :] * w_ref[k:k + 1, c0:c0 + LANES]
        pieces.append(acc)
    y = jnp.concatenate(pieces, axis=1) + b_ref[...]
    mu = jnp.mean(y, axis=-1, keepdims=True)
    yc = y - mu
    var = jnp.mean(yc * yc, axis=-1, keepdims=True)
    yn = yc * lax.rsqrt(var + CONV_LN_EPS) * g_ref[...] + beta_ref[...]
    return (yn * _sigmoid(yn)).astype(BF16)


def _t5_bucket(rel):
    nb = REL_BUCKETS // 2
    max_exact = nb // 2
    n = jnp.abs(rel)
    large = jnp.full(rel.shape, max_exact, jnp.int32)
    steps = nb - max_exact
    for m in range(1, steps):
        thr = math.ceil(max_exact * (REL_MAX_DIST / max_exact) ** (m / steps) - 1e-9)
        large = large + jnp.where(n >= thr, 1, 0)
    mag = jnp.where(n < max_exact, n, large)
    return mag + jnp.where(rel > 0, nb, 0)


def _diffattn_kernel(tbl_ref, lq_ref, sg_ref, q_ref, k_ref, v_ref, o_ref, bias_ref, *, tq, ts, kb, lam_init):
    h = pl.program_id(0)
    qi = pl.program_id(1)
    b = pl.program_id(2)
    t = k_ref.shape[1]

    @pl.when(b == 0)
    def _():
        u = lax.broadcasted_iota(jnp.int32, (1, t + tq), 1)
        bucket = _t5_bucket(u - (tq - 1) - qi * tq)
        line = jnp.zeros((1, t + tq), F32)
        for i in range(REL_BUCKETS):
            line = jnp.where(bucket == i, tbl_ref[i * DIFF_HEADS + h], line)
        rows = pltpu.roll(jnp.broadcast_to(line * LOG2E, (tq, t + tq)), 1, 1, stride=1, stride_axis=0)
        bias_ref[...] = rows[:, tq:]

    lq = lq_ref[...]
    lam = (jnp.exp(jnp.sum(lq[0:1] * lq[1:2], axis=-1, keepdims=True))
           - jnp.exp(jnp.sum(lq[2:3] * lq[3:4], axis=-1, keepdims=True)) + lam_init)
    hw = q_ref.shape[-1]
    first = lax.broadcasted_iota(jnp.int32, (ts, hw), 1) < DIFF_HEAD_DIM
    nkb = t // kb

    def softmax_v(r0, comp):
        q = q_ref[0, r0:r0 + ts, :]
        qc = jnp.where(first, q, jnp.zeros_like(q)) if comp == 0 else jnp.where(first, jnp.zeros_like(q), q)
        s = []
        mx = None
        for j in range(nkb):
            sj = _dot_nt(qc, k_ref[0, j * kb:(j + 1) * kb, :]) + bias_ref[r0:r0 + ts, j * kb:(j + 1) * kb]
            mj = jnp.max(sj, axis=-1, keepdims=True)
            mx = mj if mx is None else jnp.maximum(mx, mj)
            s.append(sj)
            yield
        pv = None
        for j in range(nkb):
            vj = v_ref[0, j * kb:(j + 1) * kb, :]
            dj = _dot(jnp.exp2(s[j] - mx), jnp.concatenate([vj, jnp.ones_like(vj)], axis=1))
            pv = dj if pv is None else pv + dj
            yield
        return pv[:, :hw] / pv[:, hw:]

    starts = range(0, tq, ts)
    maps = _run_staggered([softmax_v(r0, comp) for r0 in starts for comp in range(2)], nkb)
    for i, r0 in enumerate(starts):
        o = maps[2 * i] - lam * maps[2 * i + 1]
        o = o * lax.rsqrt(jnp.mean(o * o, axis=-1, keepdims=True) + NORM_EPS) * sg_ref[...] * (1.0 - lam_init)
        o_ref[0, r0:r0 + ts, :] = o.astype(BF16)


def _diff_attention(q, k, v, tbl, lq, sg, lam_init, tq=512, ts=256, kb=512):
    bsz, t, _ = q.shape
    hw = 2 * DIFF_HEAD_DIM
    return pl.pallas_call(
        functools.partial(_diffattn_kernel, tq=tq, ts=ts, kb=kb, lam_init=lam_init),
        grid=(DIFF_HEADS, t // tq, bsz),
        in_specs=[pl.BlockSpec(memory_space=pltpu.SMEM), _resident(lq.shape), _resident(sg.shape),
                  pl.BlockSpec((1, tq, hw), lambda h, i, b: (b, i, h)),
                  pl.BlockSpec((1, t, hw), lambda h, i, b: (b, 0, h)),
                  pl.BlockSpec((1, t, hw), lambda h, i, b: (b, 0, h))],
        out_specs=pl.BlockSpec((1, tq, hw), lambda h, i, b: (b, i, h)),
        out_shape=jax.ShapeDtypeStruct(q.shape, BF16),
        scratch_shapes=[pltpu.VMEM((tq, t), F32)],
        compiler_params=_cparams("parallel", "parallel", "arbitrary"),
        name="l0_diffattn",
    )(tbl, lq, sg, q, k, v)


def _norm_linear_kernel(x_ref, g_ref, w_ref, o_ref):
    o_ref[...] = _dot(_rms(x_ref[...], g_ref[...]), w_ref[...]).astype(o_ref.dtype)


def _norm_linear(x, g, w, out_dtype, tm=512):
    n, d = x.shape
    m = w.shape[1]
    return pl.pallas_call(
        _norm_linear_kernel,
        grid=(n // tm,),
        in_specs=[pl.BlockSpec((tm, d), lambda i: (i, 0)), _resident((1, d)), _resident(w.shape)],
        out_specs=pl.BlockSpec((tm, m), lambda i: (i, 0)),
        out_shape=jax.ShapeDtypeStruct((n, m), out_dtype),
        compiler_params=_cparams("parallel"),
        name="norm_linear",
    )(x, g, w)


def _xattn_rows(h, g_ref, wq_ref, kv_ref, wo_ref):
    d = h.shape[-1]
    hd = d // XATTN_HEADS
    q = (_dot(_rms(h, g_ref[...]), wq_ref[...]) * (hd ** -0.5 * LOG2E)).astype(BF16)
    yield
    outs = []
    for i in range(XATTN_HEADS):
        kh = kv_ref[0, :, i * hd:(i + 1) * hd]
        vh = kv_ref[0, :, d + i * hd:d + (i + 1) * hd]
        s = _dot_nt(q[:, i * hd:(i + 1) * hd], kh)
        e = jnp.exp2(s - jnp.max(s, axis=-1, keepdims=True))
        p = e / jnp.sum(e, axis=-1, keepdims=True)
        outs.append(_dot(p, vh).astype(BF16))
    yield
    return h + _dot(jnp.concatenate(outs, axis=1), wo_ref[...])


def _l0_tail_kernel(u_ref, up_ref, un_ref, o_ref, h_ref, cw_ref, cb_ref, cg_ref, cbeta_ref, wout_ref,
                    gx_ref, wq_ref, kv_ref, wo_ref, out_ref, xp_ref, *, ts, rows):
    i = pl.program_id(1)
    last = pl.num_programs(1) - 1
    tq = h_ref.shape[1]
    cc = u_ref.shape[-1]
    xp_ref[0:CONV_HALO, :] = up_ref[0] * jnp.where(i > 0, 1.0, 0.0)
    xp_ref[CONV_HALO:CONV_HALO + tq, :] = u_ref[0]
    xp_ref[CONV_HALO + tq:, :] = un_ref[0] * jnp.where(i < last, 1.0, 0.0)

    def sub(r0):
        pieces = []
        for t0 in range(r0, r0 + ts, rows):
            pieces.append(_conv_rows(xp_ref, t0, rows, cw_ref, cb_ref, cg_ref, cbeta_ref))
            yield
        h1 = (h_ref[0, r0:r0 + ts, :] + _dot(jnp.concatenate(pieces, axis=0), wout_ref[:cc, :])
              + _dot(o_ref[0, r0:r0 + ts, :], wout_ref[cc:, :]))
        yield
        out_ref[0, r0:r0 + ts, :] = yield from _xattn_rows(h1, gx_ref, wq_ref, kv_ref, wo_ref)

    _run_staggered([sub(r0) for r0 in range(0, tq, ts)], ts // rows)


def _l0_tail(u, o, h, cw, cb, cg, cbeta, wout, gx, wq, kv, wo, tq=512, ts=256, rows=64):
    bsz, t, d = h.shape
    cc = u.shape[-1]
    nb = tq // CONV_HALO
    tile = lambda w: pl.BlockSpec((1, tq, w), lambda b, i: (b, i, 0))
    return pl.pallas_call(
        functools.partial(_l0_tail_kernel, ts=ts, rows=rows),
        grid=(bsz, t // tq),
        in_specs=[tile(cc),
                  pl.BlockSpec((1, CONV_HALO, cc), lambda b, i: (b, jnp.maximum(i * nb - 1, 0), 0)),
                  pl.BlockSpec((1, CONV_HALO, cc),
                               lambda b, i: (b, jnp.minimum((i + 1) * nb, t // CONV_HALO - 1), 0)),
                  tile(o.shape[-1]), tile(d), _resident(cw.shape), _resident((1, cc)), _resident((1, cc)),
                  _resident((1, cc)), _resident(wout.shape), _resident((1, d)), _resident(wq.shape),
                  pl.BlockSpec((1, kv.shape[1], 2 * d), lambda b, i: (b, 0, 0)), _resident(wo.shape)],
        out_specs=tile(d),
        out_shape=jax.ShapeDtypeStruct(h.shape, F32),
        scratch_shapes=[pltpu.VMEM((tq + 2 * CONV_HALO, cc), F32)],
        compiler_params=_cparams("parallel", "parallel"),
        name="l0_tail",
    )(u, u, u, o, h, cw, cb, cg, cbeta, wout, gx, wq, kv, wo)


def _xattn_kernel(h_ref, g_ref, wq_ref, kv_ref, wo_ref, o_ref, *, ts):
    tq = h_ref.shape[1]

    def sub(r0):
        o_ref[0, r0:r0 + ts, :] = yield from _xattn_rows(h_ref[0, r0:r0 + ts, :], g_ref, wq_ref, kv_ref, wo_ref)

    _run_staggered([sub(r0) for r0 in range(0, tq, ts)], 1)


def _cross_attention(h, kv, g, wq, wo, tq=512, ts=256):
    bsz, t, d = h.shape
    m = kv.shape[1]
    return pl.pallas_call(
        functools.partial(_xattn_kernel, ts=ts),
        grid=(bsz, t // tq),
        in_specs=[pl.BlockSpec((1, tq, d), lambda b, i: (b, i, 0)), _resident((1, d)), _resident(wq.shape),
                  pl.BlockSpec((1, m, 2 * d), lambda b, i: (b, 0, 0)), _resident(wo.shape)],
        out_specs=pl.BlockSpec((1, tq, d), lambda b, i: (b, i, 0)),
        out_shape=jax.ShapeDtypeStruct(h.shape, F32),
        compiler_params=_cparams("parallel", "parallel"),
        name="xattn",
    )(h, g, wq, kv, wo)


def _mlp_kernel(h_ref, g_ref, wu_ref, wd_ref, gf_ref, o_ref, *, hc, final_norm):
    h = h_ref[...]
    xn = _rms(h, g_ref[...]).astype(BF16)
    acc = h
    for c in range(wu_ref.shape[1] // hc):
        a = jnp.maximum(_dot(xn, wu_ref[:, c * hc:(c + 1) * hc]), 0.0)
        acc = acc + _dot(a * a, wd_ref[c * hc:(c + 1) * hc, :])
    if final_norm:
        acc = _rms(acc, gf_ref[...])
    o_ref[...] = acc


def _mlp(h, g, wu, wd, gf, final_norm, tm=512, hc=1024):
    n, d = h.shape
    row = lambda i: (i, 0)
    return pl.pallas_call(
        functools.partial(_mlp_kernel, hc=hc, final_norm=final_norm),
        grid=(n // tm,),
        in_specs=[pl.BlockSpec((tm, d), row), _resident((1, d)), _resident(wu.shape), _resident(wd.shape),
                  _resident((1, d))],
        out_specs=pl.BlockSpec((tm, d), row),
        out_shape=jax.ShapeDtypeStruct((n, d), F32),
        compiler_params=_cparams("parallel"),
        name="mlp",
    )(h, g, wu, wd, gf)


def _head_sum(x, sel_ref, selt_ref):
    return _dot(_dot(x, sel_ref[...]), selt_ref[...])


def _rwkv_prep_kernel(h_ref, hp_ref, hn_ref, g_ref, mu_ref, wr_ref, wk_ref, wv_ref, w1_ref, w2_ref, w0_ref,
                      a1_ref, a2_ref, a0_ref, g1_ref, g2_ref, kk_ref, ka_ref, rk_ref, sel_ref, selt_ref,
                      r_out, v_out, kn_out, gate_out, bonus_out, kd_out, cum_out, b_out, *, ts):
    i = pl.program_id(1)
    last = pl.num_programs(1) - 1
    tm = h_ref.shape[1]
    g = g_ref[...]
    ti = lax.broadcasted_iota(jnp.int32, (ts, ts), 0)
    si = lax.broadcasted_iota(jnp.int32, (ts, ts), 1)
    same_chunk = (ti >> CHUNK_SHIFT) == (si >> CHUNK_SHIFT)
    before = (jnp.where(same_chunk, jnp.where(si <= ti, 1.0, 0.0), 0.0).astype(BF16),
              jnp.where(same_chunk, jnp.where(si >= ti, 1.0, 0.0), 0.0).astype(BF16))

    def rows(r0):
        x = _rms(h_ref[0, r0:r0 + ts, :], g)
        if r0 == 0:
            prev_row = _rms(hp_ref[0], g)[SUBLANES - 1:, :] * jnp.where(i > 0, 1.0, 0.0)
        else:
            prev_row = _rms(h_ref[0, r0 - SUBLANES:r0, :], g)[SUBLANES - 1:, :]
        if r0 + ts == tm:
            next_row = _rms(hn_ref[0], g)[0:1, :] * jnp.where(i < last, 1.0, 0.0)
        else:
            next_row = _rms(h_ref[0, r0 + ts:r0 + ts + SUBLANES, :], g)[0:1, :]
        rowid = lax.broadcasted_iota(jnp.int32, x.shape, 0)
        x_prev = jnp.where(rowid == 0, prev_row, pltpu.roll(x, 1, 0))
        x_next = jnp.where(rowid == ts - 1, next_row, pltpu.roll(x, ts - 1, 0))
        hh = 0.5 * (x_prev + x_next) - x
        mix = lambda j: x + hh * mu_ref[j:j + 1, :]
        r = _dot(mix(0), wr_ref[...])
        k = _dot(mix(2), wk_ref[...])
        v = _dot(mix(3), wv_ref[...])
        yield
        gate_in = _dot(mix(5), g1_ref[...])
        lw = _dot(mix(1), w1_ref[...])
        la = _dot(mix(4), a1_ref[...])
        yield
        gate = _dot(_sigmoid(gate_in), g2_ref[...])
        lw = jnp.tanh(lw)
        w_pre = [w0_ref[z:z + 1, :] + _dot(lw, w2_ref[z]) for z in range(2)]
        a_pre = [a0_ref[z:z + 1, :] + _dot(la, a2_ref[z]) for z in range(2)]
        kk = k * kk_ref[...]
        ss = _dot(kk * kk, sel_ref[...])
        yield
        kn = kk * lax.rsqrt(jnp.maximum(_dot(ss, selt_ref[...]), 1e-24))
        kka = k * ka_ref[...]
        kd_sum = jnp.zeros_like(k)
        cum = []
        for z in range(2):
            cum.append(_split_dot(before[z], _sigmoid(w_pre[z]) * (-math.exp(-0.5))))
            rate = _sigmoid(a_pre[z])
            kd = k + kka * (rate - 1.0)
            kd_out[z, 0, r0:r0 + ts, :] = kd.astype(kd_out.dtype)
            b_out[z, 0, r0:r0 + ts, :] = (kn * rate).astype(b_out.dtype)
            kd_sum = kd_sum + kd
        bs = _dot(r * kd_sum * rk_ref[...], sel_ref[...])
        yield
        r_out[0, r0:r0 + ts, :] = r.astype(r_out.dtype)
        v_out[0, r0:r0 + ts, :] = v.astype(v_out.dtype)
        kn_out[0, r0:r0 + ts, :] = kn.astype(kn_out.dtype)
        gate_out[0, r0:r0 + ts, :] = gate.astype(gate_out.dtype)
        bonus_out[0, r0:r0 + ts, :] = (_dot(bs, selt_ref[...]) * v).astype(bonus_out.dtype)
        for z in range(2):
            cum_out[z, 0, r0:r0 + ts, :] = cum[z]

    _run_staggered([rows(r0) for r0 in range(0, tm, ts)], 1)


def _rwkv_prep(h, g, mu, wr, wk, wv, w1, w2, w0, a1, a2, a0, g1, g2, kk, ka, rk, sel, selt, tm=256, ts=128):
    bsz, t, d = h.shape
    nb = tm // SUBLANES
    tile = pl.BlockSpec((1, tm, d), lambda b, i: (b, i, 0))
    tile2 = pl.BlockSpec((2, 1, tm, d), lambda b, i: (0, b, i, 0))
    one = jax.ShapeDtypeStruct((bsz, t, d), BF16)
    two = jax.ShapeDtypeStruct((2, bsz, t, d), BF16)
    consts = [g, mu, wr, wk, wv, w1, w2, w0, a1, a2, a0, g1, g2, kk, ka, rk, sel, selt]
    return pl.pallas_call(
        functools.partial(_rwkv_prep_kernel, ts=ts),
        grid=(bsz, t // tm),
        in_specs=[tile,
                  pl.BlockSpec((1, SUBLANES, d), lambda b, i: (b, jnp.maximum(i * nb - 1, 0), 0)),
                  pl.BlockSpec((1, SUBLANES, d),
                               lambda b, i: (b, jnp.minimum((i + 1) * nb, t // SUBLANES - 1), 0))]
                 + [_resident(c.shape) for c in consts],
        out_specs=[tile, tile, tile, tile, tile, tile2, tile2, tile2],
        out_shape=[one, one, one, one, one, two, jax.ShapeDtypeStruct((2, bsz, t, d), F32), two],
        compiler_params=_cparams("parallel", "parallel"),
        name="l1_rwkv_prep",
    )(h, h, h, *consts)


def _blockdiag(x):
    lane = lax.broadcasted_iota(jnp.int32, x.shape, 1)
    head0 = (lane & (PAIR - 1)) < RWKV_HEAD_DIM
    zero = jnp.zeros_like(x)
    return jnp.concatenate([jnp.where(head0, x, zero), jnp.where(head0, zero, x)], axis=0)


def _chunk_local(r, k, v, kn, cum, b, rev):
    c = CHUNK
    bd = _blockdiag
    row = lax.broadcasted_iota(jnp.int32, (c, PAIR), 0)
    lane = lax.broadcasted_iota(jnp.int32, (c, PAIR), 1)
    s_idx = lane & (c - 1)
    if rev:
        cum_prev = jnp.where(row == c - 1, 0.0, pltpu.roll(cum, c - 1, 0))
        tot = cum[0:1, :]
        strict = s_idx > row
        incl = s_idx >= row
    else:
        cum_prev = jnp.where(row == 0, 0.0, pltpu.roll(cum, 1, 0))
        tot = cum[c - 1:c, :]
        strict = s_idx < row
        incl = s_idx <= row
    w_incl = jnp.exp(cum)
    w_excl = jnp.exp(cum_prev)
    w_inv = jnp.exp(-cum)
    w_tot = jnp.exp(tot)
    w_rest = jnp.exp(tot - cum)
    a_t = -kn * w_excl
    r_t = r * w_incl
    b_t = b * w_inv
    k_t = k * w_inv
    b_h = b * w_rest
    k_h = k * w_rest
    same_blk = (s_idx >> 4) == (row >> 4)

    sc = _dot_nt(jnp.concatenate([a_t, r_t], axis=0), jnp.concatenate([bd(b_t), bd(k_t)], axis=0))
    yield
    p_ab = jnp.where(strict, sc[:c, :PAIR], 0.0)
    p_ak = jnp.where(strict, sc[:c, PAIR:], 0.0)
    p_rb = jnp.where(incl, sc[c:, :PAIR], 0.0)
    p_rk = jnp.where(incl, sc[c:, PAIR:], 0.0)
    dm = jnp.where(same_blk, p_ab, 0.0)
    em = p_ab - dm
    x2 = _dot(dm, bd(dm))
    av = _dot(p_ak, bd(v))
    yield
    td = jnp.where(s_idx == row, 1.0, 0.0) + dm
    both = _dot(jnp.concatenate([x2, td], axis=0), bd(x2))
    yield
    x4 = both[:c]
    td = td + both[c:]
    both = _dot(jnp.concatenate([x4, td], axis=0), bd(x4))
    yield
    td = td + both[c:]
    td = td + _dot(td, bd(both[:c]))
    yield
    ty = _dot(td, bd(jnp.concatenate([a_t, av, em], axis=1)))
    yield
    au, f1 = ty[:, :2 * PAIR], ty[:, 2 * PAIR:]
    both = _dot(f1, bd(jnp.concatenate([f1, au], axis=1)))
    f2 = both[:, :PAIR]
    au = au + both[:, PAIR:]
    yield
    au = au + _dot(f2, bd(au))
    yield
    rhs = jnp.concatenate([bd(au), jnp.concatenate([jnp.zeros((PAIR, PAIR), F32), bd(v)], axis=1)], axis=0)
    ry = _dot(jnp.concatenate([p_rb, p_rk], axis=1), rhs)
    rhs2 = jnp.concatenate([au, jnp.concatenate([jnp.zeros((c, PAIR), F32), v], axis=1)], axis=0)
    mg = _dot(jnp.concatenate([b_h, k_h], axis=0).T, rhs2)
    yield
    rbar = r_t + ry[:, :PAIR]
    yloc = ry[:, PAIR:]
    r2 = lax.broadcasted_iota(jnp.int32, (PAIR, PAIR), 0)
    l2 = lax.broadcasted_iota(jnp.int32, (PAIR, PAIR), 1)
    same_head = (r2 >> 6) == (l2 >> 6)
    m = jnp.where(same_head, mg[:, :PAIR], 0.0) + jnp.where(r2 == l2, w_tot, 0.0)
    gg = jnp.where(same_head, mg[:, PAIR:], 0.0)
    return rbar, yloc, m, gg


def _scan_kernel(r_ref, v_ref, kn_ref, kd_ref, cum_ref, b_ref, yf_ref, yb_ref, ds_ref, loc_a, loc_b, *, unroll):
    nc = r_ref.shape[1] // CHUNK
    groups = nc // unroll
    y_refs = (yf_ref, yb_ref)
    c = CHUNK
    ds_ref[...] = jnp.zeros(ds_ref.shape, F32)
    loc_b[...] = jnp.zeros(loc_b.shape, F32)

    def places(grp):
        out = []
        for u in range(unroll):
            for z in range(2):
                cidx = grp * unroll + u
                if z == 1:
                    cidx = nc - 1 - cidx
                out.append((z, pl.ds(pl.multiple_of(cidx * c, c), c)))
        return out

    def local_terms(grp, loc_ref):
        f32 = lambda ref, *idx: ref[idx].astype(F32)
        gens = [_chunk_local(f32(r_ref, 0, sl), f32(kd_ref, z, 0, sl), f32(v_ref, 0, sl), f32(kn_ref, 0, sl),
                             cum_ref[z, 0, sl, :], f32(b_ref, z, 0, sl), rev=(z == 1))
                for z, sl in places(grp)]

        def park(j, gen):
            rbar, yloc, m, gg = yield from gen
            loc_ref[j, 0:c, :] = rbar
            loc_ref[j, c:2 * c, :] = yloc
            loc_ref[j, 2 * c:2 * c + PAIR, :] = m
            loc_ref[j, 2 * c + PAIR:, :] = gg

        return [park(j, gen) for j, gen in enumerate(gens)]

    def recurrence(grp, loc_ref):
        ds = [ds_ref[0], ds_ref[1]]
        for j, (z, sl) in enumerate(places(grp)):
            both = _dot(jnp.concatenate([loc_ref[j, 0:c, :], loc_ref[j, 2 * c:2 * c + PAIR, :]], axis=0), ds[z])
            y_refs[z][0, sl, :] = both[:c] + loc_ref[j, c:2 * c, :]
            ds[z] = both[c:] + loc_ref[j, 2 * c + PAIR:, :]
            if z == 1:
                yield
        ds_ref[0] = ds[0]
        ds_ref[1] = ds[1]

    def body(it, carry):
        first = 2 * it
        _run_staggered(local_terms(first, loc_a) + [recurrence(jnp.maximum(first - 1, 0), loc_b)], 0)
        _run_staggered(local_terms(first + 1, loc_b) + [recurrence(first, loc_a)], 0)
        return carry

    lax.fori_loop(0, groups // 2, body, 0)
    _run_staggered([recurrence(groups - 1, loc_b)], 0)


def _wkv7_scan(r, v, kn, kd, cum, b, unroll=8):
    bsz, t, d = r.shape
    nc = t // CHUNK
    unroll = min(unroll, nc // 2)
    assert nc % (2 * unroll) == 0, "the scan kernel takes chunk groups in pairs"
    one = pl.BlockSpec((1, t, PAIR), lambda bb, p: (bb, 0, p))
    two = pl.BlockSpec((2, 1, t, PAIR), lambda bb, p: (0, bb, 0, p))
    out = jax.ShapeDtypeStruct((bsz, t, d), F32)
    return pl.pallas_call(
        functools.partial(_scan_kernel, unroll=unroll),
        grid=(bsz, d // PAIR),
        in_specs=[one, one, one, two, two, two],
        out_specs=[one, one],
        out_shape=[out, out],
        scratch_shapes=[pltpu.VMEM((2, PAIR, PAIR), F32)]
                       + [pltpu.VMEM((2 * unroll, 2 * CHUNK + 2 * PAIR, PAIR), F32)] * 2,
        compiler_params=_cparams("parallel", "parallel"),
        name="l1_wkv7_scan",
    )(r, v, kn, kd, cum, b)


def _rwkv_post_kernel(yf_ref, yb_ref, bonus_ref, gate_ref, h_ref, lg_ref, lb_ref, wo_ref, sel_ref, selt_ref,
                      o_ref):
    y = yf_ref[...] + yb_ref[...]
    inv_n = 1.0 / RWKV_HEAD_DIM
    mu = _head_sum(y, sel_ref, selt_ref) * inv_n
    yc = y - mu
    var = _head_sum(yc * yc, sel_ref, selt_ref) * inv_n
    yn = yc * lax.rsqrt(var + GN_EPS) * lg_ref[...] + lb_ref[...]
    o_ref[...] = h_ref[...] + _dot((yn + bonus_ref[...]) * gate_ref[...], wo_ref[...])


def _rwkv_post(yf, yb, bonus, gate, h, lg, lb, wo, sel, selt, tm=512):
    n, d = h.shape
    row = pl.BlockSpec((tm, d), lambda i: (i, 0))
    return pl.pallas_call(
        _rwkv_post_kernel,
        grid=(n // tm,),
        in_specs=[row, row, row, row, row, _resident((1, d)), _resident((1, d)), _resident(wo.shape),
                  _resident(sel.shape), _resident(selt.shape)],
        out_specs=row,
        out_shape=jax.ShapeDtypeStruct((n, d), F32),
        compiler_params=_cparams("parallel"),
        name="l1_rwkv_post",
    )(yf, yb, bonus, gate, h, lg, lb, wo, sel, selt)


def _pad_lora_out(w2):
    zero = jnp.zeros_like(w2[0])
    return jnp.stack([jnp.concatenate([w2[0], zero], axis=0), jnp.concatenate([zero, w2[1]], axis=0)])


def kernel(x, mem, rel_bias_table, norm_mix, norm_xattn, norm_mem, norm_ffn, norm_final, ab_w_in, ab_w_out, conv_w, conv_b, conv_ln_g, conv_ln_b, diff_lq1, diff_lk1, diff_lq2, diff_lk2, diff_subln_g, rwkv_mu, rwkv_w_r, rwkv_w_k, rwkv_w_v, rwkv_w_o, rwkv_w0, rwkv_w1, rwkv_w2, rwkv_a0, rwkv_a1, rwkv_a2, rwkv_g1, rwkv_g2, rwkv_k_k, rwkv_k_a, rwkv_r_k, rwkv_ln_g, rwkv_ln_b, xattn_w_q, xattn_w_kv, xattn_w_o, ffn_w_up, ffn_w_down):
    bsz, t, d = x.shape
    n = bsz * t
    depth = norm_mix.shape[0]
    n_mem = mem.shape[1]
    cc = conv_w.shape[-1]
    qk = DIFF_HEADS * 2 * DIFF_HEAD_DIM
    vw = ab_w_in.shape[-1] - 2 * cc - 2 * qk
    bf = lambda w: w.astype(BF16)
    row = lambda w: w.reshape(1, -1)

    heads = d // RWKV_HEAD_DIM
    head_of = jnp.arange(d, dtype=jnp.int32) // RWKV_HEAD_DIM
    sel = (head_of[:, None] == jnp.arange(128, dtype=jnp.int32)[None, :]).astype(BF16)
    selt = sel.T
    assert heads <= 128

    h = x.reshape(n, d)
    mem2 = mem.reshape(bsz * n_mem, d)
    for i in range(depth):
        j = i // 2
        kv = _norm_linear(mem2, row(norm_mem[i]), bf(xattn_w_kv[i]), BF16).reshape(bsz, n_mem, 2 * d)
        if i % 2 == 0:
            lam_init = 0.8 - 0.6 * math.exp(-0.3 * i)
            u, q, k, v = _inproj(h, row(norm_mix[i]), bf(ab_w_in[j]), cc, qk, vw)
            lq = jnp.stack([diff_lq1[j], diff_lk1[j], diff_lq2[j], diff_lk2[j]])
            o = _diff_attention(q.reshape(bsz, t, qk), k.reshape(bsz, t, qk), v.reshape(bsz, t, vw),
                                rel_bias_table.reshape(-1), lq, row(diff_subln_g[j]), lam_init)
            h = _l0_tail(u.reshape(bsz, t, cc), o, h.reshape(bsz, t, d), conv_w[j], row(conv_b[j]),
                         row(conv_ln_g[j]), row(conv_ln_b[j]), bf(ab_w_out[j]), row(norm_xattn[i]),
                         bf(xattn_w_q[i]), kv, bf(xattn_w_o[i])).reshape(n, d)
        else:
            h3 = h.reshape(bsz, t, d)
            w1 = bf(jnp.concatenate([rwkv_w1[j, 0], rwkv_w1[j, 1]], axis=1))
            a1 = bf(jnp.concatenate([rwkv_a1[j, 0], rwkv_a1[j, 1]], axis=1))
            r, v, kn, gate, bonus, kd, cum, b = _rwkv_prep(
                h3, row(norm_mix[i]), rwkv_mu[j], bf(rwkv_w_r[j]), bf(rwkv_w_k[j]), bf(rwkv_w_v[j]),
                w1, bf(_pad_lora_out(rwkv_w2[j])), rwkv_w0[j], a1, bf(_pad_lora_out(rwkv_a2[j])), rwkv_a0[j],
                bf(rwkv_g1[j]), bf(rwkv_g2[j]), row(rwkv_k_k[j]), row(rwkv_k_a[j]), row(rwkv_r_k[j]),
                sel, selt)
            yf, yb = _wkv7_scan(r, v, kn, kd, cum, b)
            h = _rwkv_post(yf.reshape(n, d), yb.reshape(n, d), bonus.reshape(n, d), gate.reshape(n, d), h,
                           row(rwkv_ln_g[j]), row(rwkv_ln_b[j]), bf(rwkv_w_o[j]), sel, selt)
            h = _cross_attention(h.reshape(bsz, t, d), kv, row(norm_xattn[i]), bf(xattn_w_q[i]),
                                 bf(xattn_w_o[i])).reshape(n, d)
        h = _mlp(h, row(norm_ffn[i]), bf(ffn_w_up[i]), bf(ffn_w_down[i]), row(norm_final),
                 final_norm=(i == depth - 1))
    return h.reshape(bsz, t, d)
```

```python
import functools
import math

import jax
import jax.numpy as jnp
from jax import lax
from jax.experimental import pallas as pl
from jax.experimental.pallas import tpu as pltpu

F32 = jnp.float32
BF16 = jnp.bfloat16

V7X_VMEM_BYTES = 64 * 1024 * 1024
VMEM_LIMIT_BYTES = V7X_VMEM_BYTES - 8 * 1024 * 1024

LOG2E = math.log2(math.e)
NORM_EPS = 1e-6
CONV_LN_EPS = 1e-5
GN_EPS = 64e-5
CONV_WIDTH = 31
CONV_PAD = CONV_WIDTH // 2
CONV_HALO = 16
DIFF_HEADS = 4
DIFF_HEAD_DIM = 64
REL_BUCKETS = 32
REL_MAX_DIST = 128
XATTN_HEADS = 4
RWKV_HEAD_DIM = 64
SUBLANES = 8
LANES = 128
CHUNK = 64
CHUNK_SHIFT = CHUNK.bit_length() - 1
PAIR = 2 * RWKV_HEAD_DIM


def _cparams(*sem):
    return pltpu.CompilerParams(dimension_semantics=sem, vmem_limit_bytes=VMEM_LIMIT_BYTES)


def _resident(shape):
    nd = len(shape)
    return pl.BlockSpec(shape, lambda *_: (0,) * nd, pipeline_mode=pl.Buffered(1))


def _rms(x, g):
    ms = jnp.mean(x * x, axis=-1, keepdims=True)
    return x * lax.rsqrt(ms + NORM_EPS) * g


def _sigmoid(x):
    return 1.0 / (1.0 + jnp.exp(-x))


def _dot(a, b):
    return jnp.dot(a.astype(BF16), b.astype(BF16), preferred_element_type=F32)


def _dot_nt(a, b):
    return lax.dot_general(a.astype(BF16), b.astype(BF16), (((1,), (1,)), ((), ())),
                           preferred_element_type=F32)


def _split_dot(mat, x):
    hi = x.astype(BF16)
    lo = (x - hi.astype(F32)).astype(BF16)
    return (jnp.dot(mat, hi, preferred_element_type=F32) + jnp.dot(mat, lo, preferred_element_type=F32))


def _run_staggered(gens, offset):
    results = [None] * len(gens)
    running = [True] * len(gens)
    rnd = 0
    while any(running):
        for i, gen in enumerate(gens):
            if rnd >= i * offset and running[i]:
                try:
                    next(gen)
                except StopIteration as stop:
                    results[i] = stop.value
                    running[i] = False
        rnd += 1
    return results


def _inproj_kernel(h_ref, g_ref, w_ref, u_ref, q_ref, k_ref, v_ref, *, cc, qk, scale):
    xn = _rms(h_ref[...], g_ref[...])
    p = _dot(xn, w_ref[...])
    u_ref[...] = p[:, :cc] * _sigmoid(p[:, cc:2 * cc])
    o = 2 * cc
    q_ref[...] = (p[:, o:o + qk] * scale).astype(BF16)
    k_ref[...] = p[:, o + qk:o + 2 * qk].astype(BF16)
    v_ref[...] = p[:, o + 2 * qk:].astype(BF16)


def _inproj(h, g, w, cc, qk, vw, tm=512):
    n, d = h.shape
    row = lambda i: (i, 0)
    return pl.pallas_call(
        functools.partial(_inproj_kernel, cc=cc, qk=qk, scale=DIFF_HEAD_DIM ** -0.5 * LOG2E),
        grid=(n // tm,),
        in_specs=[pl.BlockSpec((tm, d), row), _resident((1, d)), _resident(w.shape)],
        out_specs=[pl.BlockSpec((tm, cc), row), pl.BlockSpec((tm, qk), row),
                   pl.BlockSpec((tm, qk), row), pl.BlockSpec((tm, vw), row)],
        out_shape=[jax.ShapeDtypeStruct((n, cc), F32), jax.ShapeDtypeStruct((n, qk), BF16),
                   jax.ShapeDtypeStruct((n, qk), BF16), jax.ShapeDtypeStruct((n, vw), BF16)],
        compiler_params=_cparams("parallel"),
        name="l0_inproj",
    )(h, g, w)


def _conv_rows(xp_ref, base, rows, w_ref, b_ref, g_ref, beta_ref):
    c = xp_ref.shape[-1]
    off = CONV_HALO - CONV_PAD
    span = rows + 2 * CONV_HALO
    pieces = []
    for c0 in range(0, c, LANES):
        win = xp_ref[base:base + span, c0:c0 + LANES]
        acc = jnp.zeros((rows, LANES), F32)
        for rem in range(SUBLANES):
            shifted = win if rem == 0 else pltpu.roll(win, span - rem, 0)
            for start in range(0, 2 * CONV_HALO, SUBLANES):
                k = start + rem - off
                if 0 <= k < CONV_WIDTH:
                    acc = acc + shifted[start:start + rows, :] * w_ref[k:k + 1, c0:c0 + LANES]
        pieces.append(acc)
    y = jnp.concatenate(pieces, axis=1) + b_ref[...]
    mu = jnp.mean(y, axis=-1, keepdims=True)
    yc = y - mu
    var = jnp.mean(yc * yc, axis=-1, keepdims=True)
    yn = yc * lax.rsqrt(var + CONV_LN_EPS) * g_ref[...] + beta_ref[...]
    return (yn * _sigmoid(yn)).astype(BF16)


def _t5_bucket(rel):
    nb = REL_BUCKETS // 2
    max_exact = nb // 2
    n = jnp.abs(rel)
    large = jnp.full(rel.shape, max_exact, jnp.int32)
    steps = nb - max_exact
    for m in range(1, steps):
        thr = math.ceil(max_exact * (REL_MAX_DIST / max_exact) ** (m / steps) - 1e-9)
        large = large + jnp.where(n >= thr, 1, 0)
    mag = jnp.where(n < max_exact, n, large)
    return mag + jnp.where(rel > 0, nb, 0)


def _diffattn_kernel(tbl_ref, lq_ref, sg_ref, q_ref, k_ref, v_ref, o_ref, bias_ref, *, tq, ts, kb, lam_init):
    h = pl.program_id(0)
    qi = pl.program_id(1)
    b = pl.program_id(2)
    t = k_ref.shape[1]

    @pl.when(b == 0)
    def _():
        u = lax.broadcasted_iota(jnp.int32, (1, t + tq), 1)
        bucket = _t5_bucket(u - (tq - 1) - qi * tq)
        line = jnp.zeros((1, t + tq), F32)
        for i in range(REL_BUCKETS):
            line = jnp.where(bucket == i, tbl_ref[i * DIFF_HEADS + h], line)
        rows = pltpu.roll(jnp.broadcast_to(line * LOG2E, (tq, t + tq)), 1, 1, stride=1, stride_axis=0)
        bias_ref[...] = rows[:, tq:]

    lq = lq_ref[...]
    lam = (jnp.exp(jnp.sum(lq[0:1] * lq[1:2], axis=-1, keepdims=True))
           - jnp.exp(jnp.sum(lq[2:3] * lq[3:4], axis=-1, keepdims=True)) + lam_init)
    hw = q_ref.shape[-1]
    first = lax.broadcasted_iota(jnp.int32, (ts, hw), 1) < DIFF_HEAD_DIM
    nkb = t // kb

    def softmax_v(r0, comp):
        q = q_ref[0, r0:r0 + ts, :]
        qc = jnp.where(first, q, jnp.zeros_like(q)) if comp == 0 else jnp.where(first, jnp.zeros_like(q), q)
        s = []
        mx = None
        for j in range(nkb):
            sj = _dot_nt(qc, k_ref[0, j * kb:(j + 1) * kb, :]) + bias_ref[r0:r0 + ts, j * kb:(j + 1) * kb]
            mj = jnp.max(sj, axis=-1, keepdims=True)
            mx = mj if mx is None else jnp.maximum(mx, mj)
            s.append(sj)
            yield
        pv = None
        for j in range(nkb):
            vj = v_ref[0, j * kb:(j + 1) * kb, :]
            dj = _dot(jnp.exp2(s[j] - mx), jnp.concatenate([vj, jnp.ones_like(vj)], axis=1))
            pv = dj if pv is None else pv + dj
            yield
        return pv[:, :hw] / pv[:, hw:]

    starts = range(0, tq, ts)
    maps = _run_staggered([softmax_v(r0, comp) for r0 in starts for comp in range(2)], nkb)
    for i, r0 in enumerate(starts):
        o = maps[2 * i] - lam * maps[2 * i + 1]
        o = o * lax.rsqrt(jnp.mean(o * o, axis=-1, keepdims=True) + NORM_EPS) * sg_ref[...] * (1.0 - lam_init)
        o_ref[0, r0:r0 + ts, :] = o.astype(BF16)


def _diff_attention(q, k, v, tbl, lq, sg, lam_init, tq=512, ts=256, kb=512):
    bsz, t, _ = q.shape
    hw = 2 * DIFF_HEAD_DIM
    return pl.pallas_call(
        functools.partial(_diffattn_kernel, tq=tq, ts=ts, kb=kb, lam_init=lam_init),
        grid=(DIFF_HEADS, t // tq, bsz),
        in_specs=[pl.BlockSpec(memory_space=pltpu.SMEM), _resident(lq.shape), _resident(sg.shape),
                  pl.BlockSpec((1, tq, hw), lambda h, i, b: (b, i, h)),
                  pl.BlockSpec((1, t, hw), lambda h, i, b: (b, 0, h)),
                  pl.BlockSpec((1, t, hw), lambda h, i, b: (b, 0, h))],
        out_specs=pl.BlockSpec((1, tq, hw), lambda h, i, b: (b, i, h)),
        out_shape=jax.ShapeDtypeStruct(q.shape, BF16),
        scratch_shapes=[pltpu.VMEM((tq, t), F32)],
        compiler_params=_cparams("parallel", "parallel", "arbitrary"),
        name="l0_diffattn",
    )(tbl, lq, sg, q, k, v)


def _norm_linear_kernel(x_ref, g_ref, w_ref, o_ref):
    o_ref[...] = _dot(_rms(x_ref[...], g_ref[...]), w_ref[...]).astype(o_ref.dtype)


def _norm_linear(x, g, w, out_dtype, tm=512):
    n, d = x.shape
    m = w.shape[1]
    return pl.pallas_call(
        _norm_linear_kernel,
        grid=(n // tm,),
        in_specs=[pl.BlockSpec((tm, d), lambda i: (i, 0)), _resident((1, d)), _resident(w.shape)],
        out_specs=pl.BlockSpec((tm, m), lambda i: (i, 0)),
        out_shape=jax.ShapeDtypeStruct((n, m), out_dtype),
        compiler_params=_cparams("parallel"),
        name="norm_linear",
    )(x, g, w)


def _xattn_rows(h, g_ref, wq_ref, kv_ref, wo_ref):
    d = h.shape[-1]
    hd = d // XATTN_HEADS
    q = (_dot(_rms(h, g_ref[...]), wq_ref[...]) * (hd ** -0.5 * LOG2E)).astype(BF16)
    yield
    outs = []
    for i in range(XATTN_HEADS):
        kh = kv_ref[0, :, i * hd:(i + 1) * hd]
        vh = kv_ref[0, :, d + i * hd:d + (i + 1) * hd]
        s = _dot_nt(q[:, i * hd:(i + 1) * hd], kh)
        e = jnp.exp2(s - jnp.max(s, axis=-1, keepdims=True))
        p = e / jnp.sum(e, axis=-1, keepdims=True)
        outs.append(_dot(p, vh).astype(BF16))
    yield
    return h + _dot(jnp.concatenate(outs, axis=1), wo_ref[...])


def _l0_tail_kernel(u_ref, up_ref, un_ref, o_ref, h_ref, cw_ref, cb_ref, cg_ref, cbeta_ref, wout_ref,
                    gx_ref, wq_ref, kv_ref, wo_ref, out_ref, xp_ref, *, ts, rows):
    i = pl.program_id(1)
    last = pl.num_programs(1) - 1
    tq = h_ref.shape[1]
    cc = u_ref.shape[-1]
    xp_ref[0:CONV_HALO, :] = up_ref[0] * jnp.where(i > 0, 1.0, 0.0)
    xp_ref[CONV_HALO:CONV_HALO + tq, :] = u_ref[0]
    xp_ref[CONV_HALO + tq:, :] = un_ref[0] * jnp.where(i < last, 1.0, 0.0)

    def sub(r0):
        pieces = []
        for t0 in range(r0, r0 + ts, rows):
            pieces.append(_conv_rows(xp_ref, t0, rows, cw_ref, cb_ref, cg_ref, cbeta_ref))
            yield
        h1 = (h_ref[0, r0:r0 + ts, :] + _dot(jnp.concatenate(pieces, axis=0), wout_ref[:cc, :])
              + _dot(o_ref[0, r0:r0 + ts, :], wout_ref[cc:, :]))
        yield
        out_ref[0, r0:r0 + ts, :] = yield from _xattn_rows(h1, gx_ref, wq_ref, kv_ref, wo_ref)

    _run_staggered([sub(r0) for r0 in range(0, tq, ts)], ts // rows)


def _l0_tail(u, o, h, cw, cb, cg, cbeta, wout, gx, wq, kv, wo, tq=512, ts=256, rows=64):
    bsz, t, d = h.shape
    cc = u.shape[-1]
    nb = tq // CONV_HALO
    tile = lambda w: pl.BlockSpec((1, tq, w), lambda b, i: (b, i, 0))
    return pl.pallas_call(
        functools.partial(_l0_tail_kernel, ts=ts, rows=rows),
        grid=(bsz, t // tq),
        in_specs=[tile(cc),
                  pl.BlockSpec((1, CONV_HALO, cc), lambda b, i: (b, jnp.maximum(i * nb - 1, 0), 0)),
                  pl.BlockSpec((1, CONV_HALO, cc),
                               lambda b, i: (b, jnp.minimum((i + 1) * nb, t // CONV_HALO - 1), 0)),
                  tile(o.shape[-1]), tile(d), _resident(cw.shape), _resident((1, cc)), _resident((1, cc)),
                  _resident((1, cc)), _resident(wout.shape), _resident((1, d)), _resident(wq.shape),
                  pl.BlockSpec((1, kv.shape[1], 2 * d), lambda b, i: (b, 0, 0)), _resident(wo.shape)],
        out_specs=tile(d),
        out_shape=jax.ShapeDtypeStruct(h.shape, F32),
        scratch_shapes=[pltpu.VMEM((tq + 2 * CONV_HALO, cc), F32)],
        compiler_params=_cparams("parallel", "parallel"),
        name="l0_tail",
    )(u, u, u, o, h, cw, cb, cg, cbeta, wout, gx, wq, kv, wo)


def _mlp_kernel(h_ref, g_ref, wu_ref, wd_ref, gf_ref, o_ref, *, hc, final_norm):
    h = h_ref[...]
    xn = _rms(h, g_ref[...]).astype(BF16)
    acc = h
    for c in range(wu_ref.shape[1] // hc):
        a = jnp.maximum(_dot(xn, wu_ref[:, c * hc:(c + 1) * hc]), 0.0)
        acc = acc + _dot(a * a, wd_ref[c * hc:(c + 1) * hc, :])
    if final_norm:
        acc = _rms(acc, gf_ref[...])
    o_ref[...] = acc


def _mlp(h, g, wu, wd, gf, final_norm, tm=512, hc=1024):
    n, d = h.shape
    row = lambda i: (i, 0)
    return pl.pallas_call(
        functools.partial(_mlp_kernel, hc=hc, final_norm=final_norm),
        grid=(n // tm,),
        in_specs=[pl.BlockSpec((tm, d), row), _resident((1, d)), _resident(wu.shape), _resident(wd.shape),
                  _resident((1, d))],
        out_specs=pl.BlockSpec((tm, d), row),
        out_shape=jax.ShapeDtypeStruct((n, d), F32),
        compiler_params=_cparams("parallel"),
        name="mlp",
    )(h, g, wu, wd, gf)


def _rwkv_prep_kernel(h_ref, hp_ref, hn_ref, g_ref, mu_ref, wr_ref, wk_ref, wv_ref, w1_ref, w2_ref, w0_ref,
                      a1_ref, a2_ref, a0_ref, g1_ref, g2_ref, kk_ref, ka_ref, rk_ref, sel_ref, selt_ref,
                      r_out, v_out, kn_out, gate_out, bonus_out, kd_out, cum_out, b_out, *, ts):
    i = pl.program_id(1)
    last = pl.num_programs(1) - 1
    tm = h_ref.shape[1]
    g = g_ref[...]
    ti = lax.broadcasted_iota(jnp.int32, (ts, ts), 0)
    si = lax.broadcasted_iota(jnp.int32, (ts, ts), 1)
    same_chunk = (ti >> CHUNK_SHIFT) == (si >> CHUNK_SHIFT)
    before = (jnp.where(same_chunk, jnp.where(si <= ti, 1.0, 0.0), 0.0).astype(BF16),
              jnp.where(same_chunk, jnp.where(si >= ti, 1.0, 0.0), 0.0).astype(BF16))

    def rows(r0):
        x = _rms(h_ref[0, r0:r0 + ts, :], g)
        if r0 == 0:
            prev_row = _rms(hp_ref[0], g)[SUBLANES - 1:, :] * jnp.where(i > 0, 1.0, 0.0)
        else:
            prev_row = _rms(h_ref[0, r0 - SUBLANES:r0, :], g)[SUBLANES - 1:, :]
        if r0 + ts == tm:
            next_row = _rms(hn_ref[0], g)[0:1, :] * jnp.where(i < last, 1.0, 0.0)
        else:
            next_row = _rms(h_ref[0, r0 + ts:r0 + ts + SUBLANES, :], g)[0:1, :]
        rowid = lax.broadcasted_iota(jnp.int32, x.shape, 0)
        x_prev = jnp.where(rowid == 0, prev_row, pltpu.roll(x, 1, 0))
        x_next = jnp.where(rowid == ts - 1, next_row, pltpu.roll(x, ts - 1, 0))
        hh = 0.5 * (x_prev + x_next) - x
        mix = lambda j: x + hh * mu_ref[j:j + 1, :]
        r = _dot(mix(0), wr_ref[...])
        k = _dot(mix(2), wk_ref[...])
        v = _dot(mix(3), wv_ref[...])
        yield
        gate_in = _dot(mix(5), g1_ref[...])
        lw = _dot(mix(1), w1_ref[...])
        la = _dot(mix(4), a1_ref[...])
        yield
        gate = _dot(_sigmoid(gate_in), g2_ref[...])
        lw = jnp.tanh(lw)
        w_pre = [w0_ref[z:z + 1, :] + _dot(lw, w2_ref[z]) for z in range(2)]
        a_pre = [a0_ref[z:z + 1, :] + _dot(la, a2_ref[z]) for z in range(2)]
        kk = k * kk_ref[...]
        ss = _dot(kk * kk, sel_ref[...])
        yield
        kn = kk * lax.rsqrt(jnp.maximum(_dot(ss, selt_ref[...]), 1e-24))
        kka = k * ka_ref[...]
        kd_sum = jnp.zeros_like(k)
        cum = []
        for z in range(2):
            cum.append(_split_dot(before[z], _sigmoid(w_pre[z]) * (-math.exp(-0.5))))
            rate = _sigmoid(a_pre[z])
            kd = k + kka * (rate - 1.0)
            kd_out[z, 0, r0:r0 + ts, :] = kd.astype(kd_out.dtype)
            b_out[z, 0, r0:r0 + ts, :] = (kn * rate).astype(b_out.dtype)
            kd_sum = kd_sum + kd
        bs = _dot(r * kd_sum * rk_ref[...], sel_ref[...])
        yield
        r_out[0, r0:r0 + ts, :] = r.astype(r_out.dtype)
        v_out[0, r0:r0 + ts, :] = v.astype(v_out.dtype)
        kn_out[0, r0:r0 + ts, :] = kn.astype(kn_out.dtype)
        gate_out[0, r0:r0 + ts, :] = gate.astype(gate_out.dtype)
        bonus_out[0, r0:r0 + ts, :] = (_dot(bs, selt_ref[...]) * v).astype(bonus_out.dtype)
        for z in range(2):
            cum_out[z, 0, r0:r0 + ts, :] = cum[z]

    _run_staggered([rows(r0) for r0 in range(0, tm, ts)], 1)


def _rwkv_prep(h, g, mu, wr, wk, wv, w1, w2, w0, a1, a2, a0, g1, g2, kk, ka, rk, sel, selt, tm=256, ts=128):
    bsz, t, d = h.shape
    nb = tm // SUBLANES
    tile = pl.BlockSpec((1, tm, d), lambda b, i: (b, i, 0))
    tile2 = pl.BlockSpec((2, 1, tm, d), lambda b, i: (0, b, i, 0))
    one = jax.ShapeDtypeStruct((bsz, t, d), BF16)
    two = jax.ShapeDtypeStruct((2, bsz, t, d), BF16)
    consts = [g, mu, wr, wk, wv, w1, w2, w0, a1, a2, a0, g1, g2, kk, ka, rk, sel, selt]
    return pl.pallas_call(
        functools.partial(_rwkv_prep_kernel, ts=ts),
        grid=(bsz, t // tm),
        in_specs=[tile,
                  pl.BlockSpec((1, SUBLANES, d), lambda b, i: (b, jnp.maximum(i * nb - 1, 0), 0)),
                  pl.BlockSpec((1, SUBLANES, d),
                               lambda b, i: (b, jnp.minimum((i + 1) * nb, t // SUBLANES - 1), 0))]
                 + [_resident(c.shape) for c in consts],
        out_specs=[tile, tile, tile, tile, tile, tile2, tile2, tile2],
        out_shape=[one, one, one, one, one, two, jax.ShapeDtypeStruct((2, bsz, t, d), F32), two],
        compiler_params=_cparams("parallel", "parallel"),
        name="l1_rwkv_prep",
    )(h, h, h, *consts)


def _blockdiag(x):
    lane = lax.broadcasted_iota(jnp.int32, x.shape, 1)
    head0 = (lane & (PAIR - 1)) < RWKV_HEAD_DIM
    zero = jnp.zeros_like(x)
    return jnp.concatenate([jnp.where(head0, x, zero), jnp.where(head0, zero, x)], axis=0)


def _chunk_local(r, k, v, kn, cum, b, rev):
    c = CHUNK
    bd = _blockdiag
    row = lax.broadcasted_iota(jnp.int32, (c, PAIR), 0)
    lane = lax.broadcasted_iota(jnp.int32, (c, PAIR), 1)
    s_idx = lane & (c - 1)
    if rev:
        cum_prev = jnp.where(row == c - 1, 0.0, pltpu.roll(cum, c - 1, 0))
        tot = cum[0:1, :]
        strict = s_idx > row
        incl = s_idx >= row
    else:
        cum_prev = jnp.where(row == 0, 0.0, pltpu.roll(cum, 1, 0))
        tot = cum[c - 1:c, :]
        strict = s_idx < row
        incl = s_idx <= row
    w_incl = jnp.exp(cum)
    w_excl = jnp.exp(cum_prev)
    w_inv = jnp.exp(-cum)
    w_tot = jnp.exp(tot)
    w_rest = jnp.exp(tot - cum)
    a_t = -kn * w_excl
    r_t = r * w_incl
    b_t = b * w_inv
    k_t = k * w_inv
    b_h = b * w_rest
    k_h = k * w_rest
    same_blk = (s_idx >> 4) == (row >> 4)

    sc = _dot_nt(jnp.concatenate([a_t, r_t], axis=0), jnp.concatenate([bd(b_t), bd(k_t)], axis=0))
    yield
    p_ab = jnp.where(strict, sc[:c, :PAIR], 0.0)
    p_ak = jnp.where(strict, sc[:c, PAIR:], 0.0)
    p_rb = jnp.where(incl, sc[c:, :PAIR], 0.0)
    p_rk = jnp.where(incl, sc[c:, PAIR:], 0.0)
    dm = jnp.where(same_blk, p_ab, 0.0)
    em = p_ab - dm
    x2 = _dot(dm, bd(dm))
    av = _dot(p_ak, bd(v))
    yield
    td = jnp.where(s_idx == row, 1.0, 0.0) + dm
    both = _dot(jnp.concatenate([x2, td], axis=0), bd(x2))
    yield
    x4 = both[:c]
    td = td + both[c:]
    both = _dot(jnp.concatenate([x4, td], axis=0), bd(x4))
    yield
    td = td + both[c:]
    td = td + _dot(td, bd(both[:c]))
    yield
    ty = _dot(td, bd(jnp.concatenate([a_t, av, em], axis=1)))
    yield
    au, f1 = ty[:, :2 * PAIR], ty[:, 2 * PAIR:]
    both = _dot(f1, bd(jnp.concatenate([f1, au], axis=1)))
    f2 = both[:, :PAIR]
    au = au + both[:, PAIR:]
    yield
    au = au + _dot(f2, bd(au))
    yield
    rhs = jnp.concatenate([bd(au), jnp.concatenate([jnp.zeros((PAIR, PAIR), F32), bd(v)], axis=1)], axis=0)
    ry = _dot(jnp.concatenate([p_rb, p_rk], axis=1), rhs)
    rhs2 = jnp.concatenate([au, jnp.concatenate([jnp.zeros((c, PAIR), F32), v], axis=1)], axis=0)
    mg = _dot(jnp.concatenate([b_h, k_h], axis=0).T, rhs2)
    yield
    rbar = r_t + ry[:, :PAIR]
    yloc = ry[:, PAIR:]
    r2 = lax.broadcasted_iota(jnp.int32, (PAIR, PAIR), 0)
    l2 = lax.broadcasted_iota(jnp.int32, (PAIR, PAIR), 1)
    same_head = (r2 >> 6) == (l2 >> 6)
    m = jnp.where(same_head, mg[:, :PAIR], 0.0) + jnp.where(r2 == l2, w_tot, 0.0)
    gg = jnp.where(same_head, mg[:, PAIR:], 0.0)
    return rbar, yloc, m, gg


def _scan_kernel(r_ref, v_ref, kn_ref, kd_ref, cum_ref, b_ref, yf_ref, yb_ref, ds_ref, loc_a, loc_b, *, unroll):
    nc = r_ref.shape[1] // CHUNK
    groups = nc // unroll
    y_refs = (yf_ref, yb_ref)
    c = CHUNK
    ds_ref[...] = jnp.zeros(ds_ref.shape, F32)
    loc_b[...] = jnp.zeros(loc_b.shape, F32)

    def places(grp):
        out = []
        for u in range(unroll):
            for z in range(2):
                cidx = grp * unroll + u
                if z == 1:
                    cidx = nc - 1 - cidx
                out.append((z, pl.ds(pl.multiple_of(cidx * c, c), c)))
        return out

    def local_terms(grp, loc_ref):
        f32 = lambda ref, *idx: ref[idx].astype(F32)
        gens = [_chunk_local(f32(r_ref, 0, sl), f32(kd_ref, z, 0, sl), f32(v_ref, 0, sl), f32(kn_ref, 0, sl),
                             cum_ref[z, 0, sl, :], f32(b_ref, z, 0, sl), rev=(z == 1))
                for z, sl in places(grp)]

        def park(j, gen):
            rbar, yloc, m, gg = yield from gen
            loc_ref[j, 0:c, :] = rbar
            loc_ref[j, c:2 * c, :] = yloc
            loc_ref[j, 2 * c:2 * c + PAIR, :] = m
            loc_ref[j, 2 * c + PAIR:, :] = gg

        return [park(j, gen) for j, gen in enumerate(gens)]

    def recurrence(grp, loc_ref):
        ds = [ds_ref[0], ds_ref[1]]
        for j, (z, sl) in enumerate(places(grp)):
            both = _dot(jnp.concatenate([loc_ref[j, 0:c, :], loc_ref[j, 2 * c:2 * c + PAIR, :]], axis=0), ds[z])
            y_refs[z][0, sl, :] = (both[:c] + loc_ref[j, c:2 * c, :]).astype(y_refs[z].dtype)
            ds[z] = both[c:] + loc_ref[j, 2 * c + PAIR:, :]
            if z == 1:
                yield
        ds_ref[0] = ds[0]
        ds_ref[1] = ds[1]

    def body(it, carry):
        first = 2 * it
        _run_staggered(local_terms(first, loc_a) + [recurrence(jnp.maximum(first - 1, 0), loc_b)], 0)
        _run_staggered(local_terms(first + 1, loc_b) + [recurrence(first, loc_a)], 0)
        return carry

    lax.fori_loop(0, groups // 2, body, 0)
    _run_staggered([recurrence(groups - 1, loc_b)], 0)


def _wkv7_scan(r, v, kn, kd, cum, b, unroll=8):
    bsz, t, d = r.shape
    nc = t // CHUNK
    unroll = min(unroll, nc // 2)
    assert nc % (2 * unroll) == 0, "the scan kernel takes chunk groups in pairs"
    one = pl.BlockSpec((1, t, PAIR), lambda bb, p: (bb, 0, p))
    two = pl.BlockSpec((2, 1, t, PAIR), lambda bb, p: (0, bb, 0, p))
    out = jax.ShapeDtypeStruct((bsz, t, d), BF16)
    return pl.pallas_call(
        functools.partial(_scan_kernel, unroll=unroll),
        grid=(bsz, d // PAIR),
        in_specs=[one, one, one, two, two, two],
        out_specs=[one, one],
        out_shape=[out, out],
        scratch_shapes=[pltpu.VMEM((2, PAIR, PAIR), F32)]
                       + [pltpu.VMEM((2 * unroll, 2 * CHUNK + 2 * PAIR, PAIR), F32)] * 2,
        compiler_params=_cparams("parallel", "parallel"),
        name="l1_wkv7_scan",
    )(r, v, kn, kd, cum, b)


def _l1_tail_kernel(yf_ref, yb_ref, bonus_ref, gate_ref, h_ref, lg_ref, lb_ref, wo_ref, sel_ref, selt_ref,
                    gx_ref, wq_ref, kv_ref, wxo_ref, out_ref, *, ts):
    tq = h_ref.shape[1]
    inv_n = 1.0 / RWKV_HEAD_DIM

    def sub(r0):
        rs = slice(r0, r0 + ts)
        y = yf_ref[0, rs, :] + yb_ref[0, rs, :]
        mu = _dot(y, sel_ref[...])
        yield
        yc = y - _dot(mu, selt_ref[...]) * inv_n
        var = _dot(yc * yc, sel_ref[...])
        yield
        yn = yc * lax.rsqrt(_dot(var, selt_ref[...]) * inv_n + GN_EPS) * lg_ref[...] + lb_ref[...]
        h1 = h_ref[0, rs, :] + _dot((yn + bonus_ref[0, rs, :]) * gate_ref[0, rs, :], wo_ref[...])
        yield
        out_ref[0, rs, :] = yield from _xattn_rows(h1, gx_ref, wq_ref, kv_ref, wxo_ref)

    _run_staggered([sub(r0) for r0 in range(0, tq, ts)], 2)


def _l1_tail(yf, yb, bonus, gate, h, lg, lb, wo, sel, selt, gx, wq, kv, wxo, tq=512, ts=256):
    bsz, t, d = h.shape
    tile = pl.BlockSpec((1, tq, d), lambda b, i: (b, i, 0))
    return pl.pallas_call(
        functools.partial(_l1_tail_kernel, ts=ts),
        grid=(bsz, t // tq),
        in_specs=[tile, tile, tile, tile, tile, _resident((1, d)), _resident((1, d)), _resident(wo.shape),
                  _resident(sel.shape), _resident(selt.shape), _resident((1, d)), _resident(wq.shape),
                  pl.BlockSpec((1, kv.shape[1], 2 * d), lambda b, i: (b, 0, 0)), _resident(wxo.shape)],
        out_specs=tile,
        out_shape=jax.ShapeDtypeStruct(h.shape, F32),
        compiler_params=_cparams("parallel", "parallel"),
        name="l1_tail",
    )(yf, yb, bonus, gate, h, lg, lb, wo, sel, selt, gx, wq, kv, wxo)


def _pad_lora_out(w2):
    zero = jnp.zeros_like(w2[0])
    return jnp.stack([jnp.concatenate([w2[0], zero], axis=0), jnp.concatenate([zero, w2[1]], axis=0)])


def kernel(x, mem, rel_bias_table, norm_mix, norm_xattn, norm_mem, norm_ffn, norm_final, ab_w_in, ab_w_out, conv_w, conv_b, conv_ln_g, conv_ln_b, diff_lq1, diff_lk1, diff_lq2, diff_lk2, diff_subln_g, rwkv_mu, rwkv_w_r, rwkv_w_k, rwkv_w_v, rwkv_w_o, rwkv_w0, rwkv_w1, rwkv_w2, rwkv_a0, rwkv_a1, rwkv_a2, rwkv_g1, rwkv_g2, rwkv_k_k, rwkv_k_a, rwkv_r_k, rwkv_ln_g, rwkv_ln_b, xattn_w_q, xattn_w_kv, xattn_w_o, ffn_w_up, ffn_w_down):
    bsz, t, d = x.shape
    n = bsz * t
    depth = norm_mix.shape[0]
    n_mem = mem.shape[1]
    cc = conv_w.shape[-1]
    qk = DIFF_HEADS * 2 * DIFF_HEAD_DIM
    vw = ab_w_in.shape[-1] - 2 * cc - 2 * qk
    bf = lambda w: w.astype(BF16)
    row = lambda w: w.reshape(1, -1)

    heads = d // RWKV_HEAD_DIM
    head_of = jnp.arange(d, dtype=jnp.int32) // RWKV_HEAD_DIM
    sel = (head_of[:, None] == jnp.arange(128, dtype=jnp.int32)[None, :]).astype(BF16)
    selt = sel.T
    assert heads <= 128

    h = x.reshape(n, d)
    mem2 = mem.reshape(bsz * n_mem, d)
    for i in range(depth):
        j = i // 2
        kv = _norm_linear(mem2, row(norm_mem[i]), bf(xattn_w_kv[i]), BF16).reshape(bsz, n_mem, 2 * d)
        if i % 2 == 0:
            lam_init = 0.8 - 0.6 * math.exp(-0.3 * i)
            u, q, k, v = _inproj(h, row(norm_mix[i]), bf(ab_w_in[j]), cc, qk, vw)
            lq = jnp.stack([diff_lq1[j], diff_lk1[j], diff_lq2[j], diff_lk2[j]])
            o = _diff_attention(q.reshape(bsz, t, qk), k.reshape(bsz, t, qk), v.reshape(bsz, t, vw),
                                rel_bias_table.reshape(-1), lq, row(diff_subln_g[j]), lam_init)
            h = _l0_tail(u.reshape(bsz, t, cc), o, h.reshape(bsz, t, d), conv_w[j], row(conv_b[j]),
                         row(conv_ln_g[j]), row(conv_ln_b[j]), bf(ab_w_out[j]), row(norm_xattn[i]),
                         bf(xattn_w_q[i]), kv, bf(xattn_w_o[i])).reshape(n, d)
        else:
            h3 = h.reshape(bsz, t, d)
            w1 = bf(jnp.concatenate([rwkv_w1[j, 0], rwkv_w1[j, 1]], axis=1))
            a1 = bf(jnp.concatenate([rwkv_a1[j, 0], rwkv_a1[j, 1]], axis=1))
            r, v, kn, gate, bonus, kd, cum, b = _rwkv_prep(
                h3, row(norm_mix[i]), rwkv_mu[j], bf(rwkv_w_r[j]), bf(rwkv_w_k[j]), bf(rwkv_w_v[j]),
                w1, bf(_pad_lora_out(rwkv_w2[j])), rwkv_w0[j], a1, bf(_pad_lora_out(rwkv_a2[j])), rwkv_a0[j],
                bf(rwkv_g1[j]), bf(rwkv_g2[j]), row(rwkv_k_k[j]), row(rwkv_k_a[j]), row(rwkv_r_k[j]),
                sel, selt)
            yf, yb = _wkv7_scan(r, v, kn, kd, cum, b)
            h = _l1_tail(yf, yb, bonus, gate, h3, row(rwkv_ln_g[j]), row(rwkv_ln_b[j]), bf(rwkv_w_o[j]), sel, selt,
                         row(norm_xattn[i]), bf(xattn_w_q[i]), kv, bf(xattn_w_o[i])).reshape(n, d)
        h = _mlp(h, row(norm_ffn[i]), bf(ffn_w_up[i]), bf(ffn_w_down[i]), row(norm_final),
                 final_norm=(i == depth - 1))
    return h.reshape(bsz, t, d)
```

```python
import functools
import math

import jax
import jax.numpy as jnp
from jax import lax
from jax.experimental import pallas as pl
from jax.experimental.pallas import tpu as pltpu

F32 = jnp.float32
BF16 = jnp.bfloat16

V7X_VMEM_BYTES = 64 * 1024 * 1024
VMEM_LIMIT_BYTES = V7X_VMEM_BYTES - 8 * 1024 * 1024

LOG2E = math.log2(math.e)
NORM_EPS = 1e-6
CONV_LN_EPS = 1e-5
GN_EPS = 64e-5
CONV_WIDTH = 31
CONV_PAD = CONV_WIDTH // 2
CONV_HALO = 16
DIFF_HEADS = 4
DIFF_HEAD_DIM = 64
REL_BUCKETS = 32
REL_MAX_DIST = 128
XATTN_HEADS = 4
RWKV_HEAD_DIM = 64
SUBLANES = 8
LANES = 128
CHUNK = 64
CHUNK_SHIFT = CHUNK.bit_length() - 1
PAIR = 2 * RWKV_HEAD_DIM


def _cparams(*sem):
    return pltpu.CompilerParams(dimension_semantics=sem, vmem_limit_bytes=VMEM_LIMIT_BYTES)


def _resident(shape):
    nd = len(shape)
    return pl.BlockSpec(shape, lambda *_: (0,) * nd, pipeline_mode=pl.Buffered(1))


def _rms(x, g):
    ms = jnp.mean(x * x, axis=-1, keepdims=True)
    return x * lax.rsqrt(ms + NORM_EPS) * g


def _sigmoid(x):
    return 1.0 / (1.0 + jnp.exp(-x))


def _dot(a, b):
    return jnp.dot(a.astype(BF16), b.astype(BF16), preferred_element_type=F32)


def _dot_nt(a, b):
    return lax.dot_general(a.astype(BF16), b.astype(BF16), (((1,), (1,)), ((), ())),
                           preferred_element_type=F32)


def _split_dot(mat, x):
    hi = x.astype(BF16)
    lo = (x - hi.astype(F32)).astype(BF16)
    return (jnp.dot(mat, hi, preferred_element_type=F32) + jnp.dot(mat, lo, preferred_element_type=F32))


def _run_staggered(gens, offset):
    results = [None] * len(gens)
    running = [True] * len(gens)
    rnd = 0
    while any(running):
        for i, gen in enumerate(gens):
            if rnd >= i * offset and running[i]:
                try:
                    next(gen)
                except StopIteration as stop:
                    results[i] = stop.value
                    running[i] = False
        rnd += 1
    return results


def _inproj_kernel(h_ref, g_ref, w_ref, u_ref, q_ref, k_ref, v_ref, *, cc, qk, scale):
    xn = _rms(h_ref[...], g_ref[...])
    p = _dot(xn, w_ref[...])
    u_ref[...] = p[:, :cc] * _sigmoid(p[:, cc:2 * cc])
    o = 2 * cc
    q_ref[...] = (p[:, o:o + qk] * scale).astype(BF16)
    k_ref[...] = p[:, o + qk:o + 2 * qk].astype(BF16)
    v_ref[...] = p[:, o + 2 * qk:].astype(BF16)


def _inproj(h, g, w, cc, qk, vw, tm=512):
    n, d = h.shape
    row = lambda i: (i, 0)
    return pl.pallas_call(
        functools.partial(_inproj_kernel, cc=cc, qk=qk, scale=DIFF_HEAD_DIM ** -0.5 * LOG2E),
        grid=(n // tm,),
        in_specs=[pl.BlockSpec((tm, d), row), _resident((1, d)), _resident(w.shape)],
        out_specs=[pl.BlockSpec((tm, cc), row), pl.BlockSpec((tm, qk), row),
                   pl.BlockSpec((tm, qk), row), pl.BlockSpec((tm, vw), row)],
        out_shape=[jax.ShapeDtypeStruct((n, cc), F32), jax.ShapeDtypeStruct((n, qk), BF16),
                   jax.ShapeDtypeStruct((n, qk), BF16), jax.ShapeDtypeStruct((n, vw), BF16)],
        compiler_params=_cparams("parallel"),
        name="l0_inproj",
    )(h, g, w)


def _conv_rows(xp_ref, base, rows, w_ref, b_ref, g_ref, beta_ref):
    c = xp_ref.shape[-1]
    off = CONV_HALO - CONV_PAD
    span = rows + 2 * CONV_HALO
    pieces = []
    for c0 in range(0, c, LANES):
        win = xp_ref[base:base + span, c0:c0 + LANES]
        acc = jnp.zeros((rows, LANES), F32)
        for rem in range(SUBLANES):
            shifted = win if rem == 0 else pltpu.roll(win, span - rem, 0)
            for start in range(0, 2 * CONV_HALO, SUBLANES):
                k = start + rem - off
                if 0 <= k < CONV_WIDTH:
                    acc = acc + shifted[start:start + rows, :] * w_ref[k:k + 1, c0:c0 + LANES]
        pieces.append(acc)
    y = jnp.concatenate(pieces, axis=1) + b_ref[...]
    mu = jnp.mean(y, axis=-1, keepdims=True)
    yc = y - mu
    var = jnp.mean(yc * yc, axis=-1, keepdims=True)
    yn = yc * lax.rsqrt(var + CONV_LN_EPS) * g_ref[...] + beta_ref[...]
    return (yn * _sigmoid(yn)).astype(BF16)


def _t5_bucket(rel):
    nb = REL_BUCKETS // 2
    max_exact = nb // 2
    n = jnp.abs(rel)
    large = jnp.full(rel.shape, max_exact, jnp.int32)
    steps = nb - max_exact
    for m in range(1, steps):
        thr = math.ceil(max_exact * (REL_MAX_DIST / max_exact) ** (m / steps) - 1e-9)
        large = large + jnp.where(n >= thr, 1, 0)
    mag = jnp.where(n < max_exact, n, large)
    return mag + jnp.where(rel > 0, nb, 0)


def _diffattn_kernel(tbl_ref, lq_ref, sg_ref, q_ref, k_ref, v_ref, o_ref, bias_ref, *, tq, ts, kb, lam_init):
    h = pl.program_id(0)
    qi = pl.program_id(1)
    b = pl.program_id(2)
    t = k_ref.shape[1]

    @pl.when(b == 0)
    def _():
        u = lax.broadcasted_iota(jnp.int32, (1, t + tq), 1)
        bucket = _t5_bucket(u - (tq - 1) - qi * tq)
        line = jnp.zeros((1, t + tq), F32)
        for i in range(REL_BUCKETS):
            line = jnp.where(bucket == i, tbl_ref[i * DIFF_HEADS + h], line)
        rows = pltpu.roll(jnp.broadcast_to(line * LOG2E, (tq, t + tq)), 1, 1, stride=1, stride_axis=0)
        bias_ref[...] = rows[:, tq:]

    lq = lq_ref[...]
    lam = (jnp.exp(jnp.sum(lq[0:1] * lq[1:2], axis=-1, keepdims=True))
           - jnp.exp(jnp.sum(lq[2:3] * lq[3:4], axis=-1, keepdims=True)) + lam_init)
    hw = q_ref.shape[-1]
    first = lax.broadcasted_iota(jnp.int32, (ts, hw), 1) < DIFF_HEAD_DIM
    nkb = t // kb

    def softmax_v(r0, comp):
        q = q_ref[0, r0:r0 + ts, :]
        qc = jnp.where(first, q, jnp.zeros_like(q)) if comp == 0 else jnp.where(first, jnp.zeros_like(q), q)
        s = []
        mx = None
        for j in range(nkb):
            sj = _dot_nt(qc, k_ref[0, j * kb:(j + 1) * kb, :]) + bias_ref[r0:r0 + ts, j * kb:(j + 1) * kb]
            mj = jnp.max(sj, axis=-1, keepdims=True)
            mx = mj if mx is None else jnp.maximum(mx, mj)
            s.append(sj)
            yield
        pv = None
        for j in range(nkb):
            vj = v_ref[0, j * kb:(j + 1) * kb, :]
            dj = _dot(jnp.exp2(s[j] - mx), jnp.concatenate([vj, jnp.ones_like(vj)], axis=1))
            pv = dj if pv is None else pv + dj
            yield
        return pv[:, :hw] / pv[:, hw:]

    starts = range(0, tq, ts)
    maps = _run_staggered([softmax_v(r0, comp) for r0 in starts for comp in range(2)], nkb)
    for i, r0 in enumerate(starts):
        o = maps[2 * i] - lam * maps[2 * i + 1]
        o = o * lax.rsqrt(jnp.mean(o * o, axis=-1, keepdims=True) + NORM_EPS) * sg_ref[...] * (1.0 - lam_init)
        o_ref[0, r0:r0 + ts, :] = o.astype(BF16)


def _diff_attention(q, k, v, tbl, lq, sg, lam_init, tq=1024, ts=256, kb=256):
    bsz, t, _ = q.shape
    hw = 2 * DIFF_HEAD_DIM
    return pl.pallas_call(
        functools.partial(_diffattn_kernel, tq=tq, ts=ts, kb=kb, lam_init=lam_init),
        grid=(DIFF_HEADS, t // tq, bsz),
        in_specs=[pl.BlockSpec(memory_space=pltpu.SMEM), _resident(lq.shape), _resident(sg.shape),
                  pl.BlockSpec((1, tq, hw), lambda h, i, b: (b, i, h)),
                  pl.BlockSpec((1, t, hw), lambda h, i, b: (b, 0, h)),
                  pl.BlockSpec((1, t, hw), lambda h, i, b: (b, 0, h))],
        out_specs=pl.BlockSpec((1, tq, hw), lambda h, i, b: (b, i, h)),
        out_shape=jax.ShapeDtypeStruct(q.shape, BF16),
        scratch_shapes=[pltpu.VMEM((tq, t), F32)],
        compiler_params=_cparams("parallel", "parallel", "arbitrary"),
        name="l0_diffattn",
    )(tbl, lq, sg, q, k, v)


def _norm_linear_kernel(x_ref, g_ref, w_ref, o_ref):
    o_ref[...] = _dot(_rms(x_ref[...], g_ref[...]), w_ref[...]).astype(o_ref.dtype)


def _norm_linear(x, g, w, out_dtype, tm=512):
    n, d = x.shape
    m = w.shape[1]
    return pl.pallas_call(
        _norm_linear_kernel,
        grid=(n // tm,),
        in_specs=[pl.BlockSpec((tm, d), lambda i: (i, 0)), _resident((1, d)), _resident(w.shape)],
        out_specs=pl.BlockSpec((tm, m), lambda i: (i, 0)),
        out_shape=jax.ShapeDtypeStruct((n, m), out_dtype),
        compiler_params=_cparams("parallel"),
        name="norm_linear",
    )(x, g, w)


def _xattn_rows(h, g_ref, wq_ref, kv_ref, wo_ref):
    d = h.shape[-1]
    hd = d // XATTN_HEADS
    q = (_dot(_rms(h, g_ref[...]), wq_ref[...]) * (hd ** -0.5 * LOG2E)).astype(BF16)
    yield
    outs = []
    for i in range(XATTN_HEADS):
        kh = kv_ref[0, :, i * hd:(i + 1) * hd]
        vh = kv_ref[0, :, d + i * hd:d + (i + 1) * hd]
        s = _dot_nt(q[:, i * hd:(i + 1) * hd], kh)
        e = jnp.exp2(s - jnp.max(s, axis=-1, keepdims=True))
        p = e / jnp.sum(e, axis=-1, keepdims=True)
        outs.append(_dot(p, vh).astype(BF16))
    yield
    return h + _dot(jnp.concatenate(outs, axis=1), wo_ref[...])


def _l0_tail_kernel(u_ref, up_ref, un_ref, o_ref, h_ref, cw_ref, cb_ref, cg_ref, cbeta_ref, wout_ref,
                    gx_ref, wq_ref, kv_ref, wo_ref, out_ref, xp_ref, *, ts, rows):
    i = pl.program_id(1)
    last = pl.num_programs(1) - 1
    tq = h_ref.shape[1]
    cc = u_ref.shape[-1]
    xp_ref[0:CONV_HALO, :] = up_ref[0] * jnp.where(i > 0, 1.0, 0.0)
    xp_ref[CONV_HALO:CONV_HALO + tq, :] = u_ref[0]
    xp_ref[CONV_HALO + tq:, :] = un_ref[0] * jnp.where(i < last, 1.0, 0.0)

    def sub(r0):
        pieces = []
        for t0 in range(r0, r0 + ts, rows):
            pieces.append(_conv_rows(xp_ref, t0, rows, cw_ref, cb_ref, cg_ref, cbeta_ref))
            yield
        h1 = (h_ref[0, r0:r0 + ts, :] + _dot(jnp.concatenate(pieces, axis=0), wout_ref[:cc, :])
              + _dot(o_ref[0, r0:r0 + ts, :], wout_ref[cc:, :]))
        yield
        out_ref[0, r0:r0 + ts, :] = yield from _xattn_rows(h1, gx_ref, wq_ref, kv_ref, wo_ref)

    _run_staggered([sub(r0) for r0 in range(0, tq, ts)], ts // rows)


def _l0_tail(u, o, h, cw, cb, cg, cbeta, wout, gx, wq, kv, wo, tq=512, ts=256, rows=64):
    bsz, t, d = h.shape
    cc = u.shape[-1]
    nb = tq // CONV_HALO
    tile = lambda w: pl.BlockSpec((1, tq, w), lambda b, i: (b, i, 0))
    return pl.pallas_call(
        functools.partial(_l0_tail_kernel, ts=ts, rows=rows),
        grid=(bsz, t // tq),
        in_specs=[tile(cc),
                  pl.BlockSpec((1, CONV_HALO, cc), lambda b, i: (b, jnp.maximum(i * nb - 1, 0), 0)),
                  pl.BlockSpec((1, CONV_HALO, cc),
                               lambda b, i: (b, jnp.minimum((i + 1) * nb, t // CONV_HALO - 1), 0)),
                  tile(o.shape[-1]), tile(d), _resident(cw.shape), _resident((1, cc)), _resident((1, cc)),
                  _resident((1, cc)), _resident(wout.shape), _resident((1, d)), _resident(wq.shape),
                  pl.BlockSpec((1, kv.shape[1], 2 * d), lambda b, i: (b, 0, 0)), _resident(wo.shape)],
        out_specs=tile(d),
        out_shape=jax.ShapeDtypeStruct(h.shape, F32),
        scratch_shapes=[pltpu.VMEM((tq + 2 * CONV_HALO, cc), F32)],
        compiler_params=_cparams("parallel", "parallel"),
        name="l0_tail",
    )(u, u, u, o, h, cw, cb, cg, cbeta, wout, gx, wq, kv, wo)


def _mlp_kernel(h_ref, g_ref, wu_ref, wd_ref, gf_ref, o_ref, *, hc, final_norm):
    h = h_ref[...]
    xn = _rms(h, g_ref[...]).astype(BF16)
    acc = h
    for c in range(wu_ref.shape[1] // hc):
        a = jnp.maximum(_dot(xn, wu_ref[:, c * hc:(c + 1) * hc]), 0.0)
        acc = acc + _dot(a * a, wd_ref[c * hc:(c + 1) * hc, :])
    if final_norm:
        acc = _rms(acc, gf_ref[...])
    o_ref[...] = acc


def _mlp(h, g, wu, wd, gf, final_norm, tm=512, hc=1024):
    n, d = h.shape
    row = lambda i: (i, 0)
    return pl.pallas_call(
        functools.partial(_mlp_kernel, hc=hc, final_norm=final_norm),
        grid=(n // tm,),
        in_specs=[pl.BlockSpec((tm, d), row), _resident((1, d)), _resident(wu.shape), _resident(wd.shape),
                  _resident((1, d))],
        out_specs=pl.BlockSpec((tm, d), row),
        out_shape=jax.ShapeDtypeStruct((n, d), F32),
        compiler_params=_cparams("parallel"),
        name="mlp",
    )(h, g, wu, wd, gf)


def _rwkv_prep_kernel(h_ref, hp_ref, hn_ref, g_ref, mu_ref, wr_ref, wk_ref, wv_ref, w1_ref, w2_ref, w0_ref,
                      a1_ref, a2_ref, a0_ref, g1_ref, g2_ref, kk_ref, ka_ref, rk_ref, sel_ref, selt_ref,
                      r_out, v_out, kn_out, gate_out, bonus_out, kd_out, cum_out, b_out, *, ts):
    i = pl.program_id(1)
    last = pl.num_programs(1) - 1
    tm = h_ref.shape[1]
    g = g_ref[...]
    ti = lax.broadcasted_iota(jnp.int32, (ts, ts), 0)
    si = lax.broadcasted_iota(jnp.int32, (ts, ts), 1)
    same_chunk = (ti >> CHUNK_SHIFT) == (si >> CHUNK_SHIFT)
    before = (jnp.where(same_chunk, jnp.where(si <= ti, 1.0, 0.0), 0.0).astype(BF16),
              jnp.where(same_chunk, jnp.where(si >= ti, 1.0, 0.0), 0.0).astype(BF16))

    def rows(r0):
        x = _rms(h_ref[0, r0:r0 + ts, :], g)
        if r0 == 0:
            prev_row = _rms(hp_ref[0], g)[SUBLANES - 1:, :] * jnp.where(i > 0, 1.0, 0.0)
        else:
            prev_row = _rms(h_ref[0, r0 - SUBLANES:r0, :], g)[SUBLANES - 1:, :]
        if r0 + ts == tm:
            next_row = _rms(hn_ref[0], g)[0:1, :] * jnp.where(i < last, 1.0, 0.0)
        else:
            next_row = _rms(h_ref[0, r0 + ts:r0 + ts + SUBLANES, :], g)[0:1, :]
        rowid = lax.broadcasted_iota(jnp.int32, x.shape, 0)
        x_prev = jnp.where(rowid == 0, prev_row, pltpu.roll(x, 1, 0))
        x_next = jnp.where(rowid == ts - 1, next_row, pltpu.roll(x, ts - 1, 0))
        hh = 0.5 * (x_prev + x_next) - x
        mix = lambda j: x + hh * mu_ref[j:j + 1, :]
        r = _dot(mix(0), wr_ref[...])
        k = _dot(mix(2), wk_ref[...])
        v = _dot(mix(3), wv_ref[...])
        yield
        gate_in = _dot(mix(5), g1_ref[...])
        lw = _dot(mix(1), w1_ref[...])
        la = _dot(mix(4), a1_ref[...])
        yield
        gate = _dot(_sigmoid(gate_in), g2_ref[...])
        lw = jnp.tanh(lw)
        w_pre = [w0_ref[z:z + 1, :] + _dot(lw, w2_ref[z]) for z in range(2)]
        a_pre = [a0_ref[z:z + 1, :] + _dot(la, a2_ref[z]) for z in range(2)]
        kk = k * kk_ref[...]
        ss = _dot(kk * kk, sel_ref[...])
        yield
        kn = kk * lax.rsqrt(jnp.maximum(_dot(ss, selt_ref[...]), 1e-24))
        kka = k * ka_ref[...]
        kd_sum = jnp.zeros_like(k)
        cum = []
        for z in range(2):
            cum.append(_split_dot(before[z], _sigmoid(w_pre[z]) * (-math.exp(-0.5))))
            rate = _sigmoid(a_pre[z])
            kd = k + kka * (rate - 1.0)
            kd_out[z, 0, r0:r0 + ts, :] = kd.astype(kd_out.dtype)
            b_out[z, 0, r0:r0 + ts, :] = (kn * rate).astype(b_out.dtype)
            kd_sum = kd_sum + kd
        bs = _dot(r * kd_sum * rk_ref[...], sel_ref[...])
        yield
        r_out[0, r0:r0 + ts, :] = r.astype(r_out.dtype)
        v_out[0, r0:r0 + ts, :] = v.astype(v_out.dtype)
        kn_out[0, r0:r0 + ts, :] = kn.astype(kn_out.dtype)
        gate_out[0, r0:r0 + ts, :] = gate.astype(gate_out.dtype)
        bonus_out[0, r0:r0 + ts, :] = (_dot(bs, selt_ref[...]) * v).astype(bonus_out.dtype)
        for z in range(2):
            cum_out[z, 0, r0:r0 + ts, :] = cum[z]

    _run_staggered([rows(r0) for r0 in range(0, tm, ts)], 1)


def _rwkv_prep(h, g, mu, wr, wk, wv, w1, w2, w0, a1, a2, a0, g1, g2, kk, ka, rk, sel, selt, tm=256, ts=128):
    bsz, t, d = h.shape
    nb = tm // SUBLANES
    tile = pl.BlockSpec((1, tm, d), lambda b, i: (b, i, 0))
    tile2 = pl.BlockSpec((2, 1, tm, d), lambda b, i: (0, b, i, 0))
    one = jax.ShapeDtypeStruct((bsz, t, d), BF16)
    two = jax.ShapeDtypeStruct((2, bsz, t, d), BF16)
    consts = [g, mu, wr, wk, wv, w1, w2, w0, a1, a2, a0, g1, g2, kk, ka, rk, sel, selt]
    return pl.pallas_call(
        functools.partial(_rwkv_prep_kernel, ts=ts),
        grid=(bsz, t // tm),
        in_specs=[tile,
                  pl.BlockSpec((1, SUBLANES, d), lambda b, i: (b, jnp.maximum(i * nb - 1, 0), 0)),
                  pl.BlockSpec((1, SUBLANES, d),
                               lambda b, i: (b, jnp.minimum((i + 1) * nb, t // SUBLANES - 1), 0))]
                 + [_resident(c.shape) for c in consts],
        out_specs=[tile, tile, tile, tile, tile, tile2, tile2, tile2],
        out_shape=[one, one, one, one, one, two, jax.ShapeDtypeStruct((2, bsz, t, d), F32), two],
        compiler_params=_cparams("parallel", "parallel"),
        name="l1_rwkv_prep",
    )(h, h, h, *consts)


def _blockdiag(x):
    lane = lax.broadcasted_iota(jnp.int32, x.shape, 1)
    head0 = (lane & (PAIR - 1)) < RWKV_HEAD_DIM
    zero = jnp.zeros_like(x)
    return jnp.concatenate([jnp.where(head0, x, zero), jnp.where(head0, zero, x)], axis=0)


def _chunk_local(r, k, v, kn, cum, b, rev):
    c = CHUNK
    bd = _blockdiag
    row = lax.broadcasted_iota(jnp.int32, (c, PAIR), 0)
    lane = lax.broadcasted_iota(jnp.int32, (c, PAIR), 1)
    s_idx = lane & (c - 1)
    if rev:
        cum_prev = jnp.where(row == c - 1, 0.0, pltpu.roll(cum, c - 1, 0))
        tot = cum[0:1, :]
        strict = s_idx > row
        incl = s_idx >= row
    else:
        cum_prev = jnp.where(row == 0, 0.0, pltpu.roll(cum, 1, 0))
        tot = cum[c - 1:c, :]
        strict = s_idx < row
        incl = s_idx <= row
    w_incl = jnp.exp(cum)
    w_excl = jnp.exp(cum_prev)
    w_inv = jnp.exp(-cum)
    w_tot = jnp.exp(tot)
    w_rest = jnp.exp(tot - cum)
    a_t = -kn * w_excl
    r_t = r * w_incl
    b_t = b * w_inv
    k_t = k * w_inv
    b_h = b * w_rest
    k_h = k * w_rest
    same_blk = (s_idx >> 4) == (row >> 4)

    sc = _dot_nt(jnp.concatenate([a_t, r_t], axis=0), jnp.concatenate([bd(b_t), bd(k_t)], axis=0))
    yield
    p_ab = jnp.where(strict, sc[:c, :PAIR], 0.0)
    p_ak = jnp.where(strict, sc[:c, PAIR:], 0.0)
    p_rb = jnp.where(incl, sc[c:, :PAIR], 0.0)
    p_rk = jnp.where(incl, sc[c:, PAIR:], 0.0)
    dm = jnp.where(same_blk, p_ab, 0.0)
    em = p_ab - dm
    x2 = _dot(dm, bd(dm))
    av = _dot(p_ak, bd(v))
    yield
    td = jnp.where(s_idx == row, 1.0, 0.0) + dm
    both = _dot(jnp.concatenate([x2, td], axis=0), bd(x2))
    yield
    x4 = both[:c]
    td = td + both[c:]
    both = _dot(jnp.concatenate([x4, td], axis=0), bd(x4))
    yield
    td = td + both[c:]
    td = td + _dot(td, bd(both[:c]))
    yield
    ty = _dot(td, bd(jnp.concatenate([a_t, av, em], axis=1)))
    yield
    au, f1 = ty[:, :2 * PAIR], ty[:, 2 * PAIR:]
    both = _dot(f1, bd(jnp.concatenate([f1, au], axis=1)))
    f2 = both[:, :PAIR]
    au = au + both[:, PAIR:]
    yield
    au = au + _dot(f2, bd(au))
    yield
    rhs = jnp.concatenate([bd(au), jnp.concatenate([jnp.zeros((PAIR, PAIR), F32), bd(v)], axis=1)], axis=0)
    ry = _dot(jnp.concatenate([p_rb, p_rk], axis=1), rhs)
    rhs2 = jnp.concatenate([au, jnp.concatenate([jnp.zeros((c, PAIR), F32), v], axis=1)], axis=0)
    mg = _dot(jnp.concatenate([b_h, k_h], axis=0).T, rhs2)
    yield
    rbar = r_t + ry[:, :PAIR]
    yloc = ry[:, PAIR:]
    r2 = lax.broadcasted_iota(jnp.int32, (PAIR, PAIR), 0)
    l2 = lax.broadcasted_iota(jnp.int32, (PAIR, PAIR), 1)
    same_head = (r2 >> 6) == (l2 >> 6)
    m = jnp.where(same_head, mg[:, :PAIR], 0.0) + jnp.where(r2 == l2, w_tot, 0.0)
    gg = jnp.where(same_head, mg[:, PAIR:], 0.0)
    return rbar, yloc, m, gg


def _scan_kernel(r_ref, v_ref, kn_ref, kd_ref, cum_ref, b_ref, yf_ref, yb_ref, ds_ref, loc_a, loc_b, *, unroll):
    nc = r_ref.shape[1] // CHUNK
    groups = nc // unroll
    total = (r_ref.shape[2] // PAIR) * groups
    y_refs = (yf_ref, yb_ref)
    c = CHUNK
    ds_ref[...] = jnp.zeros(ds_ref.shape, F32)
    loc_b[...] = jnp.zeros(loc_b.shape, F32)

    def places(gidx):
        grp = gidx % groups
        lanes = pl.ds(pl.multiple_of((gidx // groups) * PAIR, PAIR), PAIR)
        out = []
        for u in range(unroll):
            for z in range(2):
                cidx = grp * unroll + u
                if z == 1:
                    cidx = nc - 1 - cidx
                out.append((z, pl.ds(pl.multiple_of(cidx * c, c), c), lanes))
        return out

    def local_terms(gidx, loc_ref):
        f32 = lambda ref, *idx: ref[idx].astype(F32)
        gens = [_chunk_local(f32(r_ref, 0, sl, ln), f32(kd_ref, z, 0, sl, ln), f32(v_ref, 0, sl, ln),
                             f32(kn_ref, 0, sl, ln), cum_ref[z, 0, sl, ln], f32(b_ref, z, 0, sl, ln),
                             rev=(z == 1))
                for z, sl, ln in places(gidx)]

        def park(j, gen):
            rbar, yloc, m, gg = yield from gen
            loc_ref[j, 0:c, :] = rbar
            loc_ref[j, c:2 * c, :] = yloc
            loc_ref[j, 2 * c:2 * c + PAIR, :] = m
            loc_ref[j, 2 * c + PAIR:, :] = gg

        return [park(j, gen) for j, gen in enumerate(gens)]

    def recurrence(gidx, loc_ref):
        keep = jnp.where(gidx % groups == 0, 0.0, 1.0)
        ds = [ds_ref[0] * keep, ds_ref[1] * keep]
        for j, (z, sl, ln) in enumerate(places(gidx)):
            both = _dot(jnp.concatenate([loc_ref[j, 0:c, :], loc_ref[j, 2 * c:2 * c + PAIR, :]], axis=0), ds[z])
            y_refs[z][0, sl, ln] = (both[:c] + loc_ref[j, c:2 * c, :]).astype(y_refs[z].dtype)
            ds[z] = both[c:] + loc_ref[j, 2 * c + PAIR:, :]
            if z == 1:
                yield
        ds_ref[0] = ds[0]
        ds_ref[1] = ds[1]

    def body(it, carry):
        first = 2 * it
        _run_staggered(local_terms(first, loc_a) + [recurrence(jnp.maximum(first - 1, 0), loc_b)], 0)
        _run_staggered(local_terms(first + 1, loc_b) + [recurrence(first, loc_a)], 0)
        return carry

    lax.fori_loop(0, total // 2, body, 0)
    _run_staggered([recurrence(total - 1, loc_b)], 0)


def _wkv7_scan(r, v, kn, kd, cum, b, unroll=8, pairs=2):
    bsz, t, d = r.shape
    nc = t // CHUNK
    unroll = min(unroll, nc // 2)
    assert nc % (2 * unroll) == 0, "the scan kernel takes chunk groups in pairs"
    one = pl.BlockSpec((1, t, pairs * PAIR), lambda bb, p: (bb, 0, p))
    two = pl.BlockSpec((2, 1, t, pairs * PAIR), lambda bb, p: (0, bb, 0, p))
    out = jax.ShapeDtypeStruct((bsz, t, d), BF16)
    return pl.pallas_call(
        functools.partial(_scan_kernel, unroll=unroll),
        grid=(bsz, d // (pairs * PAIR)),
        in_specs=[one, one, one, two, two, two],
        out_specs=[one, one],
        out_shape=[out, out],
        scratch_shapes=[pltpu.VMEM((2, PAIR, PAIR), F32)]
                       + [pltpu.VMEM((2 * unroll, 2 * CHUNK + 2 * PAIR, PAIR), F32)] * 2,
        compiler_params=_cparams("parallel", "parallel"),
        name="l1_wkv7_scan",
    )(r, v, kn, kd, cum, b)


def _l1_tail_kernel(yf_ref, yb_ref, bonus_ref, gate_ref, h_ref, lg_ref, lb_ref, wo_ref, sel_ref, selt_ref,
                    gx_ref, wq_ref, kv_ref, wxo_ref, out_ref, *, ts):
    tq = h_ref.shape[1]
    inv_n = 1.0 / RWKV_HEAD_DIM

    def sub(r0):
        rs = slice(r0, r0 + ts)
        y = yf_ref[0, rs, :] + yb_ref[0, rs, :]
        mu = _dot(y, sel_ref[...])
        yield
        yc = y - _dot(mu, selt_ref[...]) * inv_n
        var = _dot(yc * yc, sel_ref[...])
        yield
        yn = yc * lax.rsqrt(_dot(var, selt_ref[...]) * inv_n + GN_EPS) * lg_ref[...] + lb_ref[...]
        h1 = h_ref[0, rs, :] + _dot((yn + bonus_ref[0, rs, :]) * gate_ref[0, rs, :], wo_ref[...])
        yield
        out_ref[0, rs, :] = yield from _xattn_rows(h1, gx_ref, wq_ref, kv_ref, wxo_ref)

    _run_staggered([sub(r0) for r0 in range(0, tq, ts)], 2)


def _l1_tail(yf, yb, bonus, gate, h, lg, lb, wo, sel, selt, gx, wq, kv, wxo, tq=512, ts=256):
    bsz, t, d = h.shape
    tile = pl.BlockSpec((1, tq, d), lambda b, i: (b, i, 0))
    return pl.pallas_call(
        functools.partial(_l1_tail_kernel, ts=ts),
        grid=(bsz, t // tq),
        in_specs=[tile, tile, tile, tile, tile, _resident((1, d)), _resident((1, d)), _resident(wo.shape),
                  _resident(sel.shape), _resident(selt.shape), _resident((1, d)), _resident(wq.shape),
                  pl.BlockSpec((1, kv.shape[1], 2 * d), lambda b, i: (b, 0, 0)), _resident(wxo.shape)],
        out_specs=tile,
        out_shape=jax.ShapeDtypeStruct(h.shape, F32),
        compiler_params=_cparams("parallel", "parallel"),
        name="l1_tail",
    )(yf, yb, bonus, gate, h, lg, lb, wo, sel, selt, gx, wq, kv, wxo)


def _pad_lora_out(w2):
    zero = jnp.zeros_like(w2[0])
    return jnp.stack([jnp.concatenate([w2[0], zero], axis=0), jnp.concatenate([zero, w2[1]], axis=0)])


def kernel(x, mem, rel_bias_table, norm_mix, norm_xattn, norm_mem, norm_ffn, norm_final, ab_w_in, ab_w_out, conv_w, conv_b, conv_ln_g, conv_ln_b, diff_lq1, diff_lk1, diff_lq2, diff_lk2, diff_subln_g, rwkv_mu, rwkv_w_r, rwkv_w_k, rwkv_w_v, rwkv_w_o, rwkv_w0, rwkv_w1, rwkv_w2, rwkv_a0, rwkv_a1, rwkv_a2, rwkv_g1, rwkv_g2, rwkv_k_k, rwkv_k_a, rwkv_r_k, rwkv_ln_g, rwkv_ln_b, xattn_w_q, xattn_w_kv, xattn_w_o, ffn_w_up, ffn_w_down):
    bsz, t, d = x.shape
    n = bsz * t
    depth = norm_mix.shape[0]
    n_mem = mem.shape[1]
    cc = conv_w.shape[-1]
    qk = DIFF_HEADS * 2 * DIFF_HEAD_DIM
    vw = ab_w_in.shape[-1] - 2 * cc - 2 * qk
    bf = lambda w: w.astype(BF16)
    row = lambda w: w.reshape(1, -1)

    heads = d // RWKV_HEAD_DIM
    head_of = jnp.arange(d, dtype=jnp.int32) // RWKV_HEAD_DIM
    sel = (head_of[:, None] == jnp.arange(128, dtype=jnp.int32)[None, :]).astype(BF16)
    selt = sel.T
    assert heads <= 128

    h = x.reshape(n, d)
    mem2 = mem.reshape(bsz * n_mem, d)
    for i in range(depth):
        j = i // 2
        kv = _norm_linear(mem2, row(norm_mem[i]), bf(xattn_w_kv[i]), BF16).reshape(bsz, n_mem, 2 * d)
        if i % 2 == 0:
            lam_init = 0.8 - 0.6 * math.exp(-0.3 * i)
            u, q, k, v = _inproj(h, row(norm_mix[i]), bf(ab_w_in[j]), cc, qk, vw)
            lq = jnp.stack([diff_lq1[j], diff_lk1[j], diff_lq2[j], diff_lk2[j]])
            o = _diff_attention(q.reshape(bsz, t, qk), k.reshape(bsz, t, qk), v.reshape(bsz, t, vw),
                                rel_bias_table.reshape(-1), lq, row(diff_subln_g[j]), lam_init)
            h = _l0_tail(u.reshape(bsz, t, cc), o, h.reshape(bsz, t, d), conv_w[j], row(conv_b[j]),
                         row(conv_ln_g[j]), row(conv_ln_b[j]), bf(ab_w_out[j]), row(norm_xattn[i]),
                         bf(xattn_w_q[i]), kv, bf(xattn_w_o[i])).reshape(n, d)
        else:
            h3 = h.reshape(bsz, t, d)
            w1 = bf(jnp.concatenate([rwkv_w1[j, 0], rwkv_w1[j, 1]], axis=1))
            a1 = bf(jnp.concatenate([rwkv_a1[j, 0], rwkv_a1[j, 1]], axis=1))
            r, v, kn, gate, bonus, kd, cum, b = _rwkv_prep(
                h3, row(norm_mix[i]), rwkv_mu[j], bf(rwkv_w_r[j]), bf(rwkv_w_k[j]), bf(rwkv_w_v[j]),
                w1, bf(_pad_lora_out(rwkv_w2[j])), rwkv_w0[j], a1, bf(_pad_lora_out(rwkv_a2[j])), rwkv_a0[j],
                bf(rwkv_g1[j]), bf(rwkv_g2[j]), row(rwkv_k_k[j]), row(rwkv_k_a[j]), row(rwkv_r_k[j]),
                sel, selt)
            yf, yb = _wkv7_scan(r, v, kn, kd, cum, b)
            h = _l1_tail(yf, yb, bonus, gate, h3, row(rwkv_ln_g[j]), row(rwkv_ln_b[j]), bf(rwkv_w_o[j]), sel, selt,
                         row(norm_xattn[i]), bf(xattn_w_q[i]), kv, bf(xattn_w_o[i])).reshape(n, d)
        h = _mlp(h, row(norm_ffn[i]), bf(ffn_w_up[i]), bf(ffn_w_down[i]), row(norm_final),
                 final_norm=(i == depth - 1))
    return h.reshape(bsz, t, d)
```

```python
import functools
import math

import jax
import jax.numpy as jnp
from jax import lax
from jax.experimental import pallas as pl
from jax.experimental.pallas import tpu as pltpu

F32 = jnp.float32
BF16 = jnp.bfloat16

V7X_VMEM_BYTES = 64 * 1024 * 1024
VMEM_LIMIT_BYTES = V7X_VMEM_BYTES - 8 * 1024 * 1024

LOG2E = math.log2(math.e)
NORM_EPS = 1e-6
CONV_LN_EPS = 1e-5
GN_EPS = 64e-5
CONV_WIDTH = 31
CONV_PAD = CONV_WIDTH // 2
CONV_HALO = 16
DIFF_HEADS = 4
DIFF_HEAD_DIM = 64
REL_BUCKETS = 32
REL_MAX_DIST = 128
XATTN_HEADS = 4
RWKV_HEAD_DIM = 64
SUBLANES = 8
LANES = 128
CHUNK = 64
CHUNK_SHIFT = CHUNK.bit_length() - 1
PAIR = 2 * RWKV_HEAD_DIM


def _cparams(*sem):
    return pltpu.CompilerParams(dimension_semantics=sem, vmem_limit_bytes=VMEM_LIMIT_BYTES)


def _resident(shape):
    nd = len(shape)
    return pl.BlockSpec(shape, lambda *_: (0,) * nd, pipeline_mode=pl.Buffered(1))


def _rms(x, g):
    ms = jnp.mean(x * x, axis=-1, keepdims=True)
    return x * lax.rsqrt(ms + NORM_EPS) * g


def _sigmoid(x):
    return 1.0 / (1.0 + jnp.exp(-x))


def _dot(a, b):
    return jnp.dot(a.astype(BF16), b.astype(BF16), preferred_element_type=F32)


def _dot_nt(a, b):
    return lax.dot_general(a.astype(BF16), b.astype(BF16), (((1,), (1,)), ((), ())),
                           preferred_element_type=F32)


def _split_dot(mat, x):
    hi = x.astype(BF16)
    lo = (x - hi.astype(F32)).astype(BF16)
    return (jnp.dot(mat, hi, preferred_element_type=F32) + jnp.dot(mat, lo, preferred_element_type=F32))


def _run_staggered(gens, offset):
    results = [None] * len(gens)
    running = [True] * len(gens)
    rnd = 0
    while any(running):
        for i, gen in enumerate(gens):
            if rnd >= i * offset and running[i]:
                try:
                    next(gen)
                except StopIteration as stop:
                    results[i] = stop.value
                    running[i] = False
        rnd += 1
    return results


def _inproj_kernel(h_ref, hp_ref, hn_ref, g_ref, w_ref, cw_ref, cb_ref, cg_ref, cbeta_ref,
                   u_ref, q_ref, k_ref, v_ref, xp_ref, *, cc, qk, scale, rows):
    i = pl.program_id(1)
    last = pl.num_programs(1) - 1
    tm = h_ref.shape[1]
    g = g_ref[...]
    x = _rms(h_ref[0], g).astype(BF16)
    x_ext = jnp.concatenate([_rms(hp_ref[0], g).astype(BF16), x, _rms(hn_ref[0], g).astype(BF16)], axis=0)
    pu = _dot(x_ext, w_ref[:, :2 * cc])
    u = pu[:, :cc] * _sigmoid(pu[:, cc:])
    xp_ref[0:CONV_HALO, :] = u[:CONV_HALO] * jnp.where(i > 0, 1.0, 0.0)
    xp_ref[CONV_HALO:CONV_HALO + tm, :] = u[CONV_HALO:CONV_HALO + tm]
    xp_ref[CONV_HALO + tm:, :] = u[CONV_HALO + tm:] * jnp.where(i < last, 1.0, 0.0)

    def conv():
        for t0 in range(0, tm, rows):
            u_ref[0, t0:t0 + rows, :] = _conv_rows(xp_ref, t0, rows, cw_ref, cb_ref, cg_ref, cbeta_ref)
            yield

    def qkv():
        o = 2 * cc
        q_ref[0] = (_dot(x, w_ref[:, o:o + qk]) * scale).astype(BF16)
        yield
        k_ref[0] = _dot(x, w_ref[:, o + qk:o + 2 * qk]).astype(BF16)
        yield
        v_ref[0] = _dot(x, w_ref[:, o + 2 * qk:]).astype(BF16)

    _run_staggered([conv(), qkv()], 0)


def _inproj(h, g, w, cw, cb, cbg, cbeta, cc, qk, vw, tm=512, rows=64):
    bsz, t, d = h.shape
    nb = tm // CONV_HALO
    tile = lambda width: pl.BlockSpec((1, tm, width), lambda b, i: (b, i, 0))
    out = lambda width: jax.ShapeDtypeStruct((bsz, t, width), BF16)
    return pl.pallas_call(
        functools.partial(_inproj_kernel, cc=cc, qk=qk, scale=DIFF_HEAD_DIM ** -0.5 * LOG2E, rows=rows),
        grid=(bsz, t // tm),
        in_specs=[tile(d),
                  pl.BlockSpec((1, CONV_HALO, d), lambda b, i: (b, jnp.maximum(i * nb - 1, 0), 0)),
                  pl.BlockSpec((1, CONV_HALO, d),
                               lambda b, i: (b, jnp.minimum((i + 1) * nb, t // CONV_HALO - 1), 0)),
                  _resident((1, d)), _resident(w.shape), _resident(cw.shape), _resident((1, cc)),
                  _resident((1, cc)), _resident((1, cc))],
        out_specs=[tile(cc), tile(qk), tile(qk), tile(vw)],
        out_shape=[out(cc), out(qk), out(qk), out(vw)],
        scratch_shapes=[pltpu.VMEM((tm + 2 * CONV_HALO, cc), F32)],
        compiler_params=_cparams("parallel", "parallel"),
        name="l0_inproj",
    )(h, h, h, g, w, cw, cb, cbg, cbeta)


def _conv_rows(xp_ref, base, rows, w_ref, b_ref, g_ref, beta_ref):
    c = xp_ref.shape[-1]
    off = CONV_HALO - CONV_PAD
    span = rows + 2 * CONV_HALO
    pieces = []
    for c0 in range(0, c, LANES):
        win = xp_ref[base:base + span, c0:c0 + LANES]
        acc = jnp.zeros((rows, LANES), F32)
        for rem in range(SUBLANES):
            shifted = win if rem == 0 else pltpu.roll(win, span - rem, 0)
            for start in range(0, 2 * CONV_HALO, SUBLANES):
                k = start + rem - off
                if 0 <= k < CONV_WIDTH:
                    acc = acc + shifted[start:start + rows, :] * w_ref[k:k + 1, c0:c0 + LANES]
        pieces.append(acc)
    y = jnp.concatenate(pieces, axis=1) + b_ref[...]
    mu = jnp.mean(y, axis=-1, keepdims=True)
    yc = y - mu
    var = jnp.mean(yc * yc, axis=-1, keepdims=True)
    yn = yc * lax.rsqrt(var + CONV_LN_EPS) * g_ref[...] + beta_ref[...]
    return (yn * _sigmoid(yn)).astype(BF16)


def _t5_bucket(rel):
    nb = REL_BUCKETS // 2
    max_exact = nb // 2
    n = jnp.abs(rel)
    large = jnp.full(rel.shape, max_exact, jnp.int32)
    steps = nb - max_exact
    for m in range(1, steps):
        thr = math.ceil(max_exact * (REL_MAX_DIST / max_exact) ** (m / steps) - 1e-9)
        large = large + jnp.where(n >= thr, 1, 0)
    mag = jnp.where(n < max_exact, n, large)
    return mag + jnp.where(rel > 0, nb, 0)


def _diffattn_kernel(tbl_ref, lq_ref, sg_ref, q_ref, k_ref, v_ref, o_ref, bias_ref, *, tq, ts, kb, lam_init):
    h = pl.program_id(0)
    qi = pl.program_id(1)
    b = pl.program_id(2)
    t = k_ref.shape[1]

    @pl.when(b == 0)
    def _():
        u = lax.broadcasted_iota(jnp.int32, (1, t + tq), 1)
        bucket = _t5_bucket(u - (tq - 1) - qi * tq)
        line = jnp.zeros((1, t + tq), F32)
        for i in range(REL_BUCKETS):
            line = jnp.where(bucket == i, tbl_ref[i * DIFF_HEADS + h], line)
        rows = pltpu.roll(jnp.broadcast_to(line * LOG2E, (tq, t + tq)), 1, 1, stride=1, stride_axis=0)
        bias_ref[...] = rows[:, tq:]

    lq = lq_ref[...]
    lam = (jnp.exp(jnp.sum(lq[0:1] * lq[1:2], axis=-1, keepdims=True))
           - jnp.exp(jnp.sum(lq[2:3] * lq[3:4], axis=-1, keepdims=True)) + lam_init)
    hw = q_ref.shape[-1]
    first = lax.broadcasted_iota(jnp.int32, (ts, hw), 1) < DIFF_HEAD_DIM
    nkb = t // kb

    def softmax_v(r0, comp):
        q = q_ref[0, r0:r0 + ts, :]
        qc = jnp.where(first, q, jnp.zeros_like(q)) if comp == 0 else jnp.where(first, jnp.zeros_like(q), q)
        s = []
        mx = None
        for j in range(nkb):
            sj = _dot_nt(qc, k_ref[0, j * kb:(j + 1) * kb, :]) + bias_ref[r0:r0 + ts, j * kb:(j + 1) * kb]
            mj = jnp.max(sj, axis=-1, keepdims=True)
            mx = mj if mx is None else jnp.maximum(mx, mj)
            s.append(sj)
            yield
        pv = None
        for j in range(nkb):
            vj = v_ref[0, j * kb:(j + 1) * kb, :]
            dj = _dot(jnp.exp2(s[j] - mx), jnp.concatenate([vj, jnp.ones_like(vj)], axis=1))
            pv = dj if pv is None else pv + dj
            yield
        return pv[:, :hw] / pv[:, hw:]

    starts = range(0, tq, ts)
    maps = _run_staggered([softmax_v(r0, comp) for r0 in starts for comp in range(2)], nkb)
    for i, r0 in enumerate(starts):
        o = maps[2 * i] - lam * maps[2 * i + 1]
        o = o * lax.rsqrt(jnp.mean(o * o, axis=-1, keepdims=True) + NORM_EPS) * sg_ref[...] * (1.0 - lam_init)
        o_ref[0, r0:r0 + ts, :] = o.astype(BF16)


def _diff_attention(q, k, v, tbl, lq, sg, lam_init, tq=1024, ts=256, kb=256):
    bsz, t, _ = q.shape
    hw = 2 * DIFF_HEAD_DIM
    return pl.pallas_call(
        functools.partial(_diffattn_kernel, tq=tq, ts=ts, kb=kb, lam_init=lam_init),
        grid=(DIFF_HEADS, t // tq, bsz),
        in_specs=[pl.BlockSpec(memory_space=pltpu.SMEM), _resident(lq.shape), _resident(sg.shape),
                  pl.BlockSpec((1, tq, hw), lambda h, i, b: (b, i, h)),
                  pl.BlockSpec((1, t, hw), lambda h, i, b: (b, 0, h)),
                  pl.BlockSpec((1, t, hw), lambda h, i, b: (b, 0, h))],
        out_specs=pl.BlockSpec((1, tq, hw), lambda h, i, b: (b, i, h)),
        out_shape=jax.ShapeDtypeStruct(q.shape, BF16),
        scratch_shapes=[pltpu.VMEM((tq, t), F32)],
        compiler_params=_cparams("parallel", "parallel", "arbitrary"),
        name="l0_diffattn",
    )(tbl, lq, sg, q, k, v)


def _norm_linear_kernel(x_ref, g_ref, w_ref, o_ref):
    o_ref[...] = _dot(_rms(x_ref[...], g_ref[...]), w_ref[...]).astype(o_ref.dtype)


def _norm_linear(x, g, w, out_dtype, tm=512):
    n, d = x.shape
    m = w.shape[1]
    return pl.pallas_call(
        _norm_linear_kernel,
        grid=(n // tm,),
        in_specs=[pl.BlockSpec((tm, d), lambda i: (i, 0)), _resident((1, d)), _resident(w.shape)],
        out_specs=pl.BlockSpec((tm, m), lambda i: (i, 0)),
        out_shape=jax.ShapeDtypeStruct((n, m), out_dtype),
        compiler_params=_cparams("parallel"),
        name="norm_linear",
    )(x, g, w)


def _xattn_rows(h, g_ref, wq_ref, kv_ref, wo_ref):
    d = h.shape[-1]
    hd = d // XATTN_HEADS
    q = (_dot(_rms(h, g_ref[...]), wq_ref[...]) * (hd ** -0.5 * LOG2E)).astype(BF16)
    yield
    outs = []
    for i in range(XATTN_HEADS):
        kh = kv_ref[0, :, i * hd:(i + 1) * hd]
        vh = kv_ref[0, :, d + i * hd:d + (i + 1) * hd]
        s = _dot_nt(q[:, i * hd:(i + 1) * hd], kh)
        e = jnp.exp2(s - jnp.max(s, axis=-1, keepdims=True))
        p = e / jnp.sum(e, axis=-1, keepdims=True)
        outs.append(_dot(p, vh).astype(BF16))
    yield
    return h + _dot(jnp.concatenate(outs, axis=1), wo_ref[...])


def _l0_tail_kernel(u_ref, o_ref, h_ref, wout_ref, gx_ref, wq_ref, kv_ref, wo_ref, out_ref, *, ts):
    tq = h_ref.shape[1]
    cc = u_ref.shape[-1]

    def sub(r0):
        rs = slice(r0, r0 + ts)
        h1 = h_ref[0, rs, :] + _dot(u_ref[0, rs, :], wout_ref[:cc, :]) + _dot(o_ref[0, rs, :], wout_ref[cc:, :])
        yield
        out_ref[0, rs, :] = yield from _xattn_rows(h1, gx_ref, wq_ref, kv_ref, wo_ref)

    _run_staggered([sub(r0) for r0 in range(0, tq, ts)], 1)


def _l0_tail(u, o, h, wout, gx, wq, kv, wo, tq=512, ts=256):
    bsz, t, d = h.shape
    tile = lambda w: pl.BlockSpec((1, tq, w), lambda b, i: (b, i, 0))
    return pl.pallas_call(
        functools.partial(_l0_tail_kernel, ts=ts),
        grid=(bsz, t // tq),
        in_specs=[tile(u.shape[-1]), tile(o.shape[-1]), tile(d), _resident(wout.shape), _resident((1, d)),
                  _resident(wq.shape), pl.BlockSpec((1, kv.shape[1], 2 * d), lambda b, i: (b, 0, 0)),
                  _resident(wo.shape)],
        out_specs=tile(d),
        out_shape=jax.ShapeDtypeStruct(h.shape, F32),
        compiler_params=_cparams("parallel", "parallel"),
        name="l0_tail",
    )(u, o, h, wout, gx, wq, kv, wo)


def _mlp_kernel(h_ref, g_ref, wu_ref, wd_ref, gf_ref, o_ref, *, hc, final_norm):
    h = h_ref[...]
    xn = _rms(h, g_ref[...]).astype(BF16)
    acc = h
    for c in range(wu_ref.shape[1] // hc):
        a = jnp.maximum(_dot(xn, wu_ref[:, c * hc:(c + 1) * hc]), 0.0)
        acc = acc + _dot(a * a, wd_ref[c * hc:(c + 1) * hc, :])
    if final_norm:
        acc = _rms(acc, gf_ref[...])
    o_ref[...] = acc


def _mlp(h, g, wu, wd, gf, final_norm, tm=512, hc=1024):
    n, d = h.shape
    row = lambda i: (i, 0)
    return pl.pallas_call(
        functools.partial(_mlp_kernel, hc=hc, final_norm=final_norm),
        grid=(n // tm,),
        in_specs=[pl.BlockSpec((tm, d), row), _resident((1, d)), _resident(wu.shape), _resident(wd.shape),
                  _resident((1, d))],
        out_specs=pl.BlockSpec((tm, d), row),
        out_shape=jax.ShapeDtypeStruct((n, d), F32),
        compiler_params=_cparams("parallel"),
        name="mlp",
    )(h, g, wu, wd, gf)


def _rwkv_prep_kernel(h_ref, hp_ref, hn_ref, g_ref, mu_ref, wr_ref, wk_ref, wv_ref, w1_ref, w2_ref, w0_ref,
                      a1_ref, a2_ref, a0_ref, g1_ref, g2_ref, kk_ref, ka_ref, rk_ref, sel_ref, selt_ref,
                      r_out, v_out, kn_out, gate_out, bonus_out, kd_out, cum_out, b_out, *, ts):
    i = pl.program_id(1)
    last = pl.num_programs(1) - 1
    tm = h_ref.shape[1]
    g = g_ref[...]
    ti = lax.broadcasted_iota(jnp.int32, (ts, ts), 0)
    si = lax.broadcasted_iota(jnp.int32, (ts, ts), 1)
    same_chunk = (ti >> CHUNK_SHIFT) == (si >> CHUNK_SHIFT)
    before = (jnp.where(same_chunk, jnp.where(si <= ti, 1.0, 0.0), 0.0).astype(BF16),
              jnp.where(same_chunk, jnp.where(si >= ti, 1.0, 0.0), 0.0).astype(BF16))

    def rows(r0):
        x = _rms(h_ref[0, r0:r0 + ts, :], g)
        if r0 == 0:
            prev_row = _rms(hp_ref[0], g)[SUBLANES - 1:, :] * jnp.where(i > 0, 1.0, 0.0)
        else:
            prev_row = _rms(h_ref[0, r0 - SUBLANES:r0, :], g)[SUBLANES - 1:, :]
        if r0 + ts == tm:
            next_row = _rms(hn_ref[0], g)[0:1, :] * jnp.where(i < last, 1.0, 0.0)
        else:
            next_row = _rms(h_ref[0, r0 + ts:r0 + ts + SUBLANES, :], g)[0:1, :]
        rowid = lax.broadcasted_iota(jnp.int32, x.shape, 0)
        x_prev = jnp.where(rowid == 0, prev_row, pltpu.roll(x, 1, 0))
        x_next = jnp.where(rowid == ts - 1, next_row, pltpu.roll(x, ts - 1, 0))
        hh = 0.5 * (x_prev + x_next) - x
        mix = lambda j: x + hh * mu_ref[j:j + 1, :]
        r = _dot(mix(0), wr_ref[...])
        k = _dot(mix(2), wk_ref[...])
        v = _dot(mix(3), wv_ref[...])
        yield
        gate_in = _dot(mix(5), g1_ref[...])
        lw = _dot(mix(1), w1_ref[...])
        la = _dot(mix(4), a1_ref[...])
        yield
        gate = _dot(_sigmoid(gate_in), g2_ref[...])
        lw = jnp.tanh(lw)
        w_pre = [w0_ref[z:z + 1, :] + _dot(lw, w2_ref[z]) for z in range(2)]
        a_pre = [a0_ref[z:z + 1, :] + _dot(la, a2_ref[z]) for z in range(2)]
        kk = k * kk_ref[...]
        ss = _dot(kk * kk, sel_ref[...])
        yield
        kn = kk * lax.rsqrt(jnp.maximum(_dot(ss, selt_ref[...]), 1e-24))
        kka = k * ka_ref[...]
        kd_sum = jnp.zeros_like(k)
        cum = []
        for z in range(2):
            cum.append(_split_dot(before[z], _sigmoid(w_pre[z]) * (-math.exp(-0.5))))
            rate = _sigmoid(a_pre[z])
            kd = k + kka * (rate - 1.0)
            kd_out[z, 0, r0:r0 + ts, :] = kd.astype(kd_out.dtype)
            b_out[z, 0, r0:r0 + ts, :] = (kn * rate).astype(b_out.dtype)
            kd_sum = kd_sum + kd
        bs = _dot(r * kd_sum * rk_ref[...], sel_ref[...])
        yield
        r_out[0, r0:r0 + ts, :] = r.astype(r_out.dtype)
        v_out[0, r0:r0 + ts, :] = v.astype(v_out.dtype)
        kn_out[0, r0:r0 + ts, :] = kn.astype(kn_out.dtype)
        gate_out[0, r0:r0 + ts, :] = gate.astype(gate_out.dtype)
        bonus_out[0, r0:r0 + ts, :] = (_dot(bs, selt_ref[...]) * v).astype(bonus_out.dtype)
        for z in range(2):
            cum_out[z, 0, r0:r0 + ts, :] = cum[z]

    _run_staggered([rows(r0) for r0 in range(0, tm, ts)], 1)


def _rwkv_prep(h, g, mu, wr, wk, wv, w1, w2, w0, a1, a2, a0, g1, g2, kk, ka, rk, sel, selt, tm=256, ts=128):
    bsz, t, d = h.shape
    nb = tm // SUBLANES
    tile = pl.BlockSpec((1, tm, d), lambda b, i: (b, i, 0))
    tile2 = pl.BlockSpec((2, 1, tm, d), lambda b, i: (0, b, i, 0))
    one = jax.ShapeDtypeStruct((bsz, t, d), BF16)
    two = jax.ShapeDtypeStruct((2, bsz, t, d), BF16)
    consts = [g, mu, wr, wk, wv, w1, w2, w0, a1, a2, a0, g1, g2, kk, ka, rk, sel, selt]
    return pl.pallas_call(
        functools.partial(_rwkv_prep_kernel, ts=ts),
        grid=(bsz, t // tm),
        in_specs=[tile,
                  pl.BlockSpec((1, SUBLANES, d), lambda b, i: (b, jnp.maximum(i * nb - 1, 0), 0)),
                  pl.BlockSpec((1, SUBLANES, d),
                               lambda b, i: (b, jnp.minimum((i + 1) * nb, t // SUBLANES - 1), 0))]
                 + [_resident(c.shape) for c in consts],
        out_specs=[tile, tile, tile, tile, tile, tile2, tile2, tile2],
        out_shape=[one, one, one, one, one, two, jax.ShapeDtypeStruct((2, bsz, t, d), F32), two],
        compiler_params=_cparams("parallel", "parallel"),
        name="l1_rwkv_prep",
    )(h, h, h, *consts)


def _blockdiag(x):
    lane = lax.broadcasted_iota(jnp.int32, x.shape, 1)
    head0 = (lane & (PAIR - 1)) < RWKV_HEAD_DIM
    zero = jnp.zeros_like(x)
    return jnp.concatenate([jnp.where(head0, x, zero), jnp.where(head0, zero, x)], axis=0)


def _chunk_local(r, k, v, kn, cum, b, rev):
    c = CHUNK
    bd = _blockdiag
    row = lax.broadcasted_iota(jnp.int32, (c, PAIR), 0)
    lane = lax.broadcasted_iota(jnp.int32, (c, PAIR), 1)
    s_idx = lane & (c - 1)
    if rev:
        cum_prev = jnp.where(row == c - 1, 0.0, pltpu.roll(cum, c - 1, 0))
        tot = cum[0:1, :]
        strict = s_idx > row
        incl = s_idx >= row
    else:
        cum_prev = jnp.where(row == 0, 0.0, pltpu.roll(cum, 1, 0))
        tot = cum[c - 1:c, :]
        strict = s_idx < row
        incl = s_idx <= row
    w_incl = jnp.exp(cum)
    w_excl = jnp.exp(cum_prev)
    w_inv = jnp.exp(-cum)
    w_tot = jnp.exp(tot)
    w_rest = jnp.exp(tot - cum)
    a_t = -kn * w_excl
    r_t = r * w_incl
    b_t = b * w_inv
    k_t = k * w_inv
    b_h = b * w_rest
    k_h = k * w_rest
    same_blk = (s_idx >> 4) == (row >> 4)

    sc = _dot_nt(jnp.concatenate([a_t, r_t], axis=0), jnp.concatenate([bd(b_t), bd(k_t)], axis=0))
    yield
    p_ab = jnp.where(strict, sc[:c, :PAIR], 0.0)
    p_ak = jnp.where(strict, sc[:c, PAIR:], 0.0)
    p_rb = jnp.where(incl, sc[c:, :PAIR], 0.0)
    p_rk = jnp.where(incl, sc[c:, PAIR:], 0.0)
    dm = jnp.where(same_blk, p_ab, 0.0)
    em = p_ab - dm
    x2 = _dot(dm, bd(dm))
    av = _dot(p_ak, bd(v))
    yield
    td = jnp.where(s_idx == row, 1.0, 0.0) + dm
    both = _dot(jnp.concatenate([x2, td], axis=0), bd(x2))
    yield
    x4 = both[:c]
    td = td + both[c:]
    both = _dot(jnp.concatenate([x4, td], axis=0), bd(x4))
    yield
    td = td + both[c:]
    td = td + _dot(td, bd(both[:c]))
    yield
    ty = _dot(td, bd(jnp.concatenate([a_t, av, em], axis=1)))
    yield
    au, f1 = ty[:, :2 * PAIR], ty[:, 2 * PAIR:]
    both = _dot(f1, bd(jnp.concatenate([f1, au], axis=1)))
    f2 = both[:, :PAIR]
    au = au + both[:, PAIR:]
    yield
    au = au + _dot(f2, bd(au))
    yield
    rhs = jnp.concatenate([bd(au), jnp.concatenate([jnp.zeros((PAIR, PAIR), F32), bd(v)], axis=1)], axis=0)
    ry = _dot(jnp.concatenate([p_rb, p_rk], axis=1), rhs)
    rhs2 = jnp.concatenate([au, jnp.concatenate([jnp.zeros((c, PAIR), F32), v], axis=1)], axis=0)
    mg = _dot(jnp.concatenate([b_h, k_h], axis=0).T, rhs2)
    yield
    rbar = r_t + ry[:, :PAIR]
    yloc = ry[:, PAIR:]
    r2 = lax.broadcasted_iota(jnp.int32, (PAIR, PAIR), 0)
    l2 = lax.broadcasted_iota(jnp.int32, (PAIR, PAIR), 1)
    same_head = (r2 >> 6) == (l2 >> 6)
    m = jnp.where(same_head, mg[:, :PAIR], 0.0) + jnp.where(r2 == l2, w_tot, 0.0)
    gg = jnp.where(same_head, mg[:, PAIR:], 0.0)
    return rbar, yloc, m, gg


def _scan_kernel(r_ref, v_ref, kn_ref, kd_ref, cum_ref, b_ref, yf_ref, yb_ref, ds_ref, loc_a, loc_b, *, unroll):
    nc = r_ref.shape[1] // CHUNK
    groups = nc // unroll
    total = (r_ref.shape[2] // PAIR) * groups
    y_refs = (yf_ref, yb_ref)
    c = CHUNK
    ds_ref[...] = jnp.zeros(ds_ref.shape, F32)
    loc_b[...] = jnp.zeros(loc_b.shape, F32)

    def places(gidx):
        grp = gidx % groups
        lanes = pl.ds(pl.multiple_of((gidx // groups) * PAIR, PAIR), PAIR)
        out = []
        for u in range(unroll):
            for z in range(2):
                cidx = grp * unroll + u
                if z == 1:
                    cidx = nc - 1 - cidx
                out.append((z, pl.ds(pl.multiple_of(cidx * c, c), c), lanes))
        return out

    def local_terms(gidx, loc_ref):
        f32 = lambda ref, *idx: ref[idx].astype(F32)
        gens = [_chunk_local(f32(r_ref, 0, sl, ln), f32(kd_ref, z, 0, sl, ln), f32(v_ref, 0, sl, ln),
                             f32(kn_ref, 0, sl, ln), cum_ref[z, 0, sl, ln], f32(b_ref, z, 0, sl, ln),
                             rev=(z == 1))
                for z, sl, ln in places(gidx)]

        def park(j, gen):
            rbar, yloc, m, gg = yield from gen
            loc_ref[j, 0:c, :] = rbar
            loc_ref[j, c:2 * c, :] = yloc
            loc_ref[j, 2 * c:2 * c + PAIR, :] = m
            loc_ref[j, 2 * c + PAIR:, :] = gg

        return [park(j, gen) for j, gen in enumerate(gens)]

    def recurrence(gidx, loc_ref):
        keep = jnp.where(gidx % groups == 0, 0.0, 1.0)
        ds = [ds_ref[0] * keep, ds_ref[1] * keep]
        for j, (z, sl, ln) in enumerate(places(gidx)):
            both = _dot(jnp.concatenate([loc_ref[j, 0:c, :], loc_ref[j, 2 * c:2 * c + PAIR, :]], axis=0), ds[z])
            y_refs[z][0, sl, ln] = (both[:c] + loc_ref[j, c:2 * c, :]).astype(y_refs[z].dtype)
            ds[z] = both[c:] + loc_ref[j, 2 * c + PAIR:, :]
            if z == 1:
                yield
        ds_ref[0] = ds[0]
        ds_ref[1] = ds[1]

    def body(it, carry):
        first = 2 * it
        _run_staggered(local_terms(first, loc_a) + [recurrence(jnp.maximum(first - 1, 0), loc_b)], 0)
        _run_staggered(local_terms(first + 1, loc_b) + [recurrence(first, loc_a)], 0)
        return carry

    lax.fori_loop(0, total // 2, body, 0)
    _run_staggered([recurrence(total - 1, loc_b)], 0)


def _wkv7_scan(r, v, kn, kd, cum, b, unroll=8, pairs=2):
    bsz, t, d = r.shape
    nc = t // CHUNK
    unroll = min(unroll, nc // 2)
    assert nc % (2 * unroll) == 0, "the scan kernel takes chunk groups in pairs"
    one = pl.BlockSpec((1, t, pairs * PAIR), lambda bb, p: (bb, 0, p))
    two = pl.BlockSpec((2, 1, t, pairs * PAIR), lambda bb, p: (0, bb, 0, p))
    out = jax.ShapeDtypeStruct((bsz, t, d), BF16)
    return pl.pallas_call(
        functools.partial(_scan_kernel, unroll=unroll),
        grid=(bsz, d // (pairs * PAIR)),
        in_specs=[one, one, one, two, two, two],
        out_specs=[one, one],
        out_shape=[out, out],
        scratch_shapes=[pltpu.VMEM((2, PAIR, PAIR), F32)]
                       + [pltpu.VMEM((2 * unroll, 2 * CHUNK + 2 * PAIR, PAIR), F32)] * 2,
        compiler_params=_cparams("parallel", "parallel"),
        name="l1_wkv7_scan",
    )(r, v, kn, kd, cum, b)


def _l1_tail_kernel(yf_ref, yb_ref, bonus_ref, gate_ref, h_ref, lg_ref, lb_ref, wo_ref, sel_ref, selt_ref,
                    gx_ref, wq_ref, kv_ref, wxo_ref, out_ref, *, ts):
    tq = h_ref.shape[1]
    inv_n = 1.0 / RWKV_HEAD_DIM

    def sub(r0):
        rs = slice(r0, r0 + ts)
        y = yf_ref[0, rs, :] + yb_ref[0, rs, :]
        mu = _dot(y, sel_ref[...])
        yield
        yc = y - _dot(mu, selt_ref[...]) * inv_n
        var = _dot(yc * yc, sel_ref[...])
        yield
        yn = yc * lax.rsqrt(_dot(var, selt_ref[...]) * inv_n + GN_EPS) * lg_ref[...] + lb_ref[...]
        h1 = h_ref[0, rs, :] + _dot((yn + bonus_ref[0, rs, :]) * gate_ref[0, rs, :], wo_ref[...])
        yield
        out_ref[0, rs, :] = yield from _xattn_rows(h1, gx_ref, wq_ref, kv_ref, wxo_ref)

    _run_staggered([sub(r0) for r0 in range(0, tq, ts)], 2)


def _l1_tail(yf, yb, bonus, gate, h, lg, lb, wo, sel, selt, gx, wq, kv, wxo, tq=512, ts=256):
    bsz, t, d = h.shape
    tile = pl.BlockSpec((1, tq, d), lambda b, i: (b, i, 0))
    return pl.pallas_call(
        functools.partial(_l1_tail_kernel, ts=ts),
        grid=(bsz, t // tq),
        in_specs=[tile, tile, tile, tile, tile, _resident((1, d)), _resident((1, d)), _resident(wo.shape),
                  _resident(sel.shape), _resident(selt.shape), _resident((1, d)), _resident(wq.shape),
                  pl.BlockSpec((1, kv.shape[1], 2 * d), lambda b, i: (b, 0, 0)), _resident(wxo.shape)],
        out_specs=tile,
        out_shape=jax.ShapeDtypeStruct(h.shape, F32),
        compiler_params=_cparams("parallel", "parallel"),
        name="l1_tail",
    )(yf, yb, bonus, gate, h, lg, lb, wo, sel, selt, gx, wq, kv, wxo)


def _pad_lora_out(w2):
    zero = jnp.zeros_like(w2[0])
    return jnp.stack([jnp.concatenate([w2[0], zero], axis=0), jnp.concatenate([zero, w2[1]], axis=0)])


def kernel(x, mem, rel_bias_table, norm_mix, norm_xattn, norm_mem, norm_ffn, norm_final, ab_w_in, ab_w_out, conv_w, conv_b, conv_ln_g, conv_ln_b, diff_lq1, diff_lk1, diff_lq2, diff_lk2, diff_subln_g, rwkv_mu, rwkv_w_r, rwkv_w_k, rwkv_w_v, rwkv_w_o, rwkv_w0, rwkv_w1, rwkv_w2, rwkv_a0, rwkv_a1, rwkv_a2, rwkv_g1, rwkv_g2, rwkv_k_k, rwkv_k_a, rwkv_r_k, rwkv_ln_g, rwkv_ln_b, xattn_w_q, xattn_w_kv, xattn_w_o, ffn_w_up, ffn_w_down):
    bsz, t, d = x.shape
    n = bsz * t
    depth = norm_mix.shape[0]
    n_mem = mem.shape[1]
    cc = conv_w.shape[-1]
    qk = DIFF_HEADS * 2 * DIFF_HEAD_DIM
    vw = ab_w_in.shape[-1] - 2 * cc - 2 * qk
    bf = lambda w: w.astype(BF16)
    row = lambda w: w.reshape(1, -1)

    heads = d // RWKV_HEAD_DIM
    head_of = jnp.arange(d, dtype=jnp.int32) // RWKV_HEAD_DIM
    sel = (head_of[:, None] == jnp.arange(128, dtype=jnp.int32)[None, :]).astype(BF16)
    selt = sel.T
    assert heads <= 128

    h = x.reshape(n, d)
    mem2 = mem.reshape(bsz * n_mem, d)
    for i in range(depth):
        j = i // 2
        kv = _norm_linear(mem2, row(norm_mem[i]), bf(xattn_w_kv[i]), BF16).reshape(bsz, n_mem, 2 * d)
        if i % 2 == 0:
            lam_init = 0.8 - 0.6 * math.exp(-0.3 * i)
            h3 = h.reshape(bsz, t, d)
            u, q, k, v = _inproj(h3, row(norm_mix[i]), bf(ab_w_in[j]), conv_w[j], row(conv_b[j]),
                                 row(conv_ln_g[j]), row(conv_ln_b[j]), cc, qk, vw)
            lq = jnp.stack([diff_lq1[j], diff_lk1[j], diff_lq2[j], diff_lk2[j]])
            o = _diff_attention(q, k, v, rel_bias_table.reshape(-1), lq, row(diff_subln_g[j]), lam_init)
            h = _l0_tail(u, o, h3, bf(ab_w_out[j]), row(norm_xattn[i]), bf(xattn_w_q[i]), kv,
                         bf(xattn_w_o[i])).reshape(n, d)
        else:
            h3 = h.reshape(bsz, t, d)
            w1 = bf(jnp.concatenate([rwkv_w1[j, 0], rwkv_w1[j, 1]], axis=1))
            a1 = bf(jnp.concatenate([rwkv_a1[j, 0], rwkv_a1[j, 1]], axis=1))
            r, v, kn, gate, bonus, kd, cum, b = _rwkv_prep(
                h3, row(norm_mix[i]), rwkv_mu[j], bf(rwkv_w_r[j]), bf(rwkv_w_k[j]), bf(rwkv_w_v[j]),
                w1, bf(_pad_lora_out(rwkv_w2[j])), rwkv_w0[j], a1, bf(_pad_lora_out(rwkv_a2[j])), rwkv_a0[j],
                bf(rwkv_g1[j]), bf(rwkv_g2[j]), row(rwkv_k_k[j]), row(rwkv_k_a[j]), row(rwkv_r_k[j]),
                sel, selt)
            yf, yb = _wkv7_scan(r, v, kn, kd, cum, b)
            h = _l1_tail(yf, yb, bonus, gate, h3, row(rwkv_ln_g[j]), row(rwkv_ln_b[j]), bf(rwkv_w_o[j]), sel, selt,
                         row(norm_xattn[i]), bf(xattn_w_q[i]), kv, bf(xattn_w_o[i])).reshape(n, d)
        h = _mlp(h, row(norm_ffn[i]), bf(ffn_w_up[i]), bf(ffn_w_down[i]), row(norm_final),
                 final_norm=(i == depth - 1))
    return h.reshape(bsz, t, d)
```

```python
import functools
import math

import jax
import jax.numpy as jnp
from jax import lax
from jax.experimental import pallas as pl
from jax.experimental.pallas import tpu as pltpu

F32 = jnp.float32
BF16 = jnp.bfloat16

V7X_VMEM_BYTES = 64 * 1024 * 1024
VMEM_LIMIT_BYTES = V7X_VMEM_BYTES - 8 * 1024 * 1024

LOG2E = math.log2(math.e)
NORM_EPS = 1e-6
CONV_LN_EPS = 1e-5
GN_EPS = 64e-5
CONV_WIDTH = 31
CONV_PAD = CONV_WIDTH // 2
CONV_HALO = 16
DIFF_HEADS = 4
DIFF_HEAD_DIM = 64
REL_BUCKETS = 32
REL_MAX_DIST = 128
XATTN_HEADS = 4
RWKV_HEAD_DIM = 64
SUBLANES = 8
LANES = 128
CHUNK = 64
CHUNK_SHIFT = CHUNK.bit_length() - 1
PAIR = 2 * RWKV_HEAD_DIM


def _cparams(*sem):
    return pltpu.CompilerParams(dimension_semantics=sem, vmem_limit_bytes=VMEM_LIMIT_BYTES)


def _resident(shape):
    nd = len(shape)
    return pl.BlockSpec(shape, lambda *_: (0,) * nd, pipeline_mode=pl.Buffered(1))


def _rms(x, g):
    ms = jnp.mean(x * x, axis=-1, keepdims=True)
    return x * lax.rsqrt(ms + NORM_EPS) * g


def _sigmoid(x):
    return 1.0 / (1.0 + jnp.exp(-x))


def _dot(a, b):
    return jnp.dot(a.astype(BF16), b.astype(BF16), preferred_element_type=F32)


def _dot_nt(a, b):
    return lax.dot_general(a.astype(BF16), b.astype(BF16), (((1,), (1,)), ((), ())),
                           preferred_element_type=F32)


def _split_dot(mat, x):
    hi = x.astype(BF16)
    lo = (x - hi.astype(F32)).astype(BF16)
    return (jnp.dot(mat, hi, preferred_element_type=F32) + jnp.dot(mat, lo, preferred_element_type=F32))


def _run_staggered(gens, offset):
    results = [None] * len(gens)
    running = [True] * len(gens)
    rnd = 0
    while any(running):
        for i, gen in enumerate(gens):
            if rnd >= i * offset and running[i]:
                try:
                    next(gen)
                except StopIteration as stop:
                    results[i] = stop.value
                    running[i] = False
        rnd += 1
    return results


def _inproj_kernel(h_ref, hp_ref, hn_ref, g_ref, w_ref, cw_ref, cb_ref, cg_ref, cbeta_ref,
                   u_ref, q_ref, k_ref, v_ref, xp_ref, *, cc, qk, scale, rows):
    i = pl.program_id(1)
    last = pl.num_programs(1) - 1
    tm = h_ref.shape[1]
    g = g_ref[...]
    x = _rms(h_ref[0], g).astype(BF16)
    x_ext = jnp.concatenate([_rms(hp_ref[0], g).astype(BF16), x, _rms(hn_ref[0], g).astype(BF16)], axis=0)
    pu = _dot(x_ext, w_ref[:, :2 * cc])
    u = pu[:, :cc] * _sigmoid(pu[:, cc:])
    xp_ref[0:CONV_HALO, :] = u[:CONV_HALO] * jnp.where(i > 0, 1.0, 0.0)
    xp_ref[CONV_HALO:CONV_HALO + tm, :] = u[CONV_HALO:CONV_HALO + tm]
    xp_ref[CONV_HALO + tm:, :] = u[CONV_HALO + tm:] * jnp.where(i < last, 1.0, 0.0)

    def conv():
        for t0 in range(0, tm, rows):
            u_ref[0, t0:t0 + rows, :] = _conv_rows(xp_ref, t0, rows, cw_ref, cb_ref, cg_ref, cbeta_ref)
            yield

    def qkv():
        o = 2 * cc
        q_ref[0] = (_dot(x, w_ref[:, o:o + qk]) * scale).astype(BF16)
        yield
        k_ref[0] = _dot(x, w_ref[:, o + qk:o + 2 * qk]).astype(BF16)
        yield
        v_ref[0] = _dot(x, w_ref[:, o + 2 * qk:]).astype(BF16)

    _run_staggered([conv(), qkv()], 0)


def _inproj(h, g, w, cw, cb, cbg, cbeta, cc, qk, vw, tm=512, rows=64):
    bsz, t, d = h.shape
    nb = tm // CONV_HALO
    tile = lambda width: pl.BlockSpec((1, tm, width), lambda b, i: (b, i, 0))
    out = lambda width: jax.ShapeDtypeStruct((bsz, t, width), BF16)
    return pl.pallas_call(
        functools.partial(_inproj_kernel, cc=cc, qk=qk, scale=DIFF_HEAD_DIM ** -0.5 * LOG2E, rows=rows),
        grid=(bsz, t // tm),
        in_specs=[tile(d),
                  pl.BlockSpec((1, CONV_HALO, d), lambda b, i: (b, jnp.maximum(i * nb - 1, 0), 0)),
                  pl.BlockSpec((1, CONV_HALO, d),
                               lambda b, i: (b, jnp.minimum((i + 1) * nb, t // CONV_HALO - 1), 0)),
                  _resident((1, d)), _resident(w.shape), _resident(cw.shape), _resident((1, cc)),
                  _resident((1, cc)), _resident((1, cc))],
        out_specs=[tile(cc), tile(qk), tile(qk), tile(vw)],
        out_shape=[out(cc), out(qk), out(qk), out(vw)],
        scratch_shapes=[pltpu.VMEM((tm + 2 * CONV_HALO, cc), F32)],
        compiler_params=_cparams("parallel", "parallel"),
        name="l0_inproj",
    )(h, h, h, g, w, cw, cb, cbg, cbeta)


def _conv_rows(xp_ref, base, rows, w_ref, b_ref, g_ref, beta_ref):
    c = xp_ref.shape[-1]
    off = CONV_HALO - CONV_PAD
    span = rows + 2 * CONV_HALO
    pieces = []
    for c0 in range(0, c, LANES):
        win = xp_ref[base:base + span, c0:c0 + LANES]
        acc = jnp.zeros((rows, LANES), F32)
        for rem in range(SUBLANES):
            shifted = win if rem == 0 else pltpu.roll(win, span - rem, 0)
            for start in range(0, 2 * CONV_HALO, SUBLANES):
                k = start + rem - off
                if 0 <= k < CONV_WIDTH:
                    acc = acc + shifted[start:start + rows, :] * w_ref[k:k + 1, c0:c0 + LANES]
        pieces.append(acc)
    y = jnp.concatenate(pieces, axis=1) + b_ref[...]
    mu = jnp.mean(y, axis=-1, keepdims=True)
    yc = y - mu
    var = jnp.mean(yc * yc, axis=-1, keepdims=True)
    yn = yc * lax.rsqrt(var + CONV_LN_EPS) * g_ref[...] + beta_ref[...]
    return (yn * _sigmoid(yn)).astype(BF16)


def _t5_bucket(rel):
    nb = REL_BUCKETS // 2
    max_exact = nb // 2
    n = jnp.abs(rel)
    large = jnp.full(rel.shape, max_exact, jnp.int32)
    steps = nb - max_exact
    for m in range(1, steps):
        thr = math.ceil(max_exact * (REL_MAX_DIST / max_exact) ** (m / steps) - 1e-9)
        large = large + jnp.where(n >= thr, 1, 0)
    mag = jnp.where(n < max_exact, n, large)
    return mag + jnp.where(rel > 0, nb, 0)


def _diffattn_kernel(tbl_ref, lq_ref, sg_ref, q_ref, k_ref, v_ref, o_ref, bias_ref, *, tq, ts, kb, lam_init):
    h = pl.program_id(0)
    qi = pl.program_id(1)
    b = pl.program_id(2)
    t = k_ref.shape[1]

    @pl.when(b == 0)
    def _():
        u = lax.broadcasted_iota(jnp.int32, (1, t + tq), 1)
        bucket = _t5_bucket(u - (tq - 1) - qi * tq)
        line = jnp.zeros((1, t + tq), F32)
        for i in range(REL_BUCKETS):
            line = jnp.where(bucket == i, tbl_ref[i * DIFF_HEADS + h], line)
        rows = pltpu.roll(jnp.broadcast_to(line * LOG2E, (tq, t + tq)), 1, 1, stride=1, stride_axis=0)
        bias_ref[...] = rows[:, tq:]

    lq = lq_ref[...]
    lam = (jnp.exp(jnp.sum(lq[0:1] * lq[1:2], axis=-1, keepdims=True))
           - jnp.exp(jnp.sum(lq[2:3] * lq[3:4], axis=-1, keepdims=True)) + lam_init)
    hw = q_ref.shape[-1]
    first = lax.broadcasted_iota(jnp.int32, (ts, hw), 1) < DIFF_HEAD_DIM
    nkb = t // kb

    def softmax_v(r0, comp):
        q = q_ref[0, r0:r0 + ts, :]
        qc = jnp.where(first, q, jnp.zeros_like(q)) if comp == 0 else jnp.where(first, jnp.zeros_like(q), q)
        s = []
        mx = None
        for j in range(nkb):
            sj = _dot_nt(qc, k_ref[0, j * kb:(j + 1) * kb, :]) + bias_ref[r0:r0 + ts, j * kb:(j + 1) * kb]
            mj = jnp.max(sj, axis=-1, keepdims=True)
            mx = mj if mx is None else jnp.maximum(mx, mj)
            s.append(sj)
            yield
        pv = None
        for j in range(nkb):
            vj = v_ref[0, j * kb:(j + 1) * kb, :]
            dj = _dot(jnp.exp2(s[j] - mx), jnp.concatenate([vj, jnp.ones_like(vj)], axis=1))
            pv = dj if pv is None else pv + dj
            yield
        return pv[:, :hw] / pv[:, hw:]

    starts = range(0, tq, ts)
    maps = _run_staggered([softmax_v(r0, comp) for r0 in starts for comp in range(2)], nkb)
    for i, r0 in enumerate(starts):
        o = maps[2 * i] - lam * maps[2 * i + 1]
        o = o * lax.rsqrt(jnp.mean(o * o, axis=-1, keepdims=True) + NORM_EPS) * sg_ref[...] * (1.0 - lam_init)
        o_ref[0, r0:r0 + ts, :] = o.astype(BF16)


def _diff_attention(q, k, v, tbl, lq, sg, lam_init, tq=1024, ts=256, kb=256):
    bsz, t, _ = q.shape
    hw = 2 * DIFF_HEAD_DIM
    return pl.pallas_call(
        functools.partial(_diffattn_kernel, tq=tq, ts=ts, kb=kb, lam_init=lam_init),
        grid=(DIFF_HEADS, t // tq, bsz),
        in_specs=[pl.BlockSpec(memory_space=pltpu.SMEM), _resident(lq.shape), _resident(sg.shape),
                  pl.BlockSpec((1, tq, hw), lambda h, i, b: (b, i, h)),
                  pl.BlockSpec((1, t, hw), lambda h, i, b: (b, 0, h)),
                  pl.BlockSpec((1, t, hw), lambda h, i, b: (b, 0, h))],
        out_specs=pl.BlockSpec((1, tq, hw), lambda h, i, b: (b, i, h)),
        out_shape=jax.ShapeDtypeStruct(q.shape, BF16),
        scratch_shapes=[pltpu.VMEM((tq, t), F32)],
        compiler_params=_cparams("parallel", "parallel", "arbitrary"),
        name="l0_diffattn",
    )(tbl, lq, sg, q, k, v)


def _norm_linear_kernel(x_ref, g_ref, w_ref, o_ref):
    o_ref[...] = _dot(_rms(x_ref[...], g_ref[...]), w_ref[...]).astype(o_ref.dtype)


def _norm_linear(x, g, w, out_dtype, tm=512):
    n, d = x.shape
    m = w.shape[1]
    return pl.pallas_call(
        _norm_linear_kernel,
        grid=(n // tm,),
        in_specs=[pl.BlockSpec((tm, d), lambda i: (i, 0)), _resident((1, d)), _resident(w.shape)],
        out_specs=pl.BlockSpec((tm, m), lambda i: (i, 0)),
        out_shape=jax.ShapeDtypeStruct((n, m), out_dtype),
        compiler_params=_cparams("parallel"),
        name="norm_linear",
    )(x, g, w)


def _xattn_rows(h, g_ref, wq_ref, kv_ref, wo_ref):
    d = h.shape[-1]
    hd = d // XATTN_HEADS
    q = (_dot(_rms(h, g_ref[...]), wq_ref[...]) * (hd ** -0.5 * LOG2E)).astype(BF16)
    yield
    outs = []
    for i in range(XATTN_HEADS):
        kh = kv_ref[0, :, i * hd:(i + 1) * hd]
        vh = kv_ref[0, :, d + i * hd:d + (i + 1) * hd]
        s = _dot_nt(q[:, i * hd:(i + 1) * hd], kh)
        e = jnp.exp2(s - jnp.max(s, axis=-1, keepdims=True))
        p = e / jnp.sum(e, axis=-1, keepdims=True)
        outs.append(_dot(p, vh).astype(BF16))
    yield
    return h + _dot(jnp.concatenate(outs, axis=1), wo_ref[...])


def _l0_tail_kernel(u_ref, o_ref, h_ref, wout_ref, gx_ref, wq_ref, kv_ref, wo_ref, out_ref, *, ts):
    tq = h_ref.shape[1]
    cc = u_ref.shape[-1]

    def sub(r0):
        rs = slice(r0, r0 + ts)
        h1 = h_ref[0, rs, :] + _dot(u_ref[0, rs, :], wout_ref[:cc, :]) + _dot(o_ref[0, rs, :], wout_ref[cc:, :])
        yield
        out_ref[0, rs, :] = yield from _xattn_rows(h1, gx_ref, wq_ref, kv_ref, wo_ref)

    _run_staggered([sub(r0) for r0 in range(0, tq, ts)], 1)


def _l0_tail(u, o, h, wout, gx, wq, kv, wo, tq=1024, ts=256):
    bsz, t, d = h.shape
    tile = lambda w: pl.BlockSpec((1, tq, w), lambda b, i: (b, i, 0))
    return pl.pallas_call(
        functools.partial(_l0_tail_kernel, ts=ts),
        grid=(bsz, t // tq),
        in_specs=[tile(u.shape[-1]), tile(o.shape[-1]), tile(d), _resident(wout.shape), _resident((1, d)),
                  _resident(wq.shape), pl.BlockSpec((1, kv.shape[1], 2 * d), lambda b, i: (b, 0, 0)),
                  _resident(wo.shape)],
        out_specs=tile(d),
        out_shape=jax.ShapeDtypeStruct(h.shape, F32),
        compiler_params=_cparams("parallel", "parallel"),
        name="l0_tail",
    )(u, o, h, wout, gx, wq, kv, wo)


def _mlp_kernel(h_ref, g_ref, wu_ref, wd_ref, gf_ref, o_ref, *, hc, final_norm):
    h = h_ref[...]
    xn = _rms(h, g_ref[...]).astype(BF16)
    acc = h
    for c in range(wu_ref.shape[1] // hc):
        a = jnp.maximum(_dot(xn, wu_ref[:, c * hc:(c + 1) * hc]), 0.0)
        acc = acc + _dot(a * a, wd_ref[c * hc:(c + 1) * hc, :])
    if final_norm:
        acc = _rms(acc, gf_ref[...])
    o_ref[...] = acc


def _mlp(h, g, wu, wd, gf, final_norm, tm=512, hc=1024):
    n, d = h.shape
    row = lambda i: (i, 0)
    return pl.pallas_call(
        functools.partial(_mlp_kernel, hc=hc, final_norm=final_norm),
        grid=(n // tm,),
        in_specs=[pl.BlockSpec((tm, d), row), _resident((1, d)), _resident(wu.shape), _resident(wd.shape),
                  _resident((1, d))],
        out_specs=pl.BlockSpec((tm, d), row),
        out_shape=jax.ShapeDtypeStruct((n, d), F32),
        compiler_params=_cparams("parallel"),
        name="mlp",
    )(h, g, wu, wd, gf)


def _rwkv_prep_kernel(h_ref, hp_ref, hn_ref, g_ref, mu_ref, wr_ref, wk_ref, wv_ref, w1_ref, w2_ref, w0_ref,
                      a1_ref, a2_ref, a0_ref, g1_ref, g2_ref, kk_ref, ka_ref, rk_ref, sel_ref, selt_ref,
                      r_out, v_out, kn_out, gate_out, bonus_out, kd_out, cum_out, b_out, *, ts):
    i = pl.program_id(1)
    last = pl.num_programs(1) - 1
    tm = h_ref.shape[1]
    g = g_ref[...]
    ti = lax.broadcasted_iota(jnp.int32, (ts, ts), 0)
    si = lax.broadcasted_iota(jnp.int32, (ts, ts), 1)
    same_chunk = (ti >> CHUNK_SHIFT) == (si >> CHUNK_SHIFT)
    before = (jnp.where(same_chunk, jnp.where(si <= ti, 1.0, 0.0), 0.0).astype(BF16),
              jnp.where(same_chunk, jnp.where(si >= ti, 1.0, 0.0), 0.0).astype(BF16))

    def rows(r0):
        x = _rms(h_ref[0, r0:r0 + ts, :], g)
        if r0 == 0:
            prev_row = _rms(hp_ref[0], g)[SUBLANES - 1:, :] * jnp.where(i > 0, 1.0, 0.0)
        else:
            prev_row = _rms(h_ref[0, r0 - SUBLANES:r0, :], g)[SUBLANES - 1:, :]
        if r0 + ts == tm:
            next_row = _rms(hn_ref[0], g)[0:1, :] * jnp.where(i < last, 1.0, 0.0)
        else:
            next_row = _rms(h_ref[0, r0 + ts:r0 + ts + SUBLANES, :], g)[0:1, :]
        rowid = lax.broadcasted_iota(jnp.int32, x.shape, 0)
        x_prev = jnp.where(rowid == 0, prev_row, pltpu.roll(x, 1, 0))
        x_next = jnp.where(rowid == ts - 1, next_row, pltpu.roll(x, ts - 1, 0))
        hh = 0.5 * (x_prev + x_next) - x
        mix = lambda j: x + hh * mu_ref[j:j + 1, :]
        r = _dot(mix(0), wr_ref[...])
        k = _dot(mix(2), wk_ref[...])
        v = _dot(mix(3), wv_ref[...])
        yield
        gate_in = _dot(mix(5), g1_ref[...])
        lw = _dot(mix(1), w1_ref[...])
        la = _dot(mix(4), a1_ref[...])
        yield
        gate = _dot(_sigmoid(gate_in), g2_ref[...])
        lw = jnp.tanh(lw)
        w_pre = [w0_ref[z:z + 1, :] + _dot(lw, w2_ref[z]) for z in range(2)]
        a_pre = [a0_ref[z:z + 1, :] + _dot(la, a2_ref[z]) for z in range(2)]
        kk = k * kk_ref[...]
        ss = _dot(kk * kk, sel_ref[...])
        yield
        kn = kk * lax.rsqrt(jnp.maximum(_dot(ss, selt_ref[...]), 1e-24))
        kka = k * ka_ref[...]
        kd_sum = jnp.zeros_like(k)
        cum = []
        for z in range(2):
            cum.append(_split_dot(before[z], _sigmoid(w_pre[z]) * (-math.exp(-0.5))))
            rate = _sigmoid(a_pre[z])
            kd = k + kka * (rate - 1.0)
            kd_out[z, 0, r0:r0 + ts, :] = kd.astype(kd_out.dtype)
            b_out[z, 0, r0:r0 + ts, :] = (kn * rate).astype(b_out.dtype)
            kd_sum = kd_sum + kd
        bs = _dot(r * kd_sum * rk_ref[...], sel_ref[...])
        yield
        r_out[0, r0:r0 + ts, :] = r.astype(r_out.dtype)
        v_out[0, r0:r0 + ts, :] = v.astype(v_out.dtype)
        kn_out[0, r0:r0 + ts, :] = kn.astype(kn_out.dtype)
        gate_out[0, r0:r0 + ts, :] = gate.astype(gate_out.dtype)
        bonus_out[0, r0:r0 + ts, :] = (_dot(bs, selt_ref[...]) * v).astype(bonus_out.dtype)
        for z in range(2):
            cum_out[z, 0, r0:r0 + ts, :] = cum[z]

    _run_staggered([rows(r0) for r0 in range(0, tm, ts)], 1)


def _rwkv_prep(h, g, mu, wr, wk, wv, w1, w2, w0, a1, a2, a0, g1, g2, kk, ka, rk, sel, selt, tm=256, ts=128):
    bsz, t, d = h.shape
    nb = tm // SUBLANES
    tile = pl.BlockSpec((1, tm, d), lambda b, i: (b, i, 0))
    tile2 = pl.BlockSpec((2, 1, tm, d), lambda b, i: (0, b, i, 0))
    one = jax.ShapeDtypeStruct((bsz, t, d), BF16)
    two = jax.ShapeDtypeStruct((2, bsz, t, d), BF16)
    consts = [g, mu, wr, wk, wv, w1, w2, w0, a1, a2, a0, g1, g2, kk, ka, rk, sel, selt]
    return pl.pallas_call(
        functools.partial(_rwkv_prep_kernel, ts=ts),
        grid=(bsz, t // tm),
        in_specs=[tile,
                  pl.BlockSpec((1, SUBLANES, d), lambda b, i: (b, jnp.maximum(i * nb - 1, 0), 0)),
                  pl.BlockSpec((1, SUBLANES, d),
                               lambda b, i: (b, jnp.minimum((i + 1) * nb, t // SUBLANES - 1), 0))]
                 + [_resident(c.shape) for c in consts],
        out_specs=[tile, tile, tile, tile, tile, tile2, tile2, tile2],
        out_shape=[one, one, one, one, one, two, jax.ShapeDtypeStruct((2, bsz, t, d), F32), two],
        compiler_params=_cparams("parallel", "parallel"),
        name="l1_rwkv_prep",
    )(h, h, h, *consts)


def _blockdiag(x):
    lane = lax.broadcasted_iota(jnp.int32, x.shape, 1)
    head0 = (lane & (PAIR - 1)) < RWKV_HEAD_DIM
    zero = jnp.zeros_like(x)
    return jnp.concatenate([jnp.where(head0, x, zero), jnp.where(head0, zero, x)], axis=0)


def _chunk_local(r, k, v, kn, cum, b, rev):
    c = CHUNK
    bd = _blockdiag
    row = lax.broadcasted_iota(jnp.int32, (c, PAIR), 0)
    lane = lax.broadcasted_iota(jnp.int32, (c, PAIR), 1)
    s_idx = lane & (c - 1)
    if rev:
        cum_prev = jnp.where(row == c - 1, 0.0, pltpu.roll(cum, c - 1, 0))
        tot = cum[0:1, :]
        strict = s_idx > row
        incl = s_idx >= row
    else:
        cum_prev = jnp.where(row == 0, 0.0, pltpu.roll(cum, 1, 0))
        tot = cum[c - 1:c, :]
        strict = s_idx < row
        incl = s_idx <= row
    w_incl = jnp.exp(cum)
    w_excl = jnp.exp(cum_prev)
    w_inv = jnp.exp(-cum)
    w_tot = jnp.exp(tot)
    w_rest = jnp.exp(tot - cum)
    a_t = -kn * w_excl
    r_t = r * w_incl
    b_t = b * w_inv
    k_t = k * w_inv
    b_h = b * w_rest
    k_h = k * w_rest
    same_blk = (s_idx >> 4) == (row >> 4)

    sc = _dot_nt(jnp.concatenate([a_t, r_t], axis=0), jnp.concatenate([bd(b_t), bd(k_t)], axis=0))
    yield
    p_ab = jnp.where(strict, sc[:c, :PAIR], 0.0)
    p_ak = jnp.where(strict, sc[:c, PAIR:], 0.0)
    p_rb = jnp.where(incl, sc[c:, :PAIR], 0.0)
    p_rk = jnp.where(incl, sc[c:, PAIR:], 0.0)
    dm = jnp.where(same_blk, p_ab, 0.0)
    em = p_ab - dm
    x2 = _dot(dm, bd(dm))
    av = _dot(p_ak, bd(v))
    yield
    td = jnp.where(s_idx == row, 1.0, 0.0) + dm
    both = _dot(jnp.concatenate([x2, td], axis=0), bd(x2))
    yield
    x4 = both[:c]
    td = td + both[c:]
    both = _dot(jnp.concatenate([x4, td], axis=0), bd(x4))
    yield
    td = td + both[c:]
    td = td + _dot(td, bd(both[:c]))
    yield
    ty = _dot(td, bd(jnp.concatenate([a_t, av, em], axis=1)))
    yield
    au, f1 = ty[:, :2 * PAIR], ty[:, 2 * PAIR:]
    both = _dot(f1, bd(jnp.concatenate([f1, au], axis=1)))
    f2 = both[:, :PAIR]
    au = au + both[:, PAIR:]
    yield
    au = au + _dot(f2, bd(au))
    yield
    rhs = jnp.concatenate([bd(au), jnp.concatenate([jnp.zeros((PAIR, PAIR), F32), bd(v)], axis=1)], axis=0)
    ry = _dot(jnp.concatenate([p_rb, p_rk], axis=1), rhs)
    rhs2 = jnp.concatenate([au, jnp.concatenate([jnp.zeros((c, PAIR), F32), v], axis=1)], axis=0)
    mg = _dot(jnp.concatenate([b_h, k_h], axis=0).T, rhs2)
    yield
    rbar = r_t + ry[:, :PAIR]
    yloc = ry[:, PAIR:]
    r2 = lax.broadcasted_iota(jnp.int32, (PAIR, PAIR), 0)
    l2 = lax.broadcasted_iota(jnp.int32, (PAIR, PAIR), 1)
    same_head = (r2 >> 6) == (l2 >> 6)
    m = jnp.where(same_head, mg[:, :PAIR], 0.0) + jnp.where(r2 == l2, w_tot, 0.0)
    gg = jnp.where(same_head, mg[:, PAIR:], 0.0)
    return rbar, yloc, m, gg


def _scan_kernel(r_ref, v_ref, kn_ref, kd_ref, cum_ref, b_ref, yf_ref, yb_ref, ds_ref, loc_a, loc_b, *, unroll):
    nc = r_ref.shape[1] // CHUNK
    groups = nc // unroll
    total = (r_ref.shape[2] // PAIR) * groups
    y_refs = (yf_ref, yb_ref)
    c = CHUNK
    ds_ref[...] = jnp.zeros(ds_ref.shape, F32)
    loc_b[...] = jnp.zeros(loc_b.shape, F32)

    def places(gidx):
        grp = gidx % groups
        lanes = pl.ds(pl.multiple_of((gidx // groups) * PAIR, PAIR), PAIR)
        out = []
        for u in range(unroll):
            for z in range(2):
                cidx = grp * unroll + u
                if z == 1:
                    cidx = nc - 1 - cidx
                out.append((z, pl.ds(pl.multiple_of(cidx * c, c), c), lanes))
        return out

    def local_terms(gidx, loc_ref):
        f32 = lambda ref, *idx: ref[idx].astype(F32)
        gens = [_chunk_local(f32(r_ref, 0, sl, ln), f32(kd_ref, z, 0, sl, ln), f32(v_ref, 0, sl, ln),
                             f32(kn_ref, 0, sl, ln), cum_ref[z, 0, sl, ln], f32(b_ref, z, 0, sl, ln),
                             rev=(z == 1))
                for z, sl, ln in places(gidx)]

        def park(j, gen):
            rbar, yloc, m, gg = yield from gen
            loc_ref[j, 0:c, :] = rbar
            loc_ref[j, c:2 * c, :] = yloc
            loc_ref[j, 2 * c:2 * c + PAIR, :] = m
            loc_ref[j, 2 * c + PAIR:, :] = gg

        return [park(j, gen) for j, gen in enumerate(gens)]

    def recurrence(gidx, loc_ref):
        keep = jnp.where(gidx % groups == 0, 0.0, 1.0)
        ds = [ds_ref[0] * keep, ds_ref[1] * keep]
        for j, (z, sl, ln) in enumerate(places(gidx)):
            both = _dot(jnp.concatenate([loc_ref[j, 0:c, :], loc_ref[j, 2 * c:2 * c + PAIR, :]], axis=0), ds[z])
            y_refs[z][0, sl, ln] = (both[:c] + loc_ref[j, c:2 * c, :]).astype(y_refs[z].dtype)
            ds[z] = both[c:] + loc_ref[j, 2 * c + PAIR:, :]
            if z == 1:
                yield
        ds_ref[0] = ds[0]
        ds_ref[1] = ds[1]

    def body(it, carry):
        first = 2 * it
        _run_staggered(local_terms(first, loc_a) + [recurrence(jnp.maximum(first - 1, 0), loc_b)], 0)
        _run_staggered(local_terms(first + 1, loc_b) + [recurrence(first, loc_a)], 0)
        return carry

    lax.fori_loop(0, total // 2, body, 0)
    _run_staggered([recurrence(total - 1, loc_b)], 0)


def _wkv7_scan(r, v, kn, kd, cum, b, unroll=8, pairs=2):
    bsz, t, d = r.shape
    nc = t // CHUNK
    unroll = min(unroll, nc // 2)
    assert nc % (2 * unroll) == 0, "the scan kernel takes chunk groups in pairs"
    one = pl.BlockSpec((1, t, pairs * PAIR), lambda bb, p: (bb, 0, p))
    two = pl.BlockSpec((2, 1, t, pairs * PAIR), lambda bb, p: (0, bb, 0, p))
    out = jax.ShapeDtypeStruct((bsz, t, d), BF16)
    return pl.pallas_call(
        functools.partial(_scan_kernel, unroll=unroll),
        grid=(bsz, d // (pairs * PAIR)),
        in_specs=[one, one, one, two, two, two],
        out_specs=[one, one],
        out_shape=[out, out],
        scratch_shapes=[pltpu.VMEM((2, PAIR, PAIR), F32)]
                       + [pltpu.VMEM((2 * unroll, 2 * CHUNK + 2 * PAIR, PAIR), F32)] * 2,
        compiler_params=_cparams("parallel", "parallel"),
        name="l1_wkv7_scan",
    )(r, v, kn, kd, cum, b)


def _l1_tail_kernel(yf_ref, yb_ref, bonus_ref, gate_ref, h_ref, lg_ref, lb_ref, wo_ref, sel_ref, selt_ref,
                    gx_ref, wq_ref, kv_ref, wxo_ref, out_ref, *, ts):
    tq = h_ref.shape[1]
    inv_n = 1.0 / RWKV_HEAD_DIM

    def sub(r0):
        rs = slice(r0, r0 + ts)
        y = yf_ref[0, rs, :] + yb_ref[0, rs, :]
        mu = _dot(y, sel_ref[...])
        yield
        yc = y - _dot(mu, selt_ref[...]) * inv_n
        var = _dot(yc * yc, sel_ref[...])
        yield
        yn = yc * lax.rsqrt(_dot(var, selt_ref[...]) * inv_n + GN_EPS) * lg_ref[...] + lb_ref[...]
        h1 = h_ref[0, rs, :] + _dot((yn + bonus_ref[0, rs, :]) * gate_ref[0, rs, :], wo_ref[...])
        yield
        out_ref[0, rs, :] = yield from _xattn_rows(h1, gx_ref, wq_ref, kv_ref, wxo_ref)

    _run_staggered([sub(r0) for r0 in range(0, tq, ts)], 2)


def _l1_tail(yf, yb, bonus, gate, h, lg, lb, wo, sel, selt, gx, wq, kv, wxo, tq=1024, ts=256):
    bsz, t, d = h.shape
    tile = pl.BlockSpec((1, tq, d), lambda b, i: (b, i, 0))
    return pl.pallas_call(
        functools.partial(_l1_tail_kernel, ts=ts),
        grid=(bsz, t // tq),
        in_specs=[tile, tile, tile, tile, tile, _resident((1, d)), _resident((1, d)), _resident(wo.shape),
                  _resident(sel.shape), _resident(selt.shape), _resident((1, d)), _resident(wq.shape),
                  pl.BlockSpec((1, kv.shape[1], 2 * d), lambda b, i: (b, 0, 0)), _resident(wxo.shape)],
        out_specs=tile,
        out_shape=jax.ShapeDtypeStruct(h.shape, F32),
        compiler_params=_cparams("parallel", "parallel"),
        name="l1_tail",
    )(yf, yb, bonus, gate, h, lg, lb, wo, sel, selt, gx, wq, kv, wxo)


def _pad_lora_out(w2):
    zero = jnp.zeros_like(w2[0])
    return jnp.stack([jnp.concatenate([w2[0], zero], axis=0), jnp.concatenate([zero, w2[1]], axis=0)])


def kernel(x, mem, rel_bias_table, norm_mix, norm_xattn, norm_mem, norm_ffn, norm_final, ab_w_in, ab_w_out, conv_w, conv_b, conv_ln_g, conv_ln_b, diff_lq1, diff_lk1, diff_lq2, diff_lk2, diff_subln_g, rwkv_mu, rwkv_w_r, rwkv_w_k, rwkv_w_v, rwkv_w_o, rwkv_w0, rwkv_w1, rwkv_w2, rwkv_a0, rwkv_a1, rwkv_a2, rwkv_g1, rwkv_g2, rwkv_k_k, rwkv_k_a, rwkv_r_k, rwkv_ln_g, rwkv_ln_b, xattn_w_q, xattn_w_kv, xattn_w_o, ffn_w_up, ffn_w_down):
    bsz, t, d = x.shape
    n = bsz * t
    depth = norm_mix.shape[0]
    n_mem = mem.shape[1]
    cc = conv_w.shape[-1]
    qk = DIFF_HEADS * 2 * DIFF_HEAD_DIM
    vw = ab_w_in.shape[-1] - 2 * cc - 2 * qk
    bf = lambda w: w.astype(BF16)
    row = lambda w: w.reshape(1, -1)

    heads = d // RWKV_HEAD_DIM
    head_of = jnp.arange(d, dtype=jnp.int32) // RWKV_HEAD_DIM
    sel = (head_of[:, None] == jnp.arange(128, dtype=jnp.int32)[None, :]).astype(BF16)
    selt = sel.T
    assert heads <= 128

    h = x.reshape(n, d)
    mem2 = mem.reshape(bsz * n_mem, d)
    for i in range(depth):
        j = i // 2
        kv = _norm_linear(mem2, row(norm_mem[i]), bf(xattn_w_kv[i]), BF16).reshape(bsz, n_mem, 2 * d)
        if i % 2 == 0:
            lam_init = 0.8 - 0.6 * math.exp(-0.3 * i)
            h3 = h.reshape(bsz, t, d)
            u, q, k, v = _inproj(h3, row(norm_mix[i]), bf(ab_w_in[j]), conv_w[j], row(conv_b[j]),
                                 row(conv_ln_g[j]), row(conv_ln_b[j]), cc, qk, vw)
            lq = jnp.stack([diff_lq1[j], diff_lk1[j], diff_lq2[j], diff_lk2[j]])
            o = _diff_attention(q, k, v, rel_bias_table.reshape(-1), lq, row(diff_subln_g[j]), lam_init)
            h = _l0_tail(u, o, h3, bf(ab_w_out[j]), row(norm_xattn[i]), bf(xattn_w_q[i]), kv,
                         bf(xattn_w_o[i])).reshape(n, d)
        else:
            h3 = h.reshape(bsz, t, d)
            w1 = bf(jnp.concatenate([rwkv_w1[j, 0], rwkv_w1[j, 1]], axis=1))
            a1 = bf(jnp.concatenate([rwkv_a1[j, 0], rwkv_a1[j, 1]], axis=1))
            r, v, kn, gate, bonus, kd, cum, b = _rwkv_prep(
                h3, row(norm_mix[i]), rwkv_mu[j], bf(rwkv_w_r[j]), bf(rwkv_w_k[j]), bf(rwkv_w_v[j]),
                w1, bf(_pad_lora_out(rwkv_w2[j])), rwkv_w0[j], a1, bf(_pad_lora_out(rwkv_a2[j])), rwkv_a0[j],
                bf(rwkv_g1[j]), bf(rwkv_g2[j]), row(rwkv_k_k[j]), row(rwkv_k_a[j]), row(rwkv_r_k[j]),
                sel, selt)
            yf, yb = _wkv7_scan(r, v, kn, kd, cum, b)
            h = _l1_tail(yf, yb, bonus, gate, h3, row(rwkv_ln_g[j]), row(rwkv_ln_b[j]), bf(rwkv_w_o[j]), sel, selt,
                         row(norm_xattn[i]), bf(xattn_w_q[i]), kv, bf(xattn_w_o[i])).reshape(n, d)
        h = _mlp(h, row(norm_ffn[i]), bf(ffn_w_up[i]), bf(ffn_w_down[i]), row(norm_final),
                 final_norm=(i == depth - 1))
    return h.reshape(bsz, t, d)
```

```python
import functools
import math

import jax
import jax.numpy as jnp
from jax import lax
from jax.experimental import pallas as pl
from jax.experimental.pallas import tpu as pltpu

F32 = jnp.float32
BF16 = jnp.bfloat16

V7X_VMEM_BYTES = 64 * 1024 * 1024
VMEM_LIMIT_BYTES = V7X_VMEM_BYTES - 8 * 1024 * 1024

LOG2E = math.log2(math.e)
NORM_EPS = 1e-6
CONV_LN_EPS = 1e-5
GN_EPS = 64e-5
CONV_WIDTH = 31
CONV_PAD = CONV_WIDTH // 2
CONV_HALO = 16
DIFF_HEADS = 4
DIFF_HEAD_DIM = 64
REL_BUCKETS = 32
REL_MAX_DIST = 128
XATTN_HEADS = 4
RWKV_HEAD_DIM = 64
SUBLANES = 8
LANES = 128
CHUNK = 64
CHUNK_SHIFT = CHUNK.bit_length() - 1
HEAD_SHIFT = RWKV_HEAD_DIM.bit_length() - 1
DIAG_SHIFT = 4
PAIR = 2 * RWKV_HEAD_DIM


def _cparams(*sem):
    return pltpu.CompilerParams(dimension_semantics=sem, vmem_limit_bytes=VMEM_LIMIT_BYTES)


def _resident(shape):
    nd = len(shape)
    return pl.BlockSpec(shape, lambda *_: (0,) * nd, pipeline_mode=pl.Buffered(1))


def _rms(x, g):
    ms = jnp.mean(x * x, axis=-1, keepdims=True)
    return x * lax.rsqrt(ms + NORM_EPS) * g


def _sigmoid(x):
    return 1.0 / (1.0 + jnp.exp(-x))


def _dot(a, b):
    return jnp.dot(a.astype(BF16), b.astype(BF16), preferred_element_type=F32)


def _dot_nt(a, b):
    return lax.dot_general(a.astype(BF16), b.astype(BF16), (((1,), (1,)), ((), ())),
                           preferred_element_type=F32)


def _split_dot(mat, x):
    hi = x.astype(BF16)
    lo = (x - hi.astype(F32)).astype(BF16)
    return (jnp.dot(mat, hi, preferred_element_type=F32) + jnp.dot(mat, lo, preferred_element_type=F32))


def _run_staggered(gens, offset):
    results = [None] * len(gens)
    running = [True] * len(gens)
    rnd = 0
    while any(running):
        for i, gen in enumerate(gens):
            if rnd >= i * offset and running[i]:
                try:
                    next(gen)
                except StopIteration as stop:
                    results[i] = stop.value
                    running[i] = False
        rnd += 1
    return results


def _inproj_kernel(h_ref, hp_ref, hn_ref, g_ref, w_ref, cw_ref, cb_ref, cg_ref, cbeta_ref,
                   u_ref, q_ref, k_ref, v_ref, xp_ref, *, cc, qk, scale, rows):
    i = pl.program_id(1)
    last = pl.num_programs(1) - 1
    tm = h_ref.shape[1]
    g = g_ref[...]
    x = _rms(h_ref[0], g).astype(BF16)
    x_ext = jnp.concatenate([_rms(hp_ref[0], g).astype(BF16), x, _rms(hn_ref[0], g).astype(BF16)], axis=0)
    pu = _dot(x_ext, w_ref[:, :2 * cc])
    u = pu[:, :cc] * _sigmoid(pu[:, cc:])
    xp_ref[0:CONV_HALO, :] = u[:CONV_HALO] * jnp.where(i > 0, 1.0, 0.0)
    xp_ref[CONV_HALO:CONV_HALO + tm, :] = u[CONV_HALO:CONV_HALO + tm]
    xp_ref[CONV_HALO + tm:, :] = u[CONV_HALO + tm:] * jnp.where(i < last, 1.0, 0.0)

    def conv():
        for t0 in range(0, tm, rows):
            u_ref[0, t0:t0 + rows, :] = _conv_rows(xp_ref, t0, rows, cw_ref, cb_ref, cg_ref, cbeta_ref)
            yield

    def qkv():
        o = 2 * cc
        q_ref[0] = (_dot(x, w_ref[:, o:o + qk]) * scale).astype(BF16)
        yield
        k_ref[0] = _dot(x, w_ref[:, o + qk:o + 2 * qk]).astype(BF16)
        yield
        v_ref[0] = _dot(x, w_ref[:, o + 2 * qk:]).astype(BF16)

    _run_staggered([conv(), qkv()], 0)


def _inproj(h, g, w, cw, cb, cbg, cbeta, cc, qk, vw, tm=512, rows=64):
    bsz, t, d = h.shape
    nb = tm // CONV_HALO
    tile = lambda width: pl.BlockSpec((1, tm, width), lambda b, i: (b, i, 0))
    out = lambda width: jax.ShapeDtypeStruct((bsz, t, width), BF16)
    return pl.pallas_call(
        functools.partial(_inproj_kernel, cc=cc, qk=qk, scale=DIFF_HEAD_DIM ** -0.5 * LOG2E, rows=rows),
        grid=(bsz, t // tm),
        in_specs=[tile(d),
                  pl.BlockSpec((1, CONV_HALO, d), lambda b, i: (b, jnp.maximum(i * nb - 1, 0), 0)),
                  pl.BlockSpec((1, CONV_HALO, d),
                               lambda b, i: (b, jnp.minimum((i + 1) * nb, t // CONV_HALO - 1), 0)),
                  _resident((1, d)), _resident(w.shape), _resident(cw.shape), _resident((1, cc)),
                  _resident((1, cc)), _resident((1, cc))],
        out_specs=[tile(cc), tile(qk), tile(qk), tile(vw)],
        out_shape=[out(cc), out(qk), out(qk), out(vw)],
        scratch_shapes=[pltpu.VMEM((tm + 2 * CONV_HALO, cc), F32)],
        compiler_params=_cparams("parallel", "parallel"),
        name="l0_inproj",
    )(h, h, h, g, w, cw, cb, cbg, cbeta)


def _conv_rows(xp_ref, base, rows, w_ref, b_ref, g_ref, beta_ref):
    c = xp_ref.shape[-1]
    off = CONV_HALO - CONV_PAD
    span = rows + 2 * CONV_HALO
    pieces = []
    for c0 in range(0, c, LANES):
        win = xp_ref[base:base + span, c0:c0 + LANES]
        acc = jnp.zeros((rows, LANES), F32)
        for rem in range(SUBLANES):
            shifted = win if rem == 0 else pltpu.roll(win, span - rem, 0)
            for start in range(0, 2 * CONV_HALO, SUBLANES):
                k = start + rem - off
                if 0 <= k < CONV_WIDTH:
                    acc = acc + shifted[start:start + rows, :] * w_ref[k:k + 1, c0:c0 + LANES]
        pieces.append(acc)
    y = jnp.concatenate(pieces, axis=1) + b_ref[...]
    mu = jnp.mean(y, axis=-1, keepdims=True)
    yc = y - mu
    var = jnp.mean(yc * yc, axis=-1, keepdims=True)
    yn = yc * lax.rsqrt(var + CONV_LN_EPS) * g_ref[...] + beta_ref[...]
    return (yn * _sigmoid(yn)).astype(BF16)


def _t5_bucket(rel):
    nb = REL_BUCKETS // 2
    max_exact = nb // 2
    n = jnp.abs(rel)
    large = jnp.full(rel.shape, max_exact, jnp.int32)
    steps = nb - max_exact
    for m in range(1, steps):
        thr = math.ceil(max_exact * (REL_MAX_DIST / max_exact) ** (m / steps) - 1e-9)
        large = large + jnp.where(n >= thr, 1, 0)
    mag = jnp.where(n < max_exact, n, large)
    return mag + jnp.where(rel > 0, nb, 0)


def _diffattn_kernel(tbl_ref, lq_ref, sg_ref, q_ref, k_ref, v_ref, o_ref, bias_ref, *, tq, ts, kb, lam_init):
    h = pl.program_id(0)
    qi = pl.program_id(1)
    b = pl.program_id(2)
    t = k_ref.shape[1]

    @pl.when(b == 0)
    def _():
        u = lax.broadcasted_iota(jnp.int32, (1, t + tq), 1)
        bucket = _t5_bucket(u - (tq - 1) - qi * tq)
        line = jnp.zeros((1, t + tq), F32)
        for i in range(REL_BUCKETS):
            line = jnp.where(bucket == i, tbl_ref[i * DIFF_HEADS + h], line)
        rows = pltpu.roll(jnp.broadcast_to(line * LOG2E, (tq, t + tq)), 1, 1, stride=1, stride_axis=0)
        bias_ref[...] = rows[:, tq:]

    lq = lq_ref[...]
    lam = (jnp.exp(jnp.sum(lq[0:1] * lq[1:2], axis=-1, keepdims=True))
           - jnp.exp(jnp.sum(lq[2:3] * lq[3:4], axis=-1, keepdims=True)) + lam_init)
    hw = q_ref.shape[-1]
    first = lax.broadcasted_iota(jnp.int32, (ts, hw), 1) < DIFF_HEAD_DIM
    nkb = t // kb

    def softmax_v(r0, comp):
        q = q_ref[0, r0:r0 + ts, :]
        qc = jnp.where(first, q, jnp.zeros_like(q)) if comp == 0 else jnp.where(first, jnp.zeros_like(q), q)
        s = []
        mx = None
        for j in range(nkb):
            sj = _dot_nt(qc, k_ref[0, j * kb:(j + 1) * kb, :]) + bias_ref[r0:r0 + ts, j * kb:(j + 1) * kb]
            mj = jnp.max(sj, axis=-1, keepdims=True)
            mx = mj if mx is None else jnp.maximum(mx, mj)
            s.append(sj)
            yield
        pv = None
        for j in range(nkb):
            vj = v_ref[0, j * kb:(j + 1) * kb, :]
            dj = _dot(jnp.exp2(s[j] - mx), jnp.concatenate([vj, jnp.ones_like(vj)], axis=1))
            pv = dj if pv is None else pv + dj
            yield
        return pv[:, :hw] / pv[:, hw:]

    starts = range(0, tq, ts)
    maps = _run_staggered([softmax_v(r0, comp) for r0 in starts for comp in range(2)], nkb)
    for i, r0 in enumerate(starts):
        o = maps[2 * i] - lam * maps[2 * i + 1]
        o = o * lax.rsqrt(jnp.mean(o * o, axis=-1, keepdims=True) + NORM_EPS) * sg_ref[...] * (1.0 - lam_init)
        o_ref[0, r0:r0 + ts, :] = o.astype(BF16)


def _diff_attention(q, k, v, tbl, lq, sg, lam_init, tq=1024, ts=256, kb=256):
    bsz, t, _ = q.shape
    hw = 2 * DIFF_HEAD_DIM
    return pl.pallas_call(
        functools.partial(_diffattn_kernel, tq=tq, ts=ts, kb=kb, lam_init=lam_init),
        grid=(DIFF_HEADS, t // tq, bsz),
        in_specs=[pl.BlockSpec(memory_space=pltpu.SMEM), _resident(lq.shape), _resident(sg.shape),
                  pl.BlockSpec((1, tq, hw), lambda h, i, b: (b, i, h)),
                  pl.BlockSpec((1, t, hw), lambda h, i, b: (b, 0, h)),
                  pl.BlockSpec((1, t, hw), lambda h, i, b: (b, 0, h))],
        out_specs=pl.BlockSpec((1, tq, hw), lambda h, i, b: (b, i, h)),
        out_shape=jax.ShapeDtypeStruct(q.shape, BF16),
        scratch_shapes=[pltpu.VMEM((tq, t), F32)],
        compiler_params=_cparams("parallel", "parallel", "arbitrary"),
        name="l0_diffattn",
    )(tbl, lq, sg, q, k, v)


def _norm_linear_kernel(x_ref, g_ref, w_ref, o_ref):
    o_ref[...] = _dot(_rms(x_ref[...], g_ref[...]), w_ref[...]).astype(o_ref.dtype)


def _norm_linear(x, g, w, out_dtype, tm=512):
    n, d = x.shape
    m = w.shape[1]
    return pl.pallas_call(
        _norm_linear_kernel,
        grid=(n // tm,),
        in_specs=[pl.BlockSpec((tm, d), lambda i: (i, 0)), _resident((1, d)), _resident(w.shape)],
        out_specs=pl.BlockSpec((tm, m), lambda i: (i, 0)),
        out_shape=jax.ShapeDtypeStruct((n, m), out_dtype),
        compiler_params=_cparams("parallel"),
        name="norm_linear",
    )(x, g, w)


def _xattn_rows(h, g_ref, wq_ref, kv_ref, wo_ref):
    d = h.shape[-1]
    hd = d // XATTN_HEADS
    q = (_dot(_rms(h, g_ref[...]), wq_ref[...]) * (hd ** -0.5 * LOG2E)).astype(BF16)
    yield
    outs = []
    for i in range(XATTN_HEADS):
        kh = kv_ref[0, :, i * hd:(i + 1) * hd]
        vh = kv_ref[0, :, d + i * hd:d + (i + 1) * hd]
        s = _dot_nt(q[:, i * hd:(i + 1) * hd], kh)
        e = jnp.exp2(s - jnp.max(s, axis=-1, keepdims=True))
        p = e / jnp.sum(e, axis=-1, keepdims=True)
        outs.append(_dot(p, vh).astype(BF16))
    yield
    return h + _dot(jnp.concatenate(outs, axis=1), wo_ref[...])


def _l0_tail_kernel(u_ref, o_ref, h_ref, wout_ref, gx_ref, wq_ref, kv_ref, wo_ref, out_ref, *, ts):
    tq = h_ref.shape[1]
    cc = u_ref.shape[-1]

    def sub(r0):
        rs = slice(r0, r0 + ts)
        h1 = h_ref[0, rs, :] + _dot(u_ref[0, rs, :], wout_ref[:cc, :]) + _dot(o_ref[0, rs, :], wout_ref[cc:, :])
        yield
        out_ref[0, rs, :] = yield from _xattn_rows(h1, gx_ref, wq_ref, kv_ref, wo_ref)

    _run_staggered([sub(r0) for r0 in range(0, tq, ts)], 1)


def _l0_tail(u, o, h, wout, gx, wq, kv, wo, tq=1024, ts=256):
    bsz, t, d = h.shape
    tile = lambda w: pl.BlockSpec((1, tq, w), lambda b, i: (b, i, 0))
    return pl.pallas_call(
        functools.partial(_l0_tail_kernel, ts=ts),
        grid=(bsz, t // tq),
        in_specs=[tile(u.shape[-1]), tile(o.shape[-1]), tile(d), _resident(wout.shape), _resident((1, d)),
                  _resident(wq.shape), pl.BlockSpec((1, kv.shape[1], 2 * d), lambda b, i: (b, 0, 0)),
                  _resident(wo.shape)],
        out_specs=tile(d),
        out_shape=jax.ShapeDtypeStruct(h.shape, F32),
        compiler_params=_cparams("parallel", "parallel"),
        name="l0_tail",
    )(u, o, h, wout, gx, wq, kv, wo)


def _mlp_kernel(h_ref, g_ref, wu_ref, wd_ref, gf_ref, o_ref, *, hc, final_norm):
    h = h_ref[...]
    xn = _rms(h, g_ref[...]).astype(BF16)
    acc = h
    for c in range(wu_ref.shape[1] // hc):
        a = jnp.maximum(_dot(xn, wu_ref[:, c * hc:(c + 1) * hc]), 0.0)
        acc = acc + _dot(a * a, wd_ref[c * hc:(c + 1) * hc, :])
    if final_norm:
        acc = _rms(acc, gf_ref[...])
    o_ref[...] = acc


def _mlp(h, g, wu, wd, gf, final_norm, tm=512, hc=1024):
    n, d = h.shape
    row = lambda i: (i, 0)
    return pl.pallas_call(
        functools.partial(_mlp_kernel, hc=hc, final_norm=final_norm),
        grid=(n // tm,),
        in_specs=[pl.BlockSpec((tm, d), row), _resident((1, d)), _resident(wu.shape), _resident(wd.shape),
                  _resident((1, d))],
        out_specs=pl.BlockSpec((tm, d), row),
        out_shape=jax.ShapeDtypeStruct((n, d), F32),
        compiler_params=_cparams("parallel"),
        name="mlp",
    )(h, g, wu, wd, gf)


def _rwkv_prep_kernel(h_ref, hp_ref, hn_ref, g_ref, mu_ref, wr_ref, wk_ref, wv_ref, w1_ref, w2_ref, w0_ref,
                      a1_ref, a2_ref, a0_ref, g1_ref, g2_ref, kk_ref, ka_ref, rk_ref, sel_ref, selt_ref,
                      r_out, v_out, kn_out, gate_out, bonus_out, kd_out, cum_out, b_out, *, ts):
    i = pl.program_id(1)
    last = pl.num_programs(1) - 1
    tm = h_ref.shape[1]
    g = g_ref[...]
    ti = lax.broadcasted_iota(jnp.int32, (ts, ts), 0)
    si = lax.broadcasted_iota(jnp.int32, (ts, ts), 1)
    same_chunk = (ti >> CHUNK_SHIFT) == (si >> CHUNK_SHIFT)
    before = (jnp.where(same_chunk, jnp.where(si <= ti, 1.0, 0.0), 0.0).astype(BF16),
              jnp.where(same_chunk, jnp.where(si >= ti, 1.0, 0.0), 0.0).astype(BF16))

    def rows(r0):
        x = _rms(h_ref[0, r0:r0 + ts, :], g)
        if r0 == 0:
            prev_row = _rms(hp_ref[0], g)[SUBLANES - 1:, :] * jnp.where(i > 0, 1.0, 0.0)
        else:
            prev_row = _rms(h_ref[0, r0 - SUBLANES:r0, :], g)[SUBLANES - 1:, :]
        if r0 + ts == tm:
            next_row = _rms(hn_ref[0], g)[0:1, :] * jnp.where(i < last, 1.0, 0.0)
        else:
            next_row = _rms(h_ref[0, r0 + ts:r0 + ts + SUBLANES, :], g)[0:1, :]
        rowid = lax.broadcasted_iota(jnp.int32, x.shape, 0)
        x_prev = jnp.where(rowid == 0, prev_row, pltpu.roll(x, 1, 0))
        x_next = jnp.where(rowid == ts - 1, next_row, pltpu.roll(x, ts - 1, 0))
        hh = 0.5 * (x_prev + x_next) - x
        mix = lambda j: x + hh * mu_ref[j:j + 1, :]
        r = _dot(mix(0), wr_ref[...])
        k = _dot(mix(2), wk_ref[...])
        v = _dot(mix(3), wv_ref[...])
        yield
        gate_in = _dot(mix(5), g1_ref[...])
        lw = _dot(mix(1), w1_ref[...])
        la = _dot(mix(4), a1_ref[...])
        yield
        gate = _dot(_sigmoid(gate_in), g2_ref[...])
        lw = jnp.tanh(lw)
        w_pre = [w0_ref[z:z + 1, :] + _dot(lw, w2_ref[z]) for z in range(2)]
        a_pre = [a0_ref[z:z + 1, :] + _dot(la, a2_ref[z]) for z in range(2)]
        kk = k * kk_ref[...]
        ss = _dot(kk * kk, sel_ref[...])
        yield
        kn = kk * lax.rsqrt(jnp.maximum(_dot(ss, selt_ref[...]), 1e-24))
        kka = k * ka_ref[...]
        kd_sum = jnp.zeros_like(k)
        cum = []
        for z in range(2):
            cum.append(_split_dot(before[z], _sigmoid(w_pre[z]) * (-math.exp(-0.5))))
            rate = _sigmoid(a_pre[z])
            kd = k + kka * (rate - 1.0)
            kd_out[z, 0, r0:r0 + ts, :] = kd.astype(kd_out.dtype)
            b_out[z, 0, r0:r0 + ts, :] = (kn * rate).astype(b_out.dtype)
            kd_sum = kd_sum + kd
        bs = _dot(r * kd_sum * rk_ref[...], sel_ref[...])
        yield
        r_out[0, r0:r0 + ts, :] = r.astype(r_out.dtype)
        v_out[0, r0:r0 + ts, :] = v.astype(v_out.dtype)
        kn_out[0, r0:r0 + ts, :] = kn.astype(kn_out.dtype)
        gate_out[0, r0:r0 + ts, :] = gate.astype(gate_out.dtype)
        bonus_out[0, r0:r0 + ts, :] = (_dot(bs, selt_ref[...]) * v).astype(bonus_out.dtype)
        for z in range(2):
            cum_out[z, 0, r0:r0 + ts, :] = cum[z]

    _run_staggered([rows(r0) for r0 in range(0, tm, ts)], 1)


def _rwkv_prep(h, g, mu, wr, wk, wv, w1, w2, w0, a1, a2, a0, g1, g2, kk, ka, rk, sel, selt, tm=512, ts=256):
    bsz, t, d = h.shape
    nb = tm // SUBLANES
    tile = pl.BlockSpec((1, tm, d), lambda b, i: (b, i, 0))
    tile2 = pl.BlockSpec((2, 1, tm, d), lambda b, i: (0, b, i, 0))
    one = jax.ShapeDtypeStruct((bsz, t, d), BF16)
    two = jax.ShapeDtypeStruct((2, bsz, t, d), BF16)
    consts = [g, mu, wr, wk, wv, w1, w2, w0, a1, a2, a0, g1, g2, kk, ka, rk, sel, selt]
    return pl.pallas_call(
        functools.partial(_rwkv_prep_kernel, ts=ts),
        grid=(bsz, t // tm),
        in_specs=[tile,
                  pl.BlockSpec((1, SUBLANES, d), lambda b, i: (b, jnp.maximum(i * nb - 1, 0), 0)),
                  pl.BlockSpec((1, SUBLANES, d),
                               lambda b, i: (b, jnp.minimum((i + 1) * nb, t // SUBLANES - 1), 0))]
                 + [_resident(c.shape) for c in consts],
        out_specs=[tile, tile, tile, tile, tile, tile2, tile2, tile2],
        out_shape=[one, one, one, one, one, two, jax.ShapeDtypeStruct((2, bsz, t, d), F32), two],
        compiler_params=_cparams("parallel", "parallel"),
        name="l1_rwkv_prep",
    )(h, h, h, *consts)


def _blockdiag(x):
    lane = lax.broadcasted_iota(jnp.int32, x.shape, 1)
    head0 = (lane & (PAIR - 1)) < RWKV_HEAD_DIM
    zero = jnp.zeros_like(x)
    return jnp.concatenate([jnp.where(head0, x, zero), jnp.where(head0, zero, x)], axis=0)


def _chunk_local(r, k, v, kn, cum, b, rev):
    c = CHUNK
    bd = _blockdiag
    row = lax.broadcasted_iota(jnp.int32, (c, PAIR), 0)
    lane = lax.broadcasted_iota(jnp.int32, (c, PAIR), 1)
    s_idx = lane & (c - 1)
    if rev:
        cum_prev = jnp.where(row == c - 1, 0.0, pltpu.roll(cum, c - 1, 0))
        tot = cum[0:1, :]
        strict = s_idx > row
        incl = s_idx >= row
    else:
        cum_prev = jnp.where(row == 0, 0.0, pltpu.roll(cum, 1, 0))
        tot = cum[c - 1:c, :]
        strict = s_idx < row
        incl = s_idx <= row
    w_incl = jnp.exp(cum)
    w_excl = jnp.exp(cum_prev)
    w_inv = jnp.exp(-cum)
    w_tot = jnp.exp(tot)
    w_rest = jnp.exp(tot - cum)
    a_t = -kn * w_excl
    r_t = r * w_incl
    b_t = b * w_inv
    k_t = k * w_inv
    b_h = b * w_rest
    k_h = k * w_rest
    same_blk = (s_idx >> DIAG_SHIFT) == (row >> DIAG_SHIFT)

    sc = _dot_nt(jnp.concatenate([a_t, r_t], axis=0), jnp.concatenate([bd(b_t), bd(k_t)], axis=0))
    yield
    p_ab = jnp.where(strict, sc[:c, :PAIR], 0.0)
    p_ak = jnp.where(strict, sc[:c, PAIR:], 0.0)
    p_rb = jnp.where(incl, sc[c:, :PAIR], 0.0)
    p_rk = jnp.where(incl, sc[c:, PAIR:], 0.0)
    dm = jnp.where(same_blk, p_ab, 0.0)
    em = p_ab - dm
    x2 = _dot(dm, bd(dm))
    av = _dot(p_ak, bd(v))
    yield
    td = jnp.where(s_idx == row, 1.0, 0.0) + dm
    both = _dot(jnp.concatenate([x2, td], axis=0), bd(x2))
    yield
    x4 = both[:c]
    td = td + both[c:]
    both = _dot(jnp.concatenate([x4, td], axis=0), bd(x4))
    yield
    td = td + both[c:]
    td = td + _dot(td, bd(both[:c]))
    yield
    ty = _dot(td, bd(jnp.concatenate([a_t, av, em], axis=1)))
    yield
    au, f1 = ty[:, :2 * PAIR], ty[:, 2 * PAIR:]
    both = _dot(f1, bd(jnp.concatenate([f1, au], axis=1)))
    f2 = both[:, :PAIR]
    au = au + both[:, PAIR:]
    yield
    au = au + _dot(f2, bd(au))
    yield
    rhs = jnp.concatenate([bd(au), jnp.concatenate([jnp.zeros((PAIR, PAIR), F32), bd(v)], axis=1)], axis=0)
    ry = _dot(jnp.concatenate([p_rb, p_rk], axis=1), rhs)
    rhs2 = jnp.concatenate([au, jnp.concatenate([jnp.zeros((c, PAIR), F32), v], axis=1)], axis=0)
    mg = _dot(jnp.concatenate([b_h, k_h], axis=0).T, rhs2)
    yield
    rbar = r_t + ry[:, :PAIR]
    yloc = ry[:, PAIR:]
    r2 = lax.broadcasted_iota(jnp.int32, (PAIR, PAIR), 0)
    l2 = lax.broadcasted_iota(jnp.int32, (PAIR, PAIR), 1)
    same_head = (r2 >> HEAD_SHIFT) == (l2 >> HEAD_SHIFT)
    m = jnp.where(same_head, mg[:, :PAIR], 0.0) + jnp.where(r2 == l2, w_tot, 0.0)
    gg = jnp.where(same_head, mg[:, PAIR:], 0.0)
    return rbar, yloc, m, gg


def _scan_kernel(r_ref, v_ref, kn_ref, kd_ref, cum_ref, b_ref, yf_ref, yb_ref, ds_ref, loc_a, loc_b, *, unroll):
    nc = r_ref.shape[1] // CHUNK
    groups = nc // unroll
    total = (r_ref.shape[2] // PAIR) * groups
    y_refs = (yf_ref, yb_ref)
    c = CHUNK
    ds_ref[...] = jnp.zeros(ds_ref.shape, F32)
    loc_b[...] = jnp.zeros(loc_b.shape, F32)

    def places(gidx):
        grp = gidx % groups
        lanes = pl.ds(pl.multiple_of((gidx // groups) * PAIR, PAIR), PAIR)
        out = []
        for u in range(unroll):
            for z in range(2):
                cidx = grp * unroll + u
                if z == 1:
                    cidx = nc - 1 - cidx
                out.append((z, pl.ds(pl.multiple_of(cidx * c, c), c), lanes))
        return out

    def local_terms(gidx, loc_ref):
        f32 = lambda ref, *idx: ref[idx].astype(F32)
        gens = [_chunk_local(f32(r_ref, 0, sl, ln), f32(kd_ref, z, 0, sl, ln), f32(v_ref, 0, sl, ln),
                             f32(kn_ref, 0, sl, ln), cum_ref[z, 0, sl, ln], f32(b_ref, z, 0, sl, ln),
                             rev=(z == 1))
                for z, sl, ln in places(gidx)]

        def park(j, gen):
            rbar, yloc, m, gg = yield from gen
            loc_ref[j, 0:c, :] = rbar
            loc_ref[j, c:2 * c, :] = yloc
            loc_ref[j, 2 * c:2 * c + PAIR, :] = m
            loc_ref[j, 2 * c + PAIR:, :] = gg

        return [park(j, gen) for j, gen in enumerate(gens)]

    def recurrence(gidx, loc_ref):
        keep = jnp.where(gidx % groups == 0, 0.0, 1.0)
        ds = [ds_ref[0] * keep, ds_ref[1] * keep]
        for j, (z, sl, ln) in enumerate(places(gidx)):
            both = _dot(jnp.concatenate([loc_ref[j, 0:c, :], loc_ref[j, 2 * c:2 * c + PAIR, :]], axis=0), ds[z])
            y_refs[z][0, sl, ln] = (both[:c] + loc_ref[j, c:2 * c, :]).astype(y_refs[z].dtype)
            ds[z] = both[c:] + loc_ref[j, 2 * c + PAIR:, :]
            if z == 1:
                yield
        ds_ref[0] = ds[0]
        ds_ref[1] = ds[1]

    def body(it, carry):
        first = 2 * it
        _run_staggered(local_terms(first, loc_a) + [recurrence(jnp.maximum(first - 1, 0), loc_b)], 0)
        _run_staggered(local_terms(first + 1, loc_b) + [recurrence(first, loc_a)], 0)
        return carry

    lax.fori_loop(0, total // 2, body, 0)
    _run_staggered([recurrence(total - 1, loc_b)], 0)


def _wkv7_scan(r, v, kn, kd, cum, b, unroll=8, pairs=2):
    bsz, t, d = r.shape
    nc = t // CHUNK
    unroll = min(unroll, nc // 2)
    assert nc % (2 * unroll) == 0, "the scan kernel takes chunk groups in pairs"
    one = pl.BlockSpec((1, t, pairs * PAIR), lambda bb, p: (bb, 0, p))
    two = pl.BlockSpec((2, 1, t, pairs * PAIR), lambda bb, p: (0, bb, 0, p))
    out = jax.ShapeDtypeStruct((bsz, t, d), BF16)
    return pl.pallas_call(
        functools.partial(_scan_kernel, unroll=unroll),
        grid=(bsz, d // (pairs * PAIR)),
        in_specs=[one, one, one, two, two, two],
        out_specs=[one, one],
        out_shape=[out, out],
        scratch_shapes=[pltpu.VMEM((2, PAIR, PAIR), F32)]
                       + [pltpu.VMEM((2 * unroll, 2 * CHUNK + 2 * PAIR, PAIR), F32)] * 2,
        compiler_params=_cparams("parallel", "parallel"),
        name="l1_wkv7_scan",
    )(r, v, kn, kd, cum, b)


def _l1_tail_kernel(yf_ref, yb_ref, bonus_ref, gate_ref, h_ref, lg_ref, lb_ref, wo_ref, sel_ref, selt_ref,
                    gx_ref, wq_ref, kv_ref, wxo_ref, out_ref, *, ts):
    tq = h_ref.shape[1]
    inv_n = 1.0 / RWKV_HEAD_DIM

    def sub(r0):
        rs = slice(r0, r0 + ts)
        y = yf_ref[0, rs, :] + yb_ref[0, rs, :]
        mu = _dot(y, sel_ref[...])
        yield
        yc = y - _dot(mu, selt_ref[...]) * inv_n
        var = _dot(yc * yc, sel_ref[...])
        yield
        yn = yc * lax.rsqrt(_dot(var, selt_ref[...]) * inv_n + GN_EPS) * lg_ref[...] + lb_ref[...]
        h1 = h_ref[0, rs, :] + _dot((yn + bonus_ref[0, rs, :]) * gate_ref[0, rs, :], wo_ref[...])
        yield
        out_ref[0, rs, :] = yield from _xattn_rows(h1, gx_ref, wq_ref, kv_ref, wxo_ref)

    _run_staggered([sub(r0) for r0 in range(0, tq, ts)], 2)


def _l1_tail(yf, yb, bonus, gate, h, lg, lb, wo, sel, selt, gx, wq, kv, wxo, tq=1024, ts=256):
    bsz, t, d = h.shape
    tile = pl.BlockSpec((1, tq, d), lambda b, i: (b, i, 0))
    return pl.pallas_call(
        functools.partial(_l1_tail_kernel, ts=ts),
        grid=(bsz, t // tq),
        in_specs=[tile, tile, tile, tile, tile, _resident((1, d)), _resident((1, d)), _resident(wo.shape),
                  _resident(sel.shape), _resident(selt.shape), _resident((1, d)), _resident(wq.shape),
                  pl.BlockSpec((1, kv.shape[1], 2 * d), lambda b, i: (b, 0, 0)), _resident(wxo.shape)],
        out_specs=tile,
        out_shape=jax.ShapeDtypeStruct(h.shape, F32),
        compiler_params=_cparams("parallel", "parallel"),
        name="l1_tail",
    )(yf, yb, bonus, gate, h, lg, lb, wo, sel, selt, gx, wq, kv, wxo)


def _pad_lora_out(w2):
    zero = jnp.zeros_like(w2[0])
    return jnp.stack([jnp.concatenate([w2[0], zero], axis=0), jnp.concatenate([zero, w2[1]], axis=0)])


def kernel(x, mem, rel_bias_table, norm_mix, norm_xattn, norm_mem, norm_ffn, norm_final, ab_w_in, ab_w_out, conv_w, conv_b, conv_ln_g, conv_ln_b, diff_lq1, diff_lk1, diff_lq2, diff_lk2, diff_subln_g, rwkv_mu, rwkv_w_r, rwkv_w_k, rwkv_w_v, rwkv_w_o, rwkv_w0, rwkv_w1, rwkv_w2, rwkv_a0, rwkv_a1, rwkv_a2, rwkv_g1, rwkv_g2, rwkv_k_k, rwkv_k_a, rwkv_r_k, rwkv_ln_g, rwkv_ln_b, xattn_w_q, xattn_w_kv, xattn_w_o, ffn_w_up, ffn_w_down):
    bsz, t, d = x.shape
    n = bsz * t
    depth = norm_mix.shape[0]
    n_mem = mem.shape[1]
    cc = conv_w.shape[-1]
    qk = DIFF_HEADS * 2 * DIFF_HEAD_DIM
    vw = ab_w_in.shape[-1] - 2 * cc - 2 * qk
    bf = lambda w: w.astype(BF16)
    row = lambda w: w.reshape(1, -1)

    heads = d // RWKV_HEAD_DIM
    head_of = jnp.arange(d, dtype=jnp.int32) // RWKV_HEAD_DIM
    assert heads <= LANES
    sel = (head_of[:, None] == jnp.arange(LANES, dtype=jnp.int32)[None, :]).astype(BF16)
    selt = sel.T

    h = x.reshape(n, d)
    mem2 = mem.reshape(bsz * n_mem, d)
    for i in range(depth):
        j = i // 2
        kv = _norm_linear(mem2, row(norm_mem[i]), bf(xattn_w_kv[i]), BF16).reshape(bsz, n_mem, 2 * d)
        if i % 2 == 0:
            lam_init = 0.8 - 0.6 * math.exp(-0.3 * i)
            h3 = h.reshape(bsz, t, d)
            u, q, k, v = _inproj(h3, row(norm_mix[i]), bf(ab_w_in[j]), conv_w[j], row(conv_b[j]),
                                 row(conv_ln_g[j]), row(conv_ln_b[j]), cc, qk, vw)
            lq = jnp.stack([diff_lq1[j], diff_lk1[j], diff_lq2[j], diff_lk2[j]])
            o = _diff_attention(q, k, v, rel_bias_table.reshape(-1), lq, row(diff_subln_g[j]), lam_init)
            h = _l0_tail(u, o, h3, bf(ab_w_out[j]), row(norm_xattn[i]), bf(xattn_w_q[i]), kv,
                         bf(xattn_w_o[i])).reshape(n, d)
        else:
            h3 = h.reshape(bsz, t, d)
            w1 = bf(jnp.concatenate([rwkv_w1[j, 0], rwkv_w1[j, 1]], axis=1))
            a1 = bf(jnp.concatenate([rwkv_a1[j, 0], rwkv_a1[j, 1]], axis=1))
            r, v, kn, gate, bonus, kd, cum, b = _rwkv_prep(
                h3, row(norm_mix[i]), rwkv_mu[j], bf(rwkv_w_r[j]), bf(rwkv_w_k[j]), bf(rwkv_w_v[j]),
                w1, bf(_pad_lora_out(rwkv_w2[j])), rwkv_w0[j], a1, bf(_pad_lora_out(rwkv_a2[j])), rwkv_a0[j],
                bf(rwkv_g1[j]), bf(rwkv_g2[j]), row(rwkv_k_k[j]), row(rwkv_k_a[j]), row(rwkv_r_k[j]),
                sel, selt)
            yf, yb = _wkv7_scan(r, v, kn, kd, cum, b)
            h = _l1_tail(yf, yb, bonus, gate, h3, row(rwkv_ln_g[j]), row(rwkv_ln_b[j]), bf(rwkv_w_o[j]), sel, selt,
                         row(norm_xattn[i]), bf(xattn_w_q[i]), kv, bf(xattn_w_o[i])).reshape(n, d)
        h = _mlp(h, row(norm_ffn[i]), bf(ffn_w_up[i]), bf(ffn_w_down[i]), row(norm_final),
                 final_norm=(i == depth - 1))
    return h.reshape(bsz, t, d)
```

```python
import functools
import math

import jax
import jax.numpy as jnp
from jax import lax
from jax.experimental import pallas as pl
from jax.experimental.pallas import tpu as pltpu

F32 = jnp.float32
BF16 = jnp.bfloat16

V7X_VMEM_BYTES = 64 * 1024 * 1024
VMEM_LIMIT_BYTES = V7X_VMEM_BYTES - 8 * 1024 * 1024

LOG2E = math.log2(math.e)
NORM_EPS = 1e-6
CONV_LN_EPS = 1e-5
GN_EPS = 64e-5
CONV_WIDTH = 31
CONV_PAD = CONV_WIDTH // 2
CONV_HALO = 16
DIFF_HEADS = 4
DIFF_HEAD_DIM = 64
REL_BUCKETS = 32
REL_MAX_DIST = 128
XATTN_HEADS = 4
RWKV_HEAD_DIM = 64
SUBLANES = 8
LANES = 128
CHUNK = 64
CHUNK_SHIFT = CHUNK.bit_length() - 1
HEAD_SHIFT = RWKV_HEAD_DIM.bit_length() - 1
DIAG_SHIFT = 4
PAIR = 2 * RWKV_HEAD_DIM


def _cparams(*sem):
    return pltpu.CompilerParams(dimension_semantics=sem, vmem_limit_bytes=VMEM_LIMIT_BYTES)


def _resident(shape):
    nd = len(shape)
    return pl.BlockSpec(shape, lambda *_: (0,) * nd, pipeline_mode=pl.Buffered(1))


def _rms(x, g):
    ms = jnp.mean(x * x, axis=-1, keepdims=True)
    return x * lax.rsqrt(ms + NORM_EPS) * g


def _sigmoid(x):
    return 1.0 / (1.0 + jnp.exp(-x))


def _dot(a, b):
    return jnp.dot(a.astype(BF16), b.astype(BF16), preferred_element_type=F32)


def _dot_nt(a, b):
    return lax.dot_general(a.astype(BF16), b.astype(BF16), (((1,), (1,)), ((), ())),
                           preferred_element_type=F32)


def _split_dot(mat, x):
    hi = x.astype(BF16)
    lo = (x - hi.astype(F32)).astype(BF16)
    return (jnp.dot(mat, hi, preferred_element_type=F32) + jnp.dot(mat, lo, preferred_element_type=F32))


def _run_staggered(gens, offset):
    results = [None] * len(gens)
    running = [True] * len(gens)
    rnd = 0
    while any(running):
        for i, gen in enumerate(gens):
            if rnd >= i * offset and running[i]:
                try:
                    next(gen)
                except StopIteration as stop:
                    results[i] = stop.value
                    running[i] = False
        rnd += 1
    return results


def _inproj_kernel(h_ref, hp_ref, hn_ref, g_ref, w_ref, cw_ref, cb_ref, cg_ref, cbeta_ref,
                   u_ref, q_ref, k_ref, v_ref, xp_ref, *, cc, qk, scale, rows):
    i = pl.program_id(1)
    last = pl.num_programs(1) - 1
    tm = h_ref.shape[1]
    g = g_ref[...]
    x = _rms(h_ref[0], g).astype(BF16)
    x_ext = jnp.concatenate([_rms(hp_ref[0], g).astype(BF16), x, _rms(hn_ref[0], g).astype(BF16)], axis=0)
    pu = _dot(x_ext, w_ref[:, :2 * cc])
    u = pu[:, :cc] * _sigmoid(pu[:, cc:])
    xp_ref[0:CONV_HALO, :] = u[:CONV_HALO] * jnp.where(i > 0, 1.0, 0.0)
    xp_ref[CONV_HALO:CONV_HALO + tm, :] = u[CONV_HALO:CONV_HALO + tm]
    xp_ref[CONV_HALO + tm:, :] = u[CONV_HALO + tm:] * jnp.where(i < last, 1.0, 0.0)

    def conv():
        for t0 in range(0, tm, rows):
            u_ref[0, t0:t0 + rows, :] = _conv_rows(xp_ref, t0, rows, cw_ref, cb_ref, cg_ref, cbeta_ref)
            yield

    def qkv():
        o = 2 * cc
        q_ref[0] = (_dot(x, w_ref[:, o:o + qk]) * scale).astype(BF16)
        yield
        k_ref[0] = _dot(x, w_ref[:, o + qk:o + 2 * qk]).astype(BF16)
        yield
        v_ref[0] = _dot(x, w_ref[:, o + 2 * qk:]).astype(BF16)

    _run_staggered([conv(), qkv()], 0)


def _inproj(h, g, w, cw, cb, cbg, cbeta, cc, qk, vw, tm=512, rows=64):
    bsz, t, d = h.shape
    nb = tm // CONV_HALO
    tile = lambda width: pl.BlockSpec((1, tm, width), lambda b, i: (b, i, 0))
    out = lambda width: jax.ShapeDtypeStruct((bsz, t, width), BF16)
    return pl.pallas_call(
        functools.partial(_inproj_kernel, cc=cc, qk=qk, scale=DIFF_HEAD_DIM ** -0.5 * LOG2E, rows=rows),
        grid=(bsz, t // tm),
        in_specs=[tile(d),
                  pl.BlockSpec((1, CONV_HALO, d), lambda b, i: (b, jnp.maximum(i * nb - 1, 0), 0)),
                  pl.BlockSpec((1, CONV_HALO, d),
                               lambda b, i: (b, jnp.minimum((i + 1) * nb, t // CONV_HALO - 1), 0)),
                  _resident((1, d)), _resident(w.shape), _resident(cw.shape), _resident((1, cc)),
                  _resident((1, cc)), _resident((1, cc))],
        out_specs=[tile(cc), tile(qk), tile(qk), tile(vw)],
        out_shape=[out(cc), out(qk), out(qk), out(vw)],
        scratch_shapes=[pltpu.VMEM((tm + 2 * CONV_HALO, cc), F32)],
        compiler_params=_cparams("parallel", "parallel"),
        name="l0_inproj",
    )(h, h, h, g, w, cw, cb, cbg, cbeta)


def _conv_rows(xp_ref, base, rows, w_ref, b_ref, g_ref, beta_ref):
    c = xp_ref.shape[-1]
    off = CONV_HALO - CONV_PAD
    span = rows + 2 * CONV_HALO
    pieces = []
    for c0 in range(0, c, LANES):
        win = xp_ref[base:base + span, c0:c0 + LANES]
        acc = jnp.zeros((rows, LANES), F32)
        for rem in range(SUBLANES):
            shifted = win if rem == 0 else pltpu.roll(win, span - rem, 0)
            for start in range(0, 2 * CONV_HALO, SUBLANES):
                k = start + rem - off
                if 0 <= k < CONV_WIDTH:
                    acc = acc + shifted[start:start + rows, :] * w_ref[k:k + 1, c0:c0 + LANES]
        pieces.append(acc)
    y = jnp.concatenate(pieces, axis=1) + b_ref[...]
    mu = jnp.mean(y, axis=-1, keepdims=True)
    yc = y - mu
    var = jnp.mean(yc * yc, axis=-1, keepdims=True)
    yn = yc * lax.rsqrt(var + CONV_LN_EPS) * g_ref[...] + beta_ref[...]
    return (yn * _sigmoid(yn)).astype(BF16)


def _t5_bucket(rel):
    nb = REL_BUCKETS // 2
    max_exact = nb // 2
    n = jnp.abs(rel)
    large = jnp.full(rel.shape, max_exact, jnp.int32)
    steps = nb - max_exact
    for m in range(1, steps):
        thr = math.ceil(max_exact * (REL_MAX_DIST / max_exact) ** (m / steps) - 1e-9)
        large = large + jnp.where(n >= thr, 1, 0)
    mag = jnp.where(n < max_exact, n, large)
    return mag + jnp.where(rel > 0, nb, 0)


def _diffattn_kernel(tbl_ref, lq_ref, sg_ref, q_ref, k_ref, v_ref, o_ref, bias_ref, *, tq, ts, kb, lam_init):
    h = pl.program_id(0)
    qi = pl.program_id(1)
    b = pl.program_id(2)
    t = k_ref.shape[1]

    @pl.when(b == 0)
    def _():
        u = lax.broadcasted_iota(jnp.int32, (1, t + tq), 1)
        bucket = _t5_bucket(u - (tq - 1) - qi * tq)
        line = jnp.zeros((1, t + tq), F32)
        for i in range(REL_BUCKETS):
            line = jnp.where(bucket == i, tbl_ref[i * DIFF_HEADS + h], line)
        rows = pltpu.roll(jnp.broadcast_to(line * LOG2E, (tq, t + tq)), 1, 1, stride=1, stride_axis=0)
        bias_ref[...] = rows[:, tq:]

    lq = lq_ref[...]
    lam = (jnp.exp(jnp.sum(lq[0:1] * lq[1:2], axis=-1, keepdims=True))
           - jnp.exp(jnp.sum(lq[2:3] * lq[3:4], axis=-1, keepdims=True)) + lam_init)
    hw = q_ref.shape[-1]
    first = lax.broadcasted_iota(jnp.int32, (ts, hw), 1) < DIFF_HEAD_DIM
    nkb = t // kb

    def softmax_v(r0, comp):
        q = q_ref[0, r0:r0 + ts, :]
        qc = jnp.where(first, q, jnp.zeros_like(q)) if comp == 0 else jnp.where(first, jnp.zeros_like(q), q)
        s = []
        mx = None
        for j in range(nkb):
            sj = _dot_nt(qc, k_ref[0, j * kb:(j + 1) * kb, :]) + bias_ref[r0:r0 + ts, j * kb:(j + 1) * kb]
            mj = jnp.max(sj, axis=-1, keepdims=True)
            mx = mj if mx is None else jnp.maximum(mx, mj)
            s.append(sj)
            yield
        pv = None
        for j in range(nkb):
            vj = v_ref[0, j * kb:(j + 1) * kb, :]
            dj = _dot(jnp.exp2(s[j] - mx), jnp.concatenate([vj, jnp.ones_like(vj)], axis=1))
            pv = dj if pv is None else pv + dj
            yield
        return pv[:, :hw] / pv[:, hw:]

    starts = range(0, tq, ts)
    maps = _run_staggered([softmax_v(r0, comp) for r0 in starts for comp in range(2)], nkb)
    for i, r0 in enumerate(starts):
        o = maps[2 * i] - lam * maps[2 * i + 1]
        o = o * lax.rsqrt(jnp.mean(o * o, axis=-1, keepdims=True) + NORM_EPS) * sg_ref[...] * (1.0 - lam_init)
        o_ref[0, r0:r0 + ts, :] = o.astype(BF16)


def _diff_attention(q, k, v, tbl, lq, sg, lam_init, tq=1024, ts=256, kb=256):
    bsz, t, _ = q.shape
    hw = 2 * DIFF_HEAD_DIM
    return pl.pallas_call(
        functools.partial(_diffattn_kernel, tq=tq, ts=ts, kb=kb, lam_init=lam_init),
        grid=(DIFF_HEADS, t // tq, bsz),
        in_specs=[pl.BlockSpec(memory_space=pltpu.SMEM), _resident(lq.shape), _resident(sg.shape),
                  pl.BlockSpec((1, tq, hw), lambda h, i, b: (b, i, h)),
                  pl.BlockSpec((1, t, hw), lambda h, i, b: (b, 0, h)),
                  pl.BlockSpec((1, t, hw), lambda h, i, b: (b, 0, h))],
        out_specs=pl.BlockSpec((1, tq, hw), lambda h, i, b: (b, i, h)),
        out_shape=jax.ShapeDtypeStruct(q.shape, BF16),
        scratch_shapes=[pltpu.VMEM((tq, t), F32)],
        compiler_params=_cparams("parallel", "parallel", "arbitrary"),
        name="l0_diffattn",
    )(tbl, lq, sg, q, k, v)


def _norm_linear_kernel(x_ref, g_ref, w_ref, o_ref):
    o_ref[...] = _dot(_rms(x_ref[...], g_ref[...]), w_ref[...]).astype(o_ref.dtype)


def _norm_linear(x, g, w, out_dtype, tm=512):
    n, d = x.shape
    m = w.shape[1]
    return pl.pallas_call(
        _norm_linear_kernel,
        grid=(n // tm,),
        in_specs=[pl.BlockSpec((tm, d), lambda i: (i, 0)), _resident((1, d)), _resident(w.shape)],
        out_specs=pl.BlockSpec((tm, m), lambda i: (i, 0)),
        out_shape=jax.ShapeDtypeStruct((n, m), out_dtype),
        compiler_params=_cparams("parallel"),
        name="norm_linear",
    )(x, g, w)


def _xattn_rows(h, g_ref, wq_ref, kv_ref, wo_ref):
    d = h.shape[-1]
    hd = d // XATTN_HEADS
    q = (_dot(_rms(h, g_ref[...]), wq_ref[...]) * (hd ** -0.5 * LOG2E)).astype(BF16)
    yield
    outs = []
    for i in range(XATTN_HEADS):
        kh = kv_ref[0, :, i * hd:(i + 1) * hd]
        vh = kv_ref[0, :, d + i * hd:d + (i + 1) * hd]
        s = _dot_nt(q[:, i * hd:(i + 1) * hd], kh)
        e = jnp.exp2(s - jnp.max(s, axis=-1, keepdims=True))
        p = e / jnp.sum(e, axis=-1, keepdims=True)
        outs.append(_dot(p, vh).astype(BF16))
    yield
    return h + _dot(jnp.concatenate(outs, axis=1), wo_ref[...])


def _l0_tail_kernel(u_ref, o_ref, h_ref, wout_ref, gx_ref, wq_ref, kv_ref, wo_ref, out_ref, *, ts):
    tq = h_ref.shape[1]
    cc = u_ref.shape[-1]

    def sub(r0):
        rs = slice(r0, r0 + ts)
        h1 = h_ref[0, rs, :] + _dot(u_ref[0, rs, :], wout_ref[:cc, :]) + _dot(o_ref[0, rs, :], wout_ref[cc:, :])
        yield
        out_ref[0, rs, :] = yield from _xattn_rows(h1, gx_ref, wq_ref, kv_ref, wo_ref)

    _run_staggered([sub(r0) for r0 in range(0, tq, ts)], 1)


def _l0_tail(u, o, h, wout, gx, wq, kv, wo, tq=1024, ts=256):
    bsz, t, d = h.shape
    tile = lambda w: pl.BlockSpec((1, tq, w), lambda b, i: (b, i, 0))
    return pl.pallas_call(
        functools.partial(_l0_tail_kernel, ts=ts),
        grid=(bsz, t // tq),
        in_specs=[tile(u.shape[-1]), tile(o.shape[-1]), tile(d), _resident(wout.shape), _resident((1, d)),
                  _resident(wq.shape), pl.BlockSpec((1, kv.shape[1], 2 * d), lambda b, i: (b, 0, 0)),
                  _resident(wo.shape)],
        out_specs=tile(d),
        out_shape=jax.ShapeDtypeStruct(h.shape, F32),
        compiler_params=_cparams("parallel", "parallel"),
        name="l0_tail",
    )(u, o, h, wout, gx, wq, kv, wo)


def _mlp_kernel(h_ref, g_ref, wu_ref, wd_ref, gf_ref, o_ref, *, hc, final_norm):
    h = h_ref[...]
    xn = _rms(h, g_ref[...]).astype(BF16)
    acc = h
    for c in range(wu_ref.shape[1] // hc):
        a = jnp.maximum(_dot(xn, wu_ref[:, c * hc:(c + 1) * hc]), 0.0)
        acc = acc + _dot(a * a, wd_ref[c * hc:(c + 1) * hc, :])
    if final_norm:
        acc = _rms(acc, gf_ref[...])
    o_ref[...] = acc


def _mlp(h, g, wu, wd, gf, final_norm, tm=512, hc=1024):
    n, d = h.shape
    row = lambda i: (i, 0)
    return pl.pallas_call(
        functools.partial(_mlp_kernel, hc=hc, final_norm=final_norm),
        grid=(n // tm,),
        in_specs=[pl.BlockSpec((tm, d), row), _resident((1, d)), _resident(wu.shape), _resident(wd.shape),
                  _resident((1, d))],
        out_specs=pl.BlockSpec((tm, d), row),
        out_shape=jax.ShapeDtypeStruct((n, d), F32),
        compiler_params=_cparams("parallel"),
        name="mlp",
    )(h, g, wu, wd, gf)


def _rwkv_prep_kernel(h_ref, hp_ref, hn_ref, g_ref, mu_ref, wr_ref, wk_ref, wv_ref, w1_ref, w2_ref, w0_ref,
                      a1_ref, a2_ref, a0_ref, g1_ref, g2_ref, kk_ref, ka_ref, rk_ref, sel_ref, selt_ref,
                      r_out, v_out, kn_out, gate_out, bonus_out, kd_out, cum_out, b_out, *, ts):
    i = pl.program_id(1)
    last = pl.num_programs(1) - 1
    tm = h_ref.shape[1]
    g = g_ref[...]
    ti = lax.broadcasted_iota(jnp.int32, (ts, ts), 0)
    si = lax.broadcasted_iota(jnp.int32, (ts, ts), 1)
    same_chunk = (ti >> CHUNK_SHIFT) == (si >> CHUNK_SHIFT)
    before = (jnp.where(same_chunk, jnp.where(si <= ti, 1.0, 0.0), 0.0).astype(BF16),
              jnp.where(same_chunk, jnp.where(si >= ti, 1.0, 0.0), 0.0).astype(BF16))

    def rows(r0):
        x = _rms(h_ref[0, r0:r0 + ts, :], g)
        if r0 == 0:
            prev_row = _rms(hp_ref[0], g)[SUBLANES - 1:, :] * jnp.where(i > 0, 1.0, 0.0)
        else:
            prev_row = _rms(h_ref[0, r0 - SUBLANES:r0, :], g)[SUBLANES - 1:, :]
        if r0 + ts == tm:
            next_row = _rms(hn_ref[0], g)[0:1, :] * jnp.where(i < last, 1.0, 0.0)
        else:
            next_row = _rms(h_ref[0, r0 + ts:r0 + ts + SUBLANES, :], g)[0:1, :]
        rowid = lax.broadcasted_iota(jnp.int32, x.shape, 0)
        x_prev = jnp.where(rowid == 0, prev_row, pltpu.roll(x, 1, 0))
        x_next = jnp.where(rowid == ts - 1, next_row, pltpu.roll(x, ts - 1, 0))
        hh = 0.5 * (x_prev + x_next) - x
        mix = lambda j: x + hh * mu_ref[j:j + 1, :]
        r = _dot(mix(0), wr_ref[...])
        k = _dot(mix(2), wk_ref[...])
        v = _dot(mix(3), wv_ref[...])
        yield
        gate_in = _dot(mix(5), g1_ref[...])
        lw = _dot(mix(1), w1_ref[...])
        la = _dot(mix(4), a1_ref[...])
        yield
        gate = _dot(_sigmoid(gate_in), g2_ref[...])
        lw = jnp.tanh(lw)
        w_pre = [w0_ref[z:z + 1, :] + _dot(lw, w2_ref[z]) for z in range(2)]
        a_pre = [a0_ref[z:z + 1, :] + _dot(la, a2_ref[z]) for z in range(2)]
        kk = k * kk_ref[...]
        ss = _dot(kk * kk, sel_ref[...])
        yield
        kn = kk * lax.rsqrt(jnp.maximum(_dot(ss, selt_ref[...]), 1e-24))
        kka = k * ka_ref[...]
        kd_sum = jnp.zeros_like(k)
        cum = []
        for z in range(2):
            cum.append(_split_dot(before[z], _sigmoid(w_pre[z]) * (-math.exp(-0.5))))
            rate = _sigmoid(a_pre[z])
            kd = k + kka * (rate - 1.0)
            kd_out[z, 0, r0:r0 + ts, :] = kd.astype(kd_out.dtype)
            b_out[z, 0, r0:r0 + ts, :] = (kn * rate).astype(b_out.dtype)
            kd_sum = kd_sum + kd
        bs = _dot(r * kd_sum * rk_ref[...], sel_ref[...])
        yield
        r_out[0, r0:r0 + ts, :] = r.astype(r_out.dtype)
        v_out[0, r0:r0 + ts, :] = v.astype(v_out.dtype)
        kn_out[0, r0:r0 + ts, :] = kn.astype(kn_out.dtype)
        gate_out[0, r0:r0 + ts, :] = gate.astype(gate_out.dtype)
        bonus_out[0, r0:r0 + ts, :] = (_dot(bs, selt_ref[...]) * v).astype(bonus_out.dtype)
        for z in range(2):
            cum_out[z, 0, r0:r0 + ts, :] = cum[z]

    _run_staggered([rows(r0) for r0 in range(0, tm, ts)], 1)


def _rwkv_prep(h, g, mu, wr, wk, wv, w1, w2, w0, a1, a2, a0, g1, g2, kk, ka, rk, sel, selt, tm=512, ts=256):
    bsz, t, d = h.shape
    nb = tm // SUBLANES
    tile = pl.BlockSpec((1, tm, d), lambda b, i: (b, i, 0))
    tile2 = pl.BlockSpec((2, 1, tm, d), lambda b, i: (0, b, i, 0))
    one = jax.ShapeDtypeStruct((bsz, t, d), BF16)
    two = jax.ShapeDtypeStruct((2, bsz, t, d), BF16)
    consts = [g, mu, wr, wk, wv, w1, w2, w0, a1, a2, a0, g1, g2, kk, ka, rk, sel, selt]
    return pl.pallas_call(
        functools.partial(_rwkv_prep_kernel, ts=ts),
        grid=(bsz, t // tm),
        in_specs=[tile,
                  pl.BlockSpec((1, SUBLANES, d), lambda b, i: (b, jnp.maximum(i * nb - 1, 0), 0)),
                  pl.BlockSpec((1, SUBLANES, d),
                               lambda b, i: (b, jnp.minimum((i + 1) * nb, t // SUBLANES - 1), 0))]
                 + [_resident(c.shape) for c in consts],
        out_specs=[tile, tile, tile, tile, tile, tile2, tile2, tile2],
        out_shape=[one, one, one, one, one, two, jax.ShapeDtypeStruct((2, bsz, t, d), F32), two],
        compiler_params=_cparams("parallel", "parallel"),
        name="l1_rwkv_prep",
    )(h, h, h, *consts)


def _blockdiag(x):
    lane = lax.broadcasted_iota(jnp.int32, x.shape, 1)
    head0 = (lane & (PAIR - 1)) < RWKV_HEAD_DIM
    zero = jnp.zeros_like(x)
    return jnp.concatenate([jnp.where(head0, x, zero), jnp.where(head0, zero, x)], axis=0)


def _chunk_local(r, k, v, kn, cum, b, rev):
    c = CHUNK
    bd = _blockdiag
    row = lax.broadcasted_iota(jnp.int32, (c, PAIR), 0)
    lane = lax.broadcasted_iota(jnp.int32, (c, PAIR), 1)
    s_idx = lane & (c - 1)
    if rev:
        cum_prev = jnp.where(row == c - 1, 0.0, pltpu.roll(cum, c - 1, 0))
        tot = cum[0:1, :]
        strict = s_idx > row
        incl = s_idx >= row
    else:
        cum_prev = jnp.where(row == 0, 0.0, pltpu.roll(cum, 1, 0))
        tot = cum[c - 1:c, :]
        strict = s_idx < row
        incl = s_idx <= row
    w_incl = jnp.exp(cum)
    w_excl = jnp.exp(cum_prev)
    w_inv = jnp.exp(-cum)
    w_tot = jnp.exp(tot)
    w_rest = jnp.exp(tot - cum)
    a_t = -kn * w_excl
    r_t = r * w_incl
    b_t = b * w_inv
    k_t = k * w_inv
    b_h = b * w_rest
    k_h = k * w_rest
    same_blk = (s_idx >> DIAG_SHIFT) == (row >> DIAG_SHIFT)

    sc = _dot_nt(jnp.concatenate([a_t, r_t], axis=0), jnp.concatenate([bd(b_t), bd(k_t)], axis=0))
    yield
    p_ab = jnp.where(strict, sc[:c, :PAIR], 0.0)
    p_ak = jnp.where(strict, sc[:c, PAIR:], 0.0)
    p_rb = jnp.where(incl, sc[c:, :PAIR], 0.0)
    p_rk = jnp.where(incl, sc[c:, PAIR:], 0.0)
    dm = jnp.where(same_blk, p_ab, 0.0)
    em = p_ab - dm
    x2 = _dot(dm, bd(dm))
    av = _dot(p_ak, bd(v))
    yield
    td = jnp.where(s_idx == row, 1.0, 0.0) + dm
    both = _dot(jnp.concatenate([x2, td], axis=0), bd(x2))
    yield
    x4 = both[:c]
    td = td + both[c:]
    both = _dot(jnp.concatenate([x4, td], axis=0), bd(x4))
    yield
    td = td + both[c:]
    td = td + _dot(td, bd(both[:c]))
    yield
    ty = _dot(td, bd(jnp.concatenate([a_t, av, em], axis=1)))
    yield
    au, f1 = ty[:, :2 * PAIR], ty[:, 2 * PAIR:]
    both = _dot(f1, bd(jnp.concatenate([f1, au], axis=1)))
    f2 = both[:, :PAIR]
    au = au + both[:, PAIR:]
    yield
    au = au + _dot(f2, bd(au))
    yield
    rhs = jnp.concatenate([bd(au), jnp.concatenate([jnp.zeros((PAIR, PAIR), F32), bd(v)], axis=1)], axis=0)
    ry = _dot(jnp.concatenate([p_rb, p_rk], axis=1), rhs)
    rhs2 = jnp.concatenate([au, jnp.concatenate([jnp.zeros((c, PAIR), F32), v], axis=1)], axis=0)
    mg = _dot(jnp.concatenate([b_h, k_h], axis=0).T, rhs2)
    yield
    rbar = r_t + ry[:, :PAIR]
    yloc = ry[:, PAIR:]
    r2 = lax.broadcasted_iota(jnp.int32, (PAIR, PAIR), 0)
    l2 = lax.broadcasted_iota(jnp.int32, (PAIR, PAIR), 1)
    same_head = (r2 >> HEAD_SHIFT) == (l2 >> HEAD_SHIFT)
    m = jnp.where(same_head, mg[:, :PAIR], 0.0) + jnp.where(r2 == l2, w_tot, 0.0)
    gg = jnp.where(same_head, mg[:, PAIR:], 0.0)
    return rbar, yloc, m, gg


def _scan_kernel(r_ref, v_ref, kn_ref, kd_ref, cum_ref, b_ref, yf_ref, yb_ref, ds_ref, loc_a, loc_b, *, unroll):
    nc = r_ref.shape[1] // CHUNK
    groups = nc // unroll
    total = (r_ref.shape[2] // PAIR) * groups
    y_refs = (yf_ref, yb_ref)
    c = CHUNK
    ds_ref[...] = jnp.zeros(ds_ref.shape, F32)
    loc_b[...] = jnp.zeros(loc_b.shape, F32)

    def places(gidx):
        grp = gidx % groups
        lanes = pl.ds(pl.multiple_of((gidx // groups) * PAIR, PAIR), PAIR)
        out = []
        for u in range(unroll):
            for z in range(2):
                cidx = grp * unroll + u
                if z == 1:
                    cidx = nc - 1 - cidx
                out.append((z, pl.ds(pl.multiple_of(cidx * c, c), c), lanes))
        return out

    def local_terms(gidx, loc_ref):
        f32 = lambda ref, *idx: ref[idx].astype(F32)
        gens = [_chunk_local(f32(r_ref, 0, sl, ln), f32(kd_ref, z, 0, sl, ln), f32(v_ref, 0, sl, ln),
                             f32(kn_ref, 0, sl, ln), cum_ref[z, 0, sl, ln], f32(b_ref, z, 0, sl, ln),
                             rev=(z == 1))
                for z, sl, ln in places(gidx)]

        def park(j, gen):
            rbar, yloc, m, gg = yield from gen
            loc_ref[j, 0:c, :] = rbar
            loc_ref[j, c:2 * c, :] = yloc
            loc_ref[j, 2 * c:2 * c + PAIR, :] = m
            loc_ref[j, 2 * c + PAIR:, :] = gg

        return [park(j, gen) for j, gen in enumerate(gens)]

    def recurrence(gidx, loc_ref):
        keep = jnp.where(gidx % groups == 0, 0.0, 1.0)
        ds = [ds_ref[0] * keep, ds_ref[1] * keep]
        for j, (z, sl, ln) in enumerate(places(gidx)):
            both = _dot(jnp.concatenate([loc_ref[j, 0:c, :], loc_ref[j, 2 * c:2 * c + PAIR, :]], axis=0), ds[z])
            y_refs[z][0, sl, ln] = (both[:c] + loc_ref[j, c:2 * c, :]).astype(y_refs[z].dtype)
            ds[z] = both[c:] + loc_ref[j, 2 * c + PAIR:, :]
            if z == 1:
                yield
        ds_ref[0] = ds[0]
        ds_ref[1] = ds[1]

    def body(it, carry):
        first = 2 * it
        _run_staggered(local_terms(first, loc_a) + [recurrence(jnp.maximum(first - 1, 0), loc_b)], 0)
        _run_staggered(local_terms(first + 1, loc_b) + [recurrence(first, loc_a)], 0)
        return carry

    lax.fori_loop(0, total // 2, body, 0)
    _run_staggered([recurrence(total - 1, loc_b)], 0)


def _wkv7_scan(r, v, kn, kd, cum, b, unroll=8, pairs=2):
    bsz, t, d = r.shape
    nc = t // CHUNK
    unroll = min(unroll, nc // 2)
    assert nc % (2 * unroll) == 0, "the scan kernel takes chunk groups in pairs"
    one = pl.BlockSpec((1, t, pairs * PAIR), lambda bb, p: (bb, 0, p))
    two = pl.BlockSpec((2, 1, t, pairs * PAIR), lambda bb, p: (0, bb, 0, p))
    out = jax.ShapeDtypeStruct((bsz, t, d), BF16)
    return pl.pallas_call(
        functools.partial(_scan_kernel, unroll=unroll),
        grid=(bsz, d // (pairs * PAIR)),
        in_specs=[one, one, one, two, two, two],
        out_specs=[one, one],
        out_shape=[out, out],
        scratch_shapes=[pltpu.VMEM((2, PAIR, PAIR), F32)]
                       + [pltpu.VMEM((2 * unroll, 2 * CHUNK + 2 * PAIR, PAIR), F32)] * 2,
        compiler_params=_cparams("parallel", "parallel"),
        name="l1_wkv7_scan",
    )(r, v, kn, kd, cum, b)


def _l1_tail_kernel(yf_ref, yb_ref, bonus_ref, gate_ref, h_ref, lg_ref, lb_ref, wo_ref, sel_ref, selt_ref,
                    gx_ref, wq_ref, kv_ref, wxo_ref, out_ref, *, ts):
    tq = h_ref.shape[1]
    inv_n = 1.0 / RWKV_HEAD_DIM

    def sub(r0):
        rs = slice(r0, r0 + ts)
        y = yf_ref[0, rs, :] + yb_ref[0, rs, :]
        mu = _dot(y, sel_ref[...])
        yield
        yc = y - _dot(mu, selt_ref[...]) * inv_n
        var = _dot(yc * yc, sel_ref[...])
        yield
        yn = yc * lax.rsqrt(_dot(var, selt_ref[...]) * inv_n + GN_EPS) * lg_ref[...] + lb_ref[...]
        h1 = h_ref[0, rs, :] + _dot((yn + bonus_ref[0, rs, :]) * gate_ref[0, rs, :], wo_ref[...])
        yield
        out_ref[0, rs, :] = yield from _xattn_rows(h1, gx_ref, wq_ref, kv_ref, wxo_ref)

    _run_staggered([sub(r0) for r0 in range(0, tq, ts)], 1)


def _l1_tail(yf, yb, bonus, gate, h, lg, lb, wo, sel, selt, gx, wq, kv, wxo, tq=1024, ts=256):
    bsz, t, d = h.shape
    tile = pl.BlockSpec((1, tq, d), lambda b, i: (b, i, 0))
    return pl.pallas_call(
        functools.partial(_l1_tail_kernel, ts=ts),
        grid=(bsz, t // tq),
        in_specs=[tile, tile, tile, tile, tile, _resident((1, d)), _resident((1, d)), _resident(wo.shape),
                  _resident(sel.shape), _resident(selt.shape), _resident((1, d)), _resident(wq.shape),
                  pl.BlockSpec((1, kv.shape[1], 2 * d), lambda b, i: (b, 0, 0)), _resident(wxo.shape)],
        out_specs=tile,
        out_shape=jax.ShapeDtypeStruct(h.shape, F32),
        compiler_params=_cparams("parallel", "parallel"),
        name="l1_tail",
    )(yf, yb, bonus, gate, h, lg, lb, wo, sel, selt, gx, wq, kv, wxo)


def _pad_lora_out(w2):
    zero = jnp.zeros_like(w2[0])
    return jnp.stack([jnp.concatenate([w2[0], zero], axis=0), jnp.concatenate([zero, w2[1]], axis=0)])


def kernel(x, mem, rel_bias_table, norm_mix, norm_xattn, norm_mem, norm_ffn, norm_final, ab_w_in, ab_w_out, conv_w, conv_b, conv_ln_g, conv_ln_b, diff_lq1, diff_lk1, diff_lq2, diff_lk2, diff_subln_g, rwkv_mu, rwkv_w_r, rwkv_w_k, rwkv_w_v, rwkv_w_o, rwkv_w0, rwkv_w1, rwkv_w2, rwkv_a0, rwkv_a1, rwkv_a2, rwkv_g1, rwkv_g2, rwkv_k_k, rwkv_k_a, rwkv_r_k, rwkv_ln_g, rwkv_ln_b, xattn_w_q, xattn_w_kv, xattn_w_o, ffn_w_up, ffn_w_down):
    bsz, t, d = x.shape
    n = bsz * t
    depth = norm_mix.shape[0]
    n_mem = mem.shape[1]
    cc = conv_w.shape[-1]
    qk = DIFF_HEADS * 2 * DIFF_HEAD_DIM
    vw = ab_w_in.shape[-1] - 2 * cc - 2 * qk
    bf = lambda w: w.astype(BF16)
    row = lambda w: w.reshape(1, -1)

    heads = d // RWKV_HEAD_DIM
    head_of = jnp.arange(d, dtype=jnp.int32) // RWKV_HEAD_DIM
    assert heads <= LANES
    sel = (head_of[:, None] == jnp.arange(LANES, dtype=jnp.int32)[None, :]).astype(BF16)
    selt = sel.T

    h = x.reshape(n, d)
    mem2 = mem.reshape(bsz * n_mem, d)
    for i in range(depth):
        j = i // 2
        kv = _norm_linear(mem2, row(norm_mem[i]), bf(xattn_w_kv[i]), BF16).reshape(bsz, n_mem, 2 * d)
        if i % 2 == 0:
            lam_init = 0.8 - 0.6 * math.exp(-0.3 * i)
            h3 = h.reshape(bsz, t, d)
            u, q, k, v = _inproj(h3, row(norm_mix[i]), bf(ab_w_in[j]), conv_w[j], row(conv_b[j]),
                                 row(conv_ln_g[j]), row(conv_ln_b[j]), cc, qk, vw)
            lq = jnp.stack([diff_lq1[j], diff_lk1[j], diff_lq2[j], diff_lk2[j]])
            o = _diff_attention(q, k, v, rel_bias_table.reshape(-1), lq, row(diff_subln_g[j]), lam_init)
            h = _l0_tail(u, o, h3, bf(ab_w_out[j]), row(norm_xattn[i]), bf(xattn_w_q[i]), kv,
                         bf(xattn_w_o[i])).reshape(n, d)
        else:
            h3 = h.reshape(bsz, t, d)
            w1 = bf(jnp.concatenate([rwkv_w1[j, 0], rwkv_w1[j, 1]], axis=1))
            a1 = bf(jnp.concatenate([rwkv_a1[j, 0], rwkv_a1[j, 1]], axis=1))
            r, v, kn, gate, bonus, kd, cum, b = _rwkv_prep(
                h3, row(norm_mix[i]), rwkv_mu[j], bf(rwkv_w_r[j]), bf(rwkv_w_k[j]), bf(rwkv_w_v[j]),
                w1, bf(_pad_lora_out(rwkv_w2[j])), rwkv_w0[j], a1, bf(_pad_lora_out(rwkv_a2[j])), rwkv_a0[j],
                bf(rwkv_g1[j]), bf(rwkv_g2[j]), row(rwkv_k_k[j]), row(rwkv_k_a[j]), row(rwkv_r_k[j]),
                sel, selt)
            yf, yb = _wkv7_scan(r, v, kn, kd, cum, b)
            h = _l1_tail(yf, yb, bonus, gate, h3, row(rwkv_ln_g[j]), row(rwkv_ln_b[j]), bf(rwkv_w_o[j]), sel, selt,
                         row(norm_xattn[i]), bf(xattn_w_q[i]), kv, bf(xattn_w_o[i])).reshape(n, d)
        h = _mlp(h, row(norm_ffn[i]), bf(ffn_w_up[i]), bf(ffn_w_down[i]), row(norm_final),
                 final_norm=(i == depth - 1))
    return h.reshape(bsz, t, d)
```

```python
import functools
import math

import jax
import jax.numpy as jnp
from jax import lax
from jax.experimental import pallas as pl
from jax.experimental.pallas import tpu as pltpu

F32 = jnp.float32
BF16 = jnp.bfloat16

V7X_VMEM_BYTES = 64 * 1024 * 1024
VMEM_LIMIT_BYTES = V7X_VMEM_BYTES - 8 * 1024 * 1024

LOG2E = math.log2(math.e)
NORM_EPS = 1e-6
CONV_LN_EPS = 1e-5
GN_EPS = 64e-5
CONV_WIDTH = 31
CONV_PAD = CONV_WIDTH // 2
CONV_HALO = 16
DIFF_HEADS = 4
DIFF_HEAD_DIM = 64
REL_BUCKETS = 32
REL_MAX_DIST = 128
XATTN_HEADS = 4
RWKV_HEAD_DIM = 64
SUBLANES = 8
LANES = 128
CHUNK = 64
CHUNK_SHIFT = CHUNK.bit_length() - 1
HEAD_SHIFT = RWKV_HEAD_DIM.bit_length() - 1
DIAG_SHIFT = 4
PAIR = 2 * RWKV_HEAD_DIM


def _cparams(*sem):
    return pltpu.CompilerParams(dimension_semantics=sem, vmem_limit_bytes=VMEM_LIMIT_BYTES)


def _resident(shape):
    nd = len(shape)
    return pl.BlockSpec(shape, lambda *_: (0,) * nd, pipeline_mode=pl.Buffered(1))


def _rms(x, g):
    ms = jnp.mean(x * x, axis=-1, keepdims=True)
    return x * lax.rsqrt(ms + NORM_EPS) * g


def _sigmoid(x):
    return 1.0 / (1.0 + jnp.exp(-x))


def _dot(a, b):
    return jnp.dot(a.astype(BF16), b.astype(BF16), preferred_element_type=F32)


def _dot_nt(a, b):
    return lax.dot_general(a.astype(BF16), b.astype(BF16), (((1,), (1,)), ((), ())),
                           preferred_element_type=F32)


def _split_dot(mat, x):
    hi = x.astype(BF16)
    lo = (x - hi.astype(F32)).astype(BF16)
    return (jnp.dot(mat, hi, preferred_element_type=F32) + jnp.dot(mat, lo, preferred_element_type=F32))


def _run_staggered(gens, offset):
    results = [None] * len(gens)
    running = [True] * len(gens)
    rnd = 0
    while any(running):
        for i, gen in enumerate(gens):
            if rnd >= i * offset and running[i]:
                try:
                    next(gen)
                except StopIteration as stop:
                    results[i] = stop.value
                    running[i] = False
        rnd += 1
    return results


def _inproj_kernel(h_ref, hp_ref, hn_ref, g_ref, w_ref, cw_ref, cb_ref, cg_ref, cbeta_ref,
                   u_ref, q_ref, k_ref, v_ref, xp_ref, *, cc, qk, scale, rows):
    i = pl.program_id(1)
    last = pl.num_programs(1) - 1
    tm = h_ref.shape[1]
    g = g_ref[...]
    x = _rms(h_ref[0], g).astype(BF16)
    x_ext = jnp.concatenate([_rms(hp_ref[0], g).astype(BF16), x, _rms(hn_ref[0], g).astype(BF16)], axis=0)
    pu = _dot(x_ext, w_ref[:, :2 * cc])
    u = pu[:, :cc] * _sigmoid(pu[:, cc:])
    xp_ref[0:CONV_HALO, :] = u[:CONV_HALO] * jnp.where(i > 0, 1.0, 0.0)
    xp_ref[CONV_HALO:CONV_HALO + tm, :] = u[CONV_HALO:CONV_HALO + tm]
    xp_ref[CONV_HALO + tm:, :] = u[CONV_HALO + tm:] * jnp.where(i < last, 1.0, 0.0)

    def conv():
        for t0 in range(0, tm, rows):
            u_ref[0, t0:t0 + rows, :] = _conv_rows(xp_ref, t0, rows, cw_ref, cb_ref, cg_ref, cbeta_ref)
            yield

    def qkv():
        o = 2 * cc
        q_ref[0] = (_dot(x, w_ref[:, o:o + qk]) * scale).astype(BF16)
        yield
        k_ref[0] = _dot(x, w_ref[:, o + qk:o + 2 * qk]).astype(BF16)
        yield
        v_ref[0] = _dot(x, w_ref[:, o + 2 * qk:]).astype(BF16)

    _run_staggered([conv(), qkv()], 0)


def _inproj(h, g, w, cw, cb, cbg, cbeta, cc, qk, vw, tm=512, rows=64):
    bsz, t, d = h.shape
    nb = tm // CONV_HALO
    tile = lambda width: pl.BlockSpec((1, tm, width), lambda b, i: (b, i, 0))
    out = lambda width: jax.ShapeDtypeStruct((bsz, t, width), BF16)
    return pl.pallas_call(
        functools.partial(_inproj_kernel, cc=cc, qk=qk, scale=DIFF_HEAD_DIM ** -0.5 * LOG2E, rows=rows),
        grid=(bsz, t // tm),
        in_specs=[tile(d),
                  pl.BlockSpec((1, CONV_HALO, d), lambda b, i: (b, jnp.maximum(i * nb - 1, 0), 0)),
                  pl.BlockSpec((1, CONV_HALO, d),
                               lambda b, i: (b, jnp.minimum((i + 1) * nb, t // CONV_HALO - 1), 0)),
                  _resident((1, d)), _resident(w.shape), _resident(cw.shape), _resident((1, cc)),
                  _resident((1, cc)), _resident((1, cc))],
        out_specs=[tile(cc), tile(qk), tile(qk), tile(vw)],
        out_shape=[out(cc), out(qk), out(qk), out(vw)],
        scratch_shapes=[pltpu.VMEM((tm + 2 * CONV_HALO, cc), F32)],
        compiler_params=_cparams("parallel", "parallel"),
        name="l0_inproj",
    )(h, h, h, g, w, cw, cb, cbg, cbeta)


def _conv_rows(xp_ref, base, rows, w_ref, b_ref, g_ref, beta_ref):
    c = xp_ref.shape[-1]
    off = CONV_HALO - CONV_PAD
    span = rows + 2 * CONV_HALO
    pieces = []
    for c0 in range(0, c, LANES):
        win = xp_ref[base:base + span, c0:c0 + LANES]
        acc = jnp.zeros((rows, LANES), F32)
        for rem in range(SUBLANES):
            shifted = win if rem == 0 else pltpu.roll(win, span - rem, 0)
            for start in range(0, 2 * CONV_HALO, SUBLANES):
                k = start + rem - off
                if 0 <= k < CONV_WIDTH:
                    acc = acc + shifted[start:start + rows, :] * w_ref[k:k + 1, c0:c0 + LANES]
        pieces.append(acc)
    y = jnp.concatenate(pieces, axis=1) + b_ref[...]
    mu = jnp.mean(y, axis=-1, keepdims=True)
    yc = y - mu
    var = jnp.mean(yc * yc, axis=-1, keepdims=True)
    yn = yc * lax.rsqrt(var + CONV_LN_EPS) * g_ref[...] + beta_ref[...]
    return (yn * _sigmoid(yn)).astype(BF16)


def _t5_bucket(rel):
    nb = REL_BUCKETS // 2
    max_exact = nb // 2
    n = jnp.abs(rel)
    large = jnp.full(rel.shape, max_exact, jnp.int32)
    steps = nb - max_exact
    for m in range(1, steps):
        thr = math.ceil(max_exact * (REL_MAX_DIST / max_exact) ** (m / steps) - 1e-9)
        large = large + jnp.where(n >= thr, 1, 0)
    mag = jnp.where(n < max_exact, n, large)
    return mag + jnp.where(rel > 0, nb, 0)


def _diffattn_kernel(tbl_ref, lq_ref, sg_ref, q_ref, k_ref, v_ref, o_ref, bias_ref, *, tq, ts, kb, lam_init):
    h = pl.program_id(0)
    qi = pl.program_id(1)
    b = pl.program_id(2)
    t = k_ref.shape[1]

    @pl.when(b == 0)
    def _():
        u = lax.broadcasted_iota(jnp.int32, (1, t + tq), 1)
        bucket = _t5_bucket(u - (tq - 1) - qi * tq)
        line = jnp.zeros((1, t + tq), F32)
        for i in range(REL_BUCKETS):
            line = jnp.where(bucket == i, tbl_ref[i * DIFF_HEADS + h], line)
        rows = pltpu.roll(jnp.broadcast_to(line * LOG2E, (tq, t + tq)), 1, 1, stride=1, stride_axis=0)
        bias_ref[...] = rows[:, tq:]

    lq = lq_ref[...]
    lam = (jnp.exp(jnp.sum(lq[0:1] * lq[1:2], axis=-1, keepdims=True))
           - jnp.exp(jnp.sum(lq[2:3] * lq[3:4], axis=-1, keepdims=True)) + lam_init)
    hw = q_ref.shape[-1]
    first = lax.broadcasted_iota(jnp.int32, (ts, hw), 1) < DIFF_HEAD_DIM
    nkb = t // kb

    def softmax_v(r0, comp):
        q = q_ref[0, r0:r0 + ts, :]
        qc = jnp.where(first, q, jnp.zeros_like(q)) if comp == 0 else jnp.where(first, jnp.zeros_like(q), q)
        s = []
        mx = None
        for j in range(nkb):
            sj = _dot_nt(qc, k_ref[0, j * kb:(j + 1) * kb, :]) + bias_ref[r0:r0 + ts, j * kb:(j + 1) * kb]
            mj = jnp.max(sj, axis=-1, keepdims=True)
            mx = mj if mx is None else jnp.maximum(mx, mj)
            s.append(sj)
            yield
        pv = None
        for j in range(nkb):
            vj = v_ref[0, j * kb:(j + 1) * kb, :]
            dj = _dot(jnp.exp2(s[j] - mx), jnp.concatenate([vj, jnp.ones_like(vj)], axis=1))
            pv = dj if pv is None else pv + dj
            yield
        return pv[:, :hw] / pv[:, hw:]

    starts = range(0, tq, ts)
    maps = _run_staggered([softmax_v(r0, comp) for r0 in starts for comp in range(2)], nkb)
    for i, r0 in enumerate(starts):
        o = maps[2 * i] - lam * maps[2 * i + 1]
        o = o * lax.rsqrt(jnp.mean(o * o, axis=-1, keepdims=True) + NORM_EPS) * sg_ref[...] * (1.0 - lam_init)
        o_ref[0, r0:r0 + ts, :] = o.astype(BF16)


def _diff_attention(q, k, v, tbl, lq, sg, lam_init, tq=2048, ts=256, kb=256):
    bsz, t, _ = q.shape
    hw = 2 * DIFF_HEAD_DIM
    return pl.pallas_call(
        functools.partial(_diffattn_kernel, tq=tq, ts=ts, kb=kb, lam_init=lam_init),
        grid=(DIFF_HEADS, t // tq, bsz),
        in_specs=[pl.BlockSpec(memory_space=pltpu.SMEM), _resident(lq.shape), _resident(sg.shape),
                  pl.BlockSpec((1, tq, hw), lambda h, i, b: (b, i, h)),
                  pl.BlockSpec((1, t, hw), lambda h, i, b: (b, 0, h)),
                  pl.BlockSpec((1, t, hw), lambda h, i, b: (b, 0, h))],
        out_specs=pl.BlockSpec((1, tq, hw), lambda h, i, b: (b, i, h)),
        out_shape=jax.ShapeDtypeStruct(q.shape, BF16),
        scratch_shapes=[pltpu.VMEM((tq, t), F32)],
        compiler_params=_cparams("parallel", "parallel", "arbitrary"),
        name="l0_diffattn",
    )(tbl, lq, sg, q, k, v)


def _norm_linear_kernel(x_ref, g_ref, w_ref, o_ref):
    o_ref[...] = _dot(_rms(x_ref[...], g_ref[...]), w_ref[...]).astype(o_ref.dtype)


def _norm_linear(x, g, w, out_dtype, tm=512):
    n, d = x.shape
    m = w.shape[1]
    return pl.pallas_call(
        _norm_linear_kernel,
        grid=(n // tm,),
        in_specs=[pl.BlockSpec((tm, d), lambda i: (i, 0)), _resident((1, d)), _resident(w.shape)],
        out_specs=pl.BlockSpec((tm, m), lambda i: (i, 0)),
        out_shape=jax.ShapeDtypeStruct((n, m), out_dtype),
        compiler_params=_cparams("parallel"),
        name="norm_linear",
    )(x, g, w)


def _xattn_rows(h, g_ref, wq_ref, kv_ref, wo_ref):
    d = h.shape[-1]
    hd = d // XATTN_HEADS
    q = (_dot(_rms(h, g_ref[...]), wq_ref[...]) * (hd ** -0.5 * LOG2E)).astype(BF16)
    yield
    outs = []
    for i in range(XATTN_HEADS):
        kh = kv_ref[0, :, i * hd:(i + 1) * hd]
        vh = kv_ref[0, :, d + i * hd:d + (i + 1) * hd]
        s = _dot_nt(q[:, i * hd:(i + 1) * hd], kh)
        e = jnp.exp2(s - jnp.max(s, axis=-1, keepdims=True))
        p = e / jnp.sum(e, axis=-1, keepdims=True)
        outs.append(_dot(p, vh).astype(BF16))
    yield
    return h + _dot(jnp.concatenate(outs, axis=1), wo_ref[...])


def _l0_tail_kernel(u_ref, o_ref, h_ref, wout_ref, gx_ref, wq_ref, kv_ref, wo_ref, out_ref, *, ts):
    tq = h_ref.shape[1]
    cc = u_ref.shape[-1]

    def sub(r0):
        rs = slice(r0, r0 + ts)
        h1 = h_ref[0, rs, :] + _dot(u_ref[0, rs, :], wout_ref[:cc, :]) + _dot(o_ref[0, rs, :], wout_ref[cc:, :])
        yield
        out_ref[0, rs, :] = yield from _xattn_rows(h1, gx_ref, wq_ref, kv_ref, wo_ref)

    _run_staggered([sub(r0) for r0 in range(0, tq, ts)], 1)


def _l0_tail(u, o, h, wout, gx, wq, kv, wo, tq=1024, ts=256):
    bsz, t, d = h.shape
    tile = lambda w: pl.BlockSpec((1, tq, w), lambda b, i: (b, i, 0))
    return pl.pallas_call(
        functools.partial(_l0_tail_kernel, ts=ts),
        grid=(bsz, t // tq),
        in_specs=[tile(u.shape[-1]), tile(o.shape[-1]), tile(d), _resident(wout.shape), _resident((1, d)),
                  _resident(wq.shape), pl.BlockSpec((1, kv.shape[1], 2 * d), lambda b, i: (b, 0, 0)),
                  _resident(wo.shape)],
        out_specs=tile(d),
        out_shape=jax.ShapeDtypeStruct(h.shape, F32),
        compiler_params=_cparams("parallel", "parallel"),
        name="l0_tail",
    )(u, o, h, wout, gx, wq, kv, wo)


def _mlp_kernel(h_ref, g_ref, wu_ref, wd_ref, gf_ref, o_ref, *, hc, final_norm):
    h = h_ref[...]
    xn = _rms(h, g_ref[...]).astype(BF16)
    acc = h
    for c in range(wu_ref.shape[1] // hc):
        a = jnp.maximum(_dot(xn, wu_ref[:, c * hc:(c + 1) * hc]), 0.0)
        acc = acc + _dot(a * a, wd_ref[c * hc:(c + 1) * hc, :])
    if final_norm:
        acc = _rms(acc, gf_ref[...])
    o_ref[...] = acc


def _mlp(h, g, wu, wd, gf, final_norm, tm=512, hc=1024):
    n, d = h.shape
    row = lambda i: (i, 0)
    return pl.pallas_call(
        functools.partial(_mlp_kernel, hc=hc, final_norm=final_norm),
        grid=(n // tm,),
        in_specs=[pl.BlockSpec((tm, d), row), _resident((1, d)), _resident(wu.shape), _resident(wd.shape),
                  _resident((1, d))],
        out_specs=pl.BlockSpec((tm, d), row),
        out_shape=jax.ShapeDtypeStruct((n, d), F32),
        compiler_params=_cparams("parallel"),
        name="mlp",
    )(h, g, wu, wd, gf)


def _rwkv_prep_kernel(h_ref, hp_ref, hn_ref, g_ref, mu_ref, wr_ref, wk_ref, wv_ref, w1_ref, w2_ref, w0_ref,
                      a1_ref, a2_ref, a0_ref, g1_ref, g2_ref, kk_ref, ka_ref, rk_ref, sel_ref, selt_ref,
                      r_out, v_out, kn_out, gate_out, bonus_out, kd_out, cum_out, b_out, *, ts):
    i = pl.program_id(1)
    last = pl.num_programs(1) - 1
    tm = h_ref.shape[1]
    g = g_ref[...]
    ti = lax.broadcasted_iota(jnp.int32, (ts, ts), 0)
    si = lax.broadcasted_iota(jnp.int32, (ts, ts), 1)
    same_chunk = (ti >> CHUNK_SHIFT) == (si >> CHUNK_SHIFT)
    before = (jnp.where(same_chunk, jnp.where(si <= ti, 1.0, 0.0), 0.0).astype(BF16),
              jnp.where(same_chunk, jnp.where(si >= ti, 1.0, 0.0), 0.0).astype(BF16))

    def rows(r0):
        x = _rms(h_ref[0, r0:r0 + ts, :], g)
        if r0 == 0:
            prev_row = _rms(hp_ref[0], g)[SUBLANES - 1:, :] * jnp.where(i > 0, 1.0, 0.0)
        else:
            prev_row = _rms(h_ref[0, r0 - SUBLANES:r0, :], g)[SUBLANES - 1:, :]
        if r0 + ts == tm:
            next_row = _rms(hn_ref[0], g)[0:1, :] * jnp.where(i < last, 1.0, 0.0)
        else:
            next_row = _rms(h_ref[0, r0 + ts:r0 + ts + SUBLANES, :], g)[0:1, :]
        rowid = lax.broadcasted_iota(jnp.int32, x.shape, 0)
        x_prev = jnp.where(rowid == 0, prev_row, pltpu.roll(x, 1, 0))
        x_next = jnp.where(rowid == ts - 1, next_row, pltpu.roll(x, ts - 1, 0))
        hh = 0.5 * (x_prev + x_next) - x
        mix = lambda j: x + hh * mu_ref[j:j + 1, :]
        r = _dot(mix(0), wr_ref[...])
        k = _dot(mix(2), wk_ref[...])
        v = _dot(mix(3), wv_ref[...])
        yield
        gate_in = _dot(mix(5), g1_ref[...])
        lw = _dot(mix(1), w1_ref[...])
        la = _dot(mix(4), a1_ref[...])
        yield
        gate = _dot(_sigmoid(gate_in), g2_ref[...])
        lw = jnp.tanh(lw)
        w_pre = [w0_ref[z:z + 1, :] + _dot(lw, w2_ref[z]) for z in range(2)]
        a_pre = [a0_ref[z:z + 1, :] + _dot(la, a2_ref[z]) for z in range(2)]
        kk = k * kk_ref[...]
        ss = _dot(kk * kk, sel_ref[...])
        yield
        kn = kk * lax.rsqrt(jnp.maximum(_dot(ss, selt_ref[...]), 1e-24))
        kka = k * ka_ref[...]
        kd_sum = jnp.zeros_like(k)
        cum = []
        for z in range(2):
            cum.append(_split_dot(before[z], _sigmoid(w_pre[z]) * (-math.exp(-0.5))))
            rate = _sigmoid(a_pre[z])
            kd = k + kka * (rate - 1.0)
            kd_out[z, 0, r0:r0 + ts, :] = kd.astype(kd_out.dtype)
            b_out[z, 0, r0:r0 + ts, :] = (kn * rate).astype(b_out.dtype)
            kd_sum = kd_sum + kd
        bs = _dot(r * kd_sum * rk_ref[...], sel_ref[...])
        yield
        r_out[0, r0:r0 + ts, :] = r.astype(r_out.dtype)
        v_out[0, r0:r0 + ts, :] = v.astype(v_out.dtype)
        kn_out[0, r0:r0 + ts, :] = kn.astype(kn_out.dtype)
        gate_out[0, r0:r0 + ts, :] = gate.astype(gate_out.dtype)
        bonus_out[0, r0:r0 + ts, :] = (_dot(bs, selt_ref[...]) * v).astype(bonus_out.dtype)
        for z in range(2):
            cum_out[z, 0, r0:r0 + ts, :] = cum[z]

    _run_staggered([rows(r0) for r0 in range(0, tm, ts)], 1)


def _rwkv_prep(h, g, mu, wr, wk, wv, w1, w2, w0, a1, a2, a0, g1, g2, kk, ka, rk, sel, selt, tm=512, ts=256):
    bsz, t, d = h.shape
    nb = tm // SUBLANES
    tile = pl.BlockSpec((1, tm, d), lambda b, i: (b, i, 0))
    tile2 = pl.BlockSpec((2, 1, tm, d), lambda b, i: (0, b, i, 0))
    one = jax.ShapeDtypeStruct((bsz, t, d), BF16)
    two = jax.ShapeDtypeStruct((2, bsz, t, d), BF16)
    consts = [g, mu, wr, wk, wv, w1, w2, w0, a1, a2, a0, g1, g2, kk, ka, rk, sel, selt]
    return pl.pallas_call(
        functools.partial(_rwkv_prep_kernel, ts=ts),
        grid=(bsz, t // tm),
        in_specs=[tile,
                  pl.BlockSpec((1, SUBLANES, d), lambda b, i: (b, jnp.maximum(i * nb - 1, 0), 0)),
                  pl.BlockSpec((1, SUBLANES, d),
                               lambda b, i: (b, jnp.minimum((i + 1) * nb, t // SUBLANES - 1), 0))]
                 + [_resident(c.shape) for c in consts],
        out_specs=[tile, tile, tile, tile, tile, tile2, tile2, tile2],
        out_shape=[one, one, one, one, one, two, jax.ShapeDtypeStruct((2, bsz, t, d), F32), two],
        compiler_params=_cparams("parallel", "parallel"),
        name="l1_rwkv_prep",
    )(h, h, h, *consts)


def _blockdiag(x):
    lane = lax.broadcasted_iota(jnp.int32, x.shape, 1)
    head0 = (lane & (PAIR - 1)) < RWKV_HEAD_DIM
    zero = jnp.zeros_like(x)
    return jnp.concatenate([jnp.where(head0, x, zero), jnp.where(head0, zero, x)], axis=0)


def _chunk_local(r, k, v, kn, cum, b, rev):
    c = CHUNK
    bd = _blockdiag
    row = lax.broadcasted_iota(jnp.int32, (c, PAIR), 0)
    lane = lax.broadcasted_iota(jnp.int32, (c, PAIR), 1)
    s_idx = lane & (c - 1)
    if rev:
        cum_prev = jnp.where(row == c - 1, 0.0, pltpu.roll(cum, c - 1, 0))
        tot = cum[0:1, :]
        strict = s_idx > row
        incl = s_idx >= row
    else:
        cum_prev = jnp.where(row == 0, 0.0, pltpu.roll(cum, 1, 0))
        tot = cum[c - 1:c, :]
        strict = s_idx < row
        incl = s_idx <= row
    w_incl = jnp.exp(cum)
    w_excl = jnp.exp(cum_prev)
    w_inv = jnp.exp(-cum)
    w_tot = jnp.exp(tot)
    w_rest = jnp.exp(tot - cum)
    a_t = -kn * w_excl
    r_t = r * w_incl
    b_t = b * w_inv
    k_t = k * w_inv
    b_h = b * w_rest
    k_h = k * w_rest
    same_blk = (s_idx >> DIAG_SHIFT) == (row >> DIAG_SHIFT)

    sc = _dot_nt(jnp.concatenate([a_t, r_t], axis=0), jnp.concatenate([bd(b_t), bd(k_t)], axis=0))
    yield
    p_ab = jnp.where(strict, sc[:c, :PAIR], 0.0)
    p_ak = jnp.where(strict, sc[:c, PAIR:], 0.0)
    p_rb = jnp.where(incl, sc[c:, :PAIR], 0.0)
    p_rk = jnp.where(incl, sc[c:, PAIR:], 0.0)
    dm = jnp.where(same_blk, p_ab, 0.0)
    em = p_ab - dm
    x2 = _dot(dm, bd(dm))
    av = _dot(p_ak, bd(v))
    yield
    td = jnp.where(s_idx == row, 1.0, 0.0) + dm
    both = _dot(jnp.concatenate([x2, td], axis=0), bd(x2))
    yield
    x4 = both[:c]
    td = td + both[c:]
    both = _dot(jnp.concatenate([x4, td], axis=0), bd(x4))
    yield
    td = td + both[c:]
    td = td + _dot(td, bd(both[:c]))
    yield
    ty = _dot(td, bd(jnp.concatenate([a_t, av, em], axis=1)))
    yield
    au, f1 = ty[:, :2 * PAIR], ty[:, 2 * PAIR:]
    both = _dot(f1, bd(jnp.concatenate([f1, au], axis=1)))
    f2 = both[:, :PAIR]
    au = au + both[:, PAIR:]
    yield
    au = au + _dot(f2, bd(au))
    yield
    rhs = jnp.concatenate([bd(au), jnp.concatenate([jnp.zeros((PAIR, PAIR), F32), bd(v)], axis=1)], axis=0)
    ry = _dot(jnp.concatenate([p_rb, p_rk], axis=1), rhs)
    rhs2 = jnp.concatenate([au, jnp.concatenate([jnp.zeros((c, PAIR), F32), v], axis=1)], axis=0)
    mg = _dot(jnp.concatenate([b_h, k_h], axis=0).T, rhs2)
    yield
    rbar = r_t + ry[:, :PAIR]
    yloc = ry[:, PAIR:]
    r2 = lax.broadcasted_iota(jnp.int32, (PAIR, PAIR), 0)
    l2 = lax.broadcasted_iota(jnp.int32, (PAIR, PAIR), 1)
    same_head = (r2 >> HEAD_SHIFT) == (l2 >> HEAD_SHIFT)
    m = jnp.where(same_head, mg[:, :PAIR], 0.0) + jnp.where(r2 == l2, w_tot, 0.0)
    gg = jnp.where(same_head, mg[:, PAIR:], 0.0)
    return rbar, yloc, m, gg


def _scan_kernel(r_ref, v_ref, kn_ref, kd_ref, cum_ref, b_ref, yf_ref, yb_ref, ds_ref, loc_a, loc_b, *, unroll):
    nc = r_ref.shape[1] // CHUNK
    groups = nc // unroll
    total = (r_ref.shape[2] // PAIR) * groups
    y_refs = (yf_ref, yb_ref)
    c = CHUNK
    ds_ref[...] = jnp.zeros(ds_ref.shape, F32)
    loc_b[...] = jnp.zeros(loc_b.shape, F32)

    def places(gidx):
        grp = gidx % groups
        lanes = pl.ds(pl.multiple_of((gidx // groups) * PAIR, PAIR), PAIR)
        out = []
        for u in range(unroll):
            for z in range(2):
                cidx = grp * unroll + u
                if z == 1:
                    cidx = nc - 1 - cidx
                out.append((z, pl.ds(pl.multiple_of(cidx * c, c), c), lanes))
        return out

    def local_terms(gidx, loc_ref):
        f32 = lambda ref, *idx: ref[idx].astype(F32)
        gens = [_chunk_local(f32(r_ref, 0, sl, ln), f32(kd_ref, z, 0, sl, ln), f32(v_ref, 0, sl, ln),
                             f32(kn_ref, 0, sl, ln), cum_ref[z, 0, sl, ln], f32(b_ref, z, 0, sl, ln),
                             rev=(z == 1))
                for z, sl, ln in places(gidx)]

        def park(j, gen):
            rbar, yloc, m, gg = yield from gen
            loc_ref[j, 0:c, :] = rbar
            loc_ref[j, c:2 * c, :] = yloc
            loc_ref[j, 2 * c:2 * c + PAIR, :] = m
            loc_ref[j, 2 * c + PAIR:, :] = gg

        return [park(j, gen) for j, gen in enumerate(gens)]

    def recurrence(gidx, loc_ref):
        keep = jnp.where(gidx % groups == 0, 0.0, 1.0)
        ds = [ds_ref[0] * keep, ds_ref[1] * keep]
        for j, (z, sl, ln) in enumerate(places(gidx)):
            both = _dot(jnp.concatenate([loc_ref[j, 0:c, :], loc_ref[j, 2 * c:2 * c + PAIR, :]], axis=0), ds[z])
            y_refs[z][0, sl, ln] = (both[:c] + loc_ref[j, c:2 * c, :]).astype(y_refs[z].dtype)
            ds[z] = both[c:] + loc_ref[j, 2 * c + PAIR:, :]
            if z == 1:
                yield
        ds_ref[0] = ds[0]
        ds_ref[1] = ds[1]

    def body(it, carry):
        first = 2 * it
        _run_staggered(local_terms(first, loc_a) + [recurrence(jnp.maximum(first - 1, 0), loc_b)], 0)
        _run_staggered(local_terms(first + 1, loc_b) + [recurrence(first, loc_a)], 0)
        return carry

    lax.fori_loop(0, total // 2, body, 0)
    _run_staggered([recurrence(total - 1, loc_b)], 0)


def _wkv7_scan(r, v, kn, kd, cum, b, unroll=8, pairs=2):
    bsz, t, d = r.shape
    nc = t // CHUNK
    unroll = min(unroll, nc // 2)
    assert nc % (2 * unroll) == 0, "the scan kernel takes chunk groups in pairs"
    one = pl.BlockSpec((1, t, pairs * PAIR), lambda bb, p: (bb, 0, p))
    two = pl.BlockSpec((2, 1, t, pairs * PAIR), lambda bb, p: (0, bb, 0, p))
    out = jax.ShapeDtypeStruct((bsz, t, d), BF16)
    return pl.pallas_call(
        functools.partial(_scan_kernel, unroll=unroll),
        grid=(bsz, d // (pairs * PAIR)),
        in_specs=[one, one, one, two, two, two],
        out_specs=[one, one],
        out_shape=[out, out],
        scratch_shapes=[pltpu.VMEM((2, PAIR, PAIR), F32)]
                       + [pltpu.VMEM((2 * unroll, 2 * CHUNK + 2 * PAIR, PAIR), F32)] * 2,
        compiler_params=_cparams("parallel", "parallel"),
        name="l1_wkv7_scan",
    )(r, v, kn, kd, cum, b)


def _l1_tail_kernel(yf_ref, yb_ref, bonus_ref, gate_ref, h_ref, lg_ref, lb_ref, wo_ref, sel_ref, selt_ref,
                    gx_ref, wq_ref, kv_ref, wxo_ref, out_ref, *, ts):
    tq = h_ref.shape[1]
    inv_n = 1.0 / RWKV_HEAD_DIM

    def sub(r0):
        rs = slice(r0, r0 + ts)
        y = yf_ref[0, rs, :] + yb_ref[0, rs, :]
        mu = _dot(y, sel_ref[...])
        yield
        yc = y - _dot(mu, selt_ref[...]) * inv_n
        var = _dot(yc * yc, sel_ref[...])
        yield
        yn = yc * lax.rsqrt(_dot(var, selt_ref[...]) * inv_n + GN_EPS) * lg_ref[...] + lb_ref[...]
        h1 = h_ref[0, rs, :] + _dot((yn + bonus_ref[0, rs, :]) * gate_ref[0, rs, :], wo_ref[...])
        yield
        out_ref[0, rs, :] = yield from _xattn_rows(h1, gx_ref, wq_ref, kv_ref, wxo_ref)

    _run_staggered([sub(r0) for r0 in range(0, tq, ts)], 1)


def _l1_tail(yf, yb, bonus, gate, h, lg, lb, wo, sel, selt, gx, wq, kv, wxo, tq=1024, ts=256):
    bsz, t, d = h.shape
    tile = pl.BlockSpec((1, tq, d), lambda b, i: (b, i, 0))
    return pl.pallas_call(
        functools.partial(_l1_tail_kernel, ts=ts),
        grid=(bsz, t // tq),
        in_specs=[tile, tile, tile, tile, tile, _resident((1, d)), _resident((1, d)), _resident(wo.shape),
                  _resident(sel.shape), _resident(selt.shape), _resident((1, d)), _resident(wq.shape),
                  pl.BlockSpec((1, kv.shape[1], 2 * d), lambda b, i: (b, 0, 0)), _resident(wxo.shape)],
        out_specs=tile,
        out_shape=jax.ShapeDtypeStruct(h.shape, F32),
        compiler_params=_cparams("parallel", "parallel"),
        name="l1_tail",
    )(yf, yb, bonus, gate, h, lg, lb, wo, sel, selt, gx, wq, kv, wxo)


def _pad_lora_out(w2):
    zero = jnp.zeros_like(w2[0])
    return jnp.stack([jnp.concatenate([w2[0], zero], axis=0), jnp.concatenate([zero, w2[1]], axis=0)])


def kernel(x, mem, rel_bias_table, norm_mix, norm_xattn, norm_mem, norm_ffn, norm_final, ab_w_in, ab_w_out, conv_w, conv_b, conv_ln_g, conv_ln_b, diff_lq1, diff_lk1, diff_lq2, diff_lk2, diff_subln_g, rwkv_mu, rwkv_w_r, rwkv_w_k, rwkv_w_v, rwkv_w_o, rwkv_w0, rwkv_w1, rwkv_w2, rwkv_a0, rwkv_a1, rwkv_a2, rwkv_g1, rwkv_g2, rwkv_k_k, rwkv_k_a, rwkv_r_k, rwkv_ln_g, rwkv_ln_b, xattn_w_q, xattn_w_kv, xattn_w_o, ffn_w_up, ffn_w_down):
    bsz, t, d = x.shape
    n = bsz * t
    depth = norm_mix.shape[0]
    n_mem = mem.shape[1]
    cc = conv_w.shape[-1]
    qk = DIFF_HEADS * 2 * DIFF_HEAD_DIM
    vw = ab_w_in.shape[-1] - 2 * cc - 2 * qk
    bf = lambda w: w.astype(BF16)
    row = lambda w: w.reshape(1, -1)

    heads = d // RWKV_HEAD_DIM
    head_of = jnp.arange(d, dtype=jnp.int32) // RWKV_HEAD_DIM
    assert heads <= LANES
    sel = (head_of[:, None] == jnp.arange(LANES, dtype=jnp.int32)[None, :]).astype(BF16)
    selt = sel.T

    h = x.reshape(n, d)
    mem2 = mem.reshape(bsz * n_mem, d)
    for i in range(depth):
        j = i // 2
        kv = _norm_linear(mem2, row(norm_mem[i]), bf(xattn_w_kv[i]), BF16).reshape(bsz, n_mem, 2 * d)
        if i % 2 == 0:
            lam_init = 0.8 - 0.6 * math.exp(-0.3 * i)
            h3 = h.reshape(bsz, t, d)
            u, q, k, v = _inproj(h3, row(norm_mix[i]), bf(ab_w_in[j]), conv_w[j], row(conv_b[j]),
                                 row(conv_ln_g[j]), row(conv_ln_b[j]), cc, qk, vw)
            lq = jnp.stack([diff_lq1[j], diff_lk1[j], diff_lq2[j], diff_lk2[j]])
            o = _diff_attention(q, k, v, rel_bias_table.reshape(-1), lq, row(diff_subln_g[j]), lam_init)
            h = _l0_tail(u, o, h3, bf(ab_w_out[j]), row(norm_xattn[i]), bf(xattn_w_q[i]), kv,
                         bf(xattn_w_o[i])).reshape(n, d)
        else:
            h3 = h.reshape(bsz, t, d)
            w1 = bf(jnp.concatenate([rwkv_w1[j, 0], rwkv_w1[j, 1]], axis=1))
            a1 = bf(jnp.concatenate([rwkv_a1[j, 0], rwkv_a1[j, 1]], axis=1))
            r, v, kn, gate, bonus, kd, cum, b = _rwkv_prep(
                h3, row(norm_mix[i]), rwkv_mu[j], bf(rwkv_w_r[j]), bf(rwkv_w_k[j]), bf(rwkv_w_v[j]),
                w1, bf(_pad_lora_out(rwkv_w2[j])), rwkv_w0[j], a1, bf(_pad_lora_out(rwkv_a2[j])), rwkv_a0[j],
                bf(rwkv_g1[j]), bf(rwkv_g2[j]), row(rwkv_k_k[j]), row(rwkv_k_a[j]), row(rwkv_r_k[j]),
                sel, selt)
            yf, yb = _wkv7_scan(r, v, kn, kd, cum, b)
            h = _l1_tail(yf, yb, bonus, gate, h3, row(rwkv_ln_g[j]), row(rwkv_ln_b[j]), bf(rwkv_w_o[j]), sel, selt,
                         row(norm_xattn[i]), bf(xattn_w_q[i]), kv, bf(xattn_w_o[i])).reshape(n, d)
        h = _mlp(h, row(norm_ffn[i]), bf(ffn_w_up[i]), bf(ffn_w_down[i]), row(norm_final),
                 final_norm=(i == depth - 1))
    return h.reshape(bsz, t, d)
```

```python
import functools
import math

import jax
import jax.numpy as jnp
from jax import lax
from jax.experimental import pallas as pl
from jax.experimental.pallas import tpu as pltpu

F32 = jnp.float32
BF16 = jnp.bfloat16

V7X_VMEM_BYTES = 64 * 1024 * 1024
VMEM_LIMIT_BYTES = V7X_VMEM_BYTES - 8 * 1024 * 1024

LOG2E = math.log2(math.e)
NORM_EPS = 1e-6
CONV_LN_EPS = 1e-5
GN_EPS = 64e-5
CONV_WIDTH = 31
CONV_PAD = CONV_WIDTH // 2
CONV_HALO = 16
DIFF_HEADS = 4
DIFF_HEAD_DIM = 64
REL_BUCKETS = 32
REL_MAX_DIST = 128
XATTN_HEADS = 4
RWKV_HEAD_DIM = 64
SUBLANES = 8
LANES = 128
CHUNK = 64
CHUNK_SHIFT = CHUNK.bit_length() - 1
HEAD_SHIFT = RWKV_HEAD_DIM.bit_length() - 1
DIAG_SHIFT = 4
PAIR = 2 * RWKV_HEAD_DIM


def _cparams(*sem):
    return pltpu.CompilerParams(dimension_semantics=sem, vmem_limit_bytes=VMEM_LIMIT_BYTES)


def _resident(shape):
    nd = len(shape)
    return pl.BlockSpec(shape, lambda *_: (0,) * nd, pipeline_mode=pl.Buffered(1))


def _rms(x, g):
    ms = jnp.mean(x * x, axis=-1, keepdims=True)
    return x * lax.rsqrt(ms + NORM_EPS) * g


def _sigmoid(x):
    return 1.0 / (1.0 + jnp.exp(-x))


def _dot(a, b):
    return jnp.dot(a.astype(BF16), b.astype(BF16), preferred_element_type=F32)


def _dot_nt(a, b):
    return lax.dot_general(a.astype(BF16), b.astype(BF16), (((1,), (1,)), ((), ())),
                           preferred_element_type=F32)


def _split_dot(mat, x):
    hi = x.astype(BF16)
    lo = (x - hi.astype(F32)).astype(BF16)
    return (jnp.dot(mat, hi, preferred_element_type=F32) + jnp.dot(mat, lo, preferred_element_type=F32))


def _run_staggered(gens, offset):
    results = [None] * len(gens)
    running = [True] * len(gens)
    rnd = 0
    while any(running):
        for i, gen in enumerate(gens):
            if rnd >= i * offset and running[i]:
                try:
                    next(gen)
                except StopIteration as stop:
                    results[i] = stop.value
                    running[i] = False
        rnd += 1
    return results


def _inproj_kernel(h_ref, hp_ref, hn_ref, g_ref, w_ref, cw_ref, cb_ref, cg_ref, cbeta_ref,
                   u_ref, q_ref, k_ref, v_ref, xp_ref, *, cc, qk, scale, rows):
    i = pl.program_id(1)
    last = pl.num_programs(1) - 1
    tm = h_ref.shape[1]
    g = g_ref[...]
    x = _rms(h_ref[0], g).astype(BF16)
    x_ext = jnp.concatenate([_rms(hp_ref[0], g).astype(BF16), x, _rms(hn_ref[0], g).astype(BF16)], axis=0)
    pu = _dot(x_ext, w_ref[:, :2 * cc])
    u = pu[:, :cc] * _sigmoid(pu[:, cc:])
    xp_ref[0:CONV_HALO, :] = u[:CONV_HALO] * jnp.where(i > 0, 1.0, 0.0)
    xp_ref[CONV_HALO:CONV_HALO + tm, :] = u[CONV_HALO:CONV_HALO + tm]
    xp_ref[CONV_HALO + tm:, :] = u[CONV_HALO + tm:] * jnp.where(i < last, 1.0, 0.0)

    def conv():
        for t0 in range(0, tm, rows):
            u_ref[0, t0:t0 + rows, :] = _conv_rows(xp_ref, t0, rows, cw_ref, cb_ref, cg_ref, cbeta_ref)
            yield

    def qkv():
        o = 2 * cc
        q_ref[0] = (_dot(x, w_ref[:, o:o + qk]) * scale).astype(BF16)
        yield
        k_ref[0] = _dot(x, w_ref[:, o + qk:o + 2 * qk]).astype(BF16)
        yield
        v_ref[0] = _dot(x, w_ref[:, o + 2 * qk:]).astype(BF16)

    _run_staggered([conv(), qkv()], 0)


def _inproj(h, g, w, cw, cb, cbg, cbeta, cc, qk, vw, tm=512, rows=64):
    bsz, t, d = h.shape
    nb = tm // CONV_HALO
    tile = lambda width: pl.BlockSpec((1, tm, width), lambda b, i: (b, i, 0))
    out = lambda width: jax.ShapeDtypeStruct((bsz, t, width), BF16)
    return pl.pallas_call(
        functools.partial(_inproj_kernel, cc=cc, qk=qk, scale=DIFF_HEAD_DIM ** -0.5 * LOG2E, rows=rows),
        grid=(bsz, t // tm),
        in_specs=[tile(d),
                  pl.BlockSpec((1, CONV_HALO, d), lambda b, i: (b, jnp.maximum(i * nb - 1, 0), 0)),
                  pl.BlockSpec((1, CONV_HALO, d),
                               lambda b, i: (b, jnp.minimum((i + 1) * nb, t // CONV_HALO - 1), 0)),
                  _resident((1, d)), _resident(w.shape), _resident(cw.shape), _resident((1, cc)),
                  _resident((1, cc)), _resident((1, cc))],
        out_specs=[tile(cc), tile(qk), tile(qk), tile(vw)],
        out_shape=[out(cc), out(qk), out(qk), out(vw)],
        scratch_shapes=[pltpu.VMEM((tm + 2 * CONV_HALO, cc), F32)],
        compiler_params=_cparams("parallel", "parallel"),
        name="l0_inproj",
    )(h, h, h, g, w, cw, cb, cbg, cbeta)


def _conv_rows(xp_ref, base, rows, w_ref, b_ref, g_ref, beta_ref):
    c = xp_ref.shape[-1]
    off = CONV_HALO - CONV_PAD
    span = rows + 2 * CONV_HALO
    pieces = []
    for c0 in range(0, c, LANES):
        win = xp_ref[base:base + span, c0:c0 + LANES]
        acc = jnp.zeros((rows, LANES), F32)
        for rem in range(SUBLANES):
            shifted = win if rem == 0 else pltpu.roll(win, span - rem, 0)
            for start in range(0, 2 * CONV_HALO, SUBLANES):
                k = start + rem - off
                if 0 <= k < CONV_WIDTH:
                    acc = acc + shifted[start:start + rows, :] * w_ref[k:k + 1, c0:c0 + LANES]
        pieces.append(acc)
    y = jnp.concatenate(pieces, axis=1) + b_ref[...]
    mu = jnp.mean(y, axis=-1, keepdims=True)
    yc = y - mu
    var = jnp.mean(yc * yc, axis=-1, keepdims=True)
    yn = yc * lax.rsqrt(var + CONV_LN_EPS) * g_ref[...] + beta_ref[...]
    return (yn * _sigmoid(yn)).astype(BF16)


def _t5_bucket(rel):
    nb = REL_BUCKETS // 2
    max_exact = nb // 2
    n = jnp.abs(rel)
    large = jnp.full(rel.shape, max_exact, jnp.int32)
    steps = nb - max_exact
    for m in range(1, steps):
        thr = math.ceil(max_exact * (REL_MAX_DIST / max_exact) ** (m / steps) - 1e-9)
        large = large + jnp.where(n >= thr, 1, 0)
    mag = jnp.where(n < max_exact, n, large)
    return mag + jnp.where(rel > 0, nb, 0)


def _diffattn_kernel(tbl_ref, lq_ref, sg_ref, q_ref, k_ref, v_ref, o_ref, bias_ref, *, tq, ts, kb, lam_init):
    h = pl.program_id(0)
    qi = pl.program_id(1)
    b = pl.program_id(2)
    t = k_ref.shape[1]

    @pl.when(b == 0)
    def _():
        u = lax.broadcasted_iota(jnp.int32, (1, t + tq), 1)
        bucket = _t5_bucket(u - (tq - 1) - qi * tq)
        line = jnp.zeros((1, t + tq), F32)
        for i in range(REL_BUCKETS):
            line = jnp.where(bucket == i, tbl_ref[i * DIFF_HEADS + h], line)
        rows = pltpu.roll(jnp.broadcast_to(line * LOG2E, (tq, t + tq)), 1, 1, stride=1, stride_axis=0)
        bias_ref[...] = rows[:, tq:]

    lq = lq_ref[...]
    lam = (jnp.exp(jnp.sum(lq[0:1] * lq[1:2], axis=-1, keepdims=True))
           - jnp.exp(jnp.sum(lq[2:3] * lq[3:4], axis=-1, keepdims=True)) + lam_init)
    hw = q_ref.shape[-1]
    first = lax.broadcasted_iota(jnp.int32, (ts, hw), 1) < DIFF_HEAD_DIM
    nkb = t // kb

    def softmax_v(r0, comp):
        q = q_ref[0, r0:r0 + ts, :]
        qc = jnp.where(first, q, jnp.zeros_like(q)) if comp == 0 else jnp.where(first, jnp.zeros_like(q), q)
        s = []
        mx = None
        for j in range(nkb):
            sj = _dot_nt(qc, k_ref[0, j * kb:(j + 1) * kb, :]) + bias_ref[r0:r0 + ts, j * kb:(j + 1) * kb]
            mj = jnp.max(sj, axis=-1, keepdims=True)
            mx = mj if mx is None else jnp.maximum(mx, mj)
            s.append(sj)
            yield
        pv = None
        for j in range(nkb):
            vj = v_ref[0, j * kb:(j + 1) * kb, :]
            dj = _dot(jnp.exp2(s[j] - mx), jnp.concatenate([vj, jnp.ones_like(vj)], axis=1))
            pv = dj if pv is None else pv + dj
            yield
        return pv[:, :hw] / pv[:, hw:]

    starts = range(0, tq, ts)
    maps = _run_staggered([softmax_v(r0, comp) for r0 in starts for comp in range(2)], nkb)
    for i, r0 in enumerate(starts):
        o = maps[2 * i] - lam * maps[2 * i + 1]
        o = o * lax.rsqrt(jnp.mean(o * o, axis=-1, keepdims=True) + NORM_EPS) * sg_ref[...] * (1.0 - lam_init)
        o_ref[0, r0:r0 + ts, :] = o.astype(BF16)


def _diff_attention(q, k, v, tbl, lq, sg, lam_init, tq=2048, ts=256, kb=256):
    bsz, t, _ = q.shape
    hw = 2 * DIFF_HEAD_DIM
    return pl.pallas_call(
        functools.partial(_diffattn_kernel, tq=tq, ts=ts, kb=kb, lam_init=lam_init),
        grid=(DIFF_HEADS, t // tq, bsz),
        in_specs=[pl.BlockSpec(memory_space=pltpu.SMEM), _resident(lq.shape), _resident(sg.shape),
                  pl.BlockSpec((1, tq, hw), lambda h, i, b: (b, i, h)),
                  pl.BlockSpec((1, t, hw), lambda h, i, b: (b, 0, h)),
                  pl.BlockSpec((1, t, hw), lambda h, i, b: (b, 0, h))],
        out_specs=pl.BlockSpec((1, tq, hw), lambda h, i, b: (b, i, h)),
        out_shape=jax.ShapeDtypeStruct(q.shape, BF16),
        scratch_shapes=[pltpu.VMEM((tq, t), F32)],
        compiler_params=_cparams("parallel", "parallel", "arbitrary"),
        name="l0_diffattn",
    )(tbl, lq, sg, q, k, v)


def _norm_linear_kernel(x_ref, g_ref, w_ref, o_ref):
    o_ref[...] = _dot(_rms(x_ref[...], g_ref[...]), w_ref[...]).astype(o_ref.dtype)


def _norm_linear(x, g, w, out_dtype, tm=512):
    n, d = x.shape
    m = w.shape[1]
    return pl.pallas_call(
        _norm_linear_kernel,
        grid=(n // tm,),
        in_specs=[pl.BlockSpec((tm, d), lambda i: (i, 0)), _resident((1, d)), _resident(w.shape)],
        out_specs=pl.BlockSpec((tm, m), lambda i: (i, 0)),
        out_shape=jax.ShapeDtypeStruct((n, m), out_dtype),
        compiler_params=_cparams("parallel"),
        name="norm_linear",
    )(x, g, w)


def _xattn_rows(h, g_ref, wq_ref, kv_ref, wo_ref):
    d = h.shape[-1]
    hd = d // XATTN_HEADS
    q = (_dot(_rms(h, g_ref[...]), wq_ref[...]) * (hd ** -0.5 * LOG2E)).astype(BF16)
    yield
    outs = []
    for i in range(XATTN_HEADS):
        kh = kv_ref[0, :, i * hd:(i + 1) * hd]
        vh = kv_ref[0, :, d + i * hd:d + (i + 1) * hd]
        s = _dot_nt(q[:, i * hd:(i + 1) * hd], kh)
        e = jnp.exp2(s - jnp.max(s, axis=-1, keepdims=True))
        p = e / jnp.sum(e, axis=-1, keepdims=True)
        outs.append(_dot(p, vh).astype(BF16))
    yield
    return h + _dot(jnp.concatenate(outs, axis=1), wo_ref[...])


def _l0_tail_kernel(u_ref, o_ref, h_ref, wout_ref, gx_ref, wq_ref, kv_ref, wo_ref, out_ref, *, ts):
    tq = h_ref.shape[1]
    cc = u_ref.shape[-1]

    def sub(r0):
        rs = slice(r0, r0 + ts)
        h1 = h_ref[0, rs, :] + _dot(u_ref[0, rs, :], wout_ref[:cc, :]) + _dot(o_ref[0, rs, :], wout_ref[cc:, :])
        yield
        out_ref[0, rs, :] = yield from _xattn_rows(h1, gx_ref, wq_ref, kv_ref, wo_ref)

    _run_staggered([sub(r0) for r0 in range(0, tq, ts)], 1)


def _l0_tail(u, o, h, wout, gx, wq, kv, wo, tq=1024, ts=256):
    bsz, t, d = h.shape
    tile = lambda w: pl.BlockSpec((1, tq, w), lambda b, i: (b, i, 0))
    return pl.pallas_call(
        functools.partial(_l0_tail_kernel, ts=ts),
        grid=(bsz, t // tq),
        in_specs=[tile(u.shape[-1]), tile(o.shape[-1]), tile(d), _resident(wout.shape), _resident((1, d)),
                  _resident(wq.shape), pl.BlockSpec((1, kv.shape[1], 2 * d), lambda b, i: (b, 0, 0)),
                  _resident(wo.shape)],
        out_specs=tile(d),
        out_shape=jax.ShapeDtypeStruct(h.shape, F32),
        compiler_params=_cparams("parallel", "parallel"),
        name="l0_tail",
    )(u, o, h, wout, gx, wq, kv, wo)


def _mlp_kernel(h_ref, g_ref, wu_ref, wd_ref, gf_ref, o_ref, *, hc, final_norm):
    h = h_ref[...]
    xn = _rms(h, g_ref[...]).astype(BF16)
    acc = h
    for c in range(wu_ref.shape[1] // hc):
        a = jnp.maximum(_dot(xn, wu_ref[:, c * hc:(c + 1) * hc]), 0.0)
        acc = acc + _dot(a * a, wd_ref[c * hc:(c + 1) * hc, :])
    if final_norm:
        acc = _rms(acc, gf_ref[...])
    o_ref[...] = acc


def _mlp(h, g, wu, wd, gf, final_norm, tm=512, hc=1024):
    n, d = h.shape
    row = lambda i: (i, 0)
    return pl.pallas_call(
        functools.partial(_mlp_kernel, hc=hc, final_norm=final_norm),
        grid=(n // tm,),
        in_specs=[pl.BlockSpec((tm, d), row), _resident((1, d)), _resident(wu.shape), _resident(wd.shape),
                  _resident((1, d))],
        out_specs=pl.BlockSpec((tm, d), row),
        out_shape=jax.ShapeDtypeStruct((n, d), F32),
        compiler_params=_cparams("parallel"),
        name="mlp",
    )(h, g, wu, wd, gf)


def _rwkv_prep_kernel(h_ref, hp_ref, hn_ref, g_ref, mu_ref, wr_ref, wk_ref, wv_ref, w1_ref, w2_ref, w0_ref,
                      a1_ref, a2_ref, a0_ref, g1_ref, g2_ref, kk_ref, ka_ref, rk_ref, sel_ref, selt_ref,
                      r_out, v_out, kn_out, gate_out, bonus_out, kd_out, cum_out, b_out, *, ts):
    i = pl.program_id(1)
    last = pl.num_programs(1) - 1
    tm = h_ref.shape[1]
    g = g_ref[...]
    ti = lax.broadcasted_iota(jnp.int32, (ts, ts), 0)
    si = lax.broadcasted_iota(jnp.int32, (ts, ts), 1)
    same_chunk = (ti >> CHUNK_SHIFT) == (si >> CHUNK_SHIFT)
    before = (jnp.where(same_chunk, jnp.where(si <= ti, 1.0, 0.0), 0.0).astype(BF16),
              jnp.where(same_chunk, jnp.where(si >= ti, 1.0, 0.0), 0.0).astype(BF16))

    def rows(r0):
        x = _rms(h_ref[0, r0:r0 + ts, :], g)
        if r0 == 0:
            prev_row = _rms(hp_ref[0], g)[SUBLANES - 1:, :] * jnp.where(i > 0, 1.0, 0.0)
        else:
            prev_row = _rms(h_ref[0, r0 - SUBLANES:r0, :], g)[SUBLANES - 1:, :]
        if r0 + ts == tm:
            next_row = _rms(hn_ref[0], g)[0:1, :] * jnp.where(i < last, 1.0, 0.0)
        else:
            next_row = _rms(h_ref[0, r0 + ts:r0 + ts + SUBLANES, :], g)[0:1, :]
        rowid = lax.broadcasted_iota(jnp.int32, x.shape, 0)
        x_prev = jnp.where(rowid == 0, prev_row, pltpu.roll(x, 1, 0))
        x_next = jnp.where(rowid == ts - 1, next_row, pltpu.roll(x, ts - 1, 0))
        hh = 0.5 * (x_prev + x_next) - x
        mix = lambda j: x + hh * mu_ref[j:j + 1, :]
        r = _dot(mix(0), wr_ref[...])
        k = _dot(mix(2), wk_ref[...])
        v = _dot(mix(3), wv_ref[...])
        yield
        gate_in = _dot(mix(5), g1_ref[...])
        lw = _dot(mix(1), w1_ref[...])
        la = _dot(mix(4), a1_ref[...])
        yield
        gate = _dot(_sigmoid(gate_in), g2_ref[...])
        lw = jnp.tanh(lw)
        w_pre = [w0_ref[z:z + 1, :] + _dot(lw, w2_ref[z]) for z in range(2)]
        a_pre = [a0_ref[z:z + 1, :] + _dot(la, a2_ref[z]) for z in range(2)]
        kk = k * kk_ref[...]
        ss = _dot(kk * kk, sel_ref[...])
        yield
        kn = kk * lax.rsqrt(jnp.maximum(_dot(ss, selt_ref[...]), 1e-24))
        kka = k * ka_ref[...]
        kd_sum = jnp.zeros_like(k)
        cum = []
        for z in range(2):
            cum.append(_split_dot(before[z], _sigmoid(w_pre[z]) * (-math.exp(-0.5))))
            rate = _sigmoid(a_pre[z])
            kd = k + kka * (rate - 1.0)
            kd_out[z, 0, r0:r0 + ts, :] = kd.astype(kd_out.dtype)
            b_out[z, 0, r0:r0 + ts, :] = (kn * rate).astype(b_out.dtype)
            kd_sum = kd_sum + kd
        bs = _dot(r * kd_sum * rk_ref[...], sel_ref[...])
        yield
        r_out[0, r0:r0 + ts, :] = r.astype(r_out.dtype)
        v_out[0, r0:r0 + ts, :] = v.astype(v_out.dtype)
        kn_out[0, r0:r0 + ts, :] = kn.astype(kn_out.dtype)
        gate_out[0, r0:r0 + ts, :] = gate.astype(gate_out.dtype)
        bonus_out[0, r0:r0 + ts, :] = (_dot(bs, selt_ref[...]) * v).astype(bonus_out.dtype)
        for z in range(2):
            cum_out[z, 0, r0:r0 + ts, :] = cum[z]

    _run_staggered([rows(r0) for r0 in range(0, tm, ts)], 1)


def _rwkv_prep(h, g, mu, wr, wk, wv, w1, w2, w0, a1, a2, a0, g1, g2, kk, ka, rk, sel, selt, tm=512, ts=256):
    bsz, t, d = h.shape
    nb = tm // SUBLANES
    tile = pl.BlockSpec((1, tm, d), lambda b, i: (b, i, 0))
    tile2 = pl.BlockSpec((2, 1, tm, d), lambda b, i: (0, b, i, 0))
    one = jax.ShapeDtypeStruct((bsz, t, d), BF16)
    two = jax.ShapeDtypeStruct((2, bsz, t, d), BF16)
    consts = [g, mu, wr, wk, wv, w1, w2, w0, a1, a2, a0, g1, g2, kk, ka, rk, sel, selt]
    return pl.pallas_call(
        functools.partial(_rwkv_prep_kernel, ts=ts),
        grid=(bsz, t // tm),
        in_specs=[tile,
                  pl.BlockSpec((1, SUBLANES, d), lambda b, i: (b, jnp.maximum(i * nb - 1, 0), 0)),
                  pl.BlockSpec((1, SUBLANES, d),
                               lambda b, i: (b, jnp.minimum((i + 1) * nb, t // SUBLANES - 1), 0))]
                 + [_resident(c.shape) for c in consts],
        out_specs=[tile, tile, tile, tile, tile, tile2, tile2, tile2],
        out_shape=[one, one, one, one, one, two, jax.ShapeDtypeStruct((2, bsz, t, d), F32), two],
        compiler_params=_cparams("parallel", "parallel"),
        name="l1_rwkv_prep",
    )(h, h, h, *consts)


def _blockdiag(x):
    lane = lax.broadcasted_iota(jnp.int32, x.shape, 1)
    head0 = (lane & (PAIR - 1)) < RWKV_HEAD_DIM
    zero = jnp.zeros_like(x)
    return jnp.concatenate([jnp.where(head0, x, zero), jnp.where(head0, zero, x)], axis=0)


def _chunk_local(r, k, v, kn, cum, b, rev):
    c = CHUNK
    bd = _blockdiag
    row = lax.broadcasted_iota(jnp.int32, (c, PAIR), 0)
    lane = lax.broadcasted_iota(jnp.int32, (c, PAIR), 1)
    s_idx = lane & (c - 1)
    if rev:
        cum_prev = jnp.where(row == c - 1, 0.0, pltpu.roll(cum, c - 1, 0))
        tot = cum[0:1, :]
        strict = s_idx > row
        incl = s_idx >= row
    else:
        cum_prev = jnp.where(row == 0, 0.0, pltpu.roll(cum, 1, 0))
        tot = cum[c - 1:c, :]
        strict = s_idx < row
        incl = s_idx <= row
    w_incl = jnp.exp(cum)
    w_excl = jnp.exp(cum_prev)
    w_inv = jnp.exp(-cum)
    w_tot = jnp.exp(tot)
    w_rest = jnp.exp(tot - cum)
    a_t = -kn * w_excl
    r_t = r * w_incl
    b_t = b * w_inv
    k_t = k * w_inv
    b_h = b * w_rest
    k_h = k * w_rest
    same_blk = (s_idx >> DIAG_SHIFT) == (row >> DIAG_SHIFT)

    sc = _dot_nt(jnp.concatenate([a_t, r_t], axis=0), jnp.concatenate([bd(b_t), bd(k_t)], axis=0))
    yield
    p_ab = jnp.where(strict, sc[:c, :PAIR], 0.0)
    p_ak = jnp.where(strict, sc[:c, PAIR:], 0.0)
    p_rb = jnp.where(incl, sc[c:, :PAIR], 0.0)
    p_rk = jnp.where(incl, sc[c:, PAIR:], 0.0)
    dm = jnp.where(same_blk, p_ab, 0.0)
    em = p_ab - dm
    x2 = _dot(dm, bd(dm))
    av = _dot(p_ak, bd(v))
    yield
    td = jnp.where(s_idx == row, 1.0, 0.0) + dm
    both = _dot(jnp.concatenate([x2, td], axis=0), bd(x2))
    yield
    x4 = both[:c]
    td = td + both[c:]
    both = _dot(jnp.concatenate([x4, td], axis=0), bd(x4))
    yield
    td = td + both[c:]
    td = td + _dot(td, bd(both[:c]))
    yield
    ty = _dot(td, bd(jnp.concatenate([a_t, av, em], axis=1)))
    yield
    au, f1 = ty[:, :2 * PAIR], ty[:, 2 * PAIR:]
    both = _dot(f1, bd(jnp.concatenate([f1, au], axis=1)))
    f2 = both[:, :PAIR]
    au = au + both[:, PAIR:]
    yield
    au = au + _dot(f2, bd(au))
    yield
    rhs = jnp.concatenate([bd(au), jnp.concatenate([jnp.zeros((PAIR, PAIR), F32), bd(v)], axis=1)], axis=0)
    ry = _dot(jnp.concatenate([p_rb, p_rk], axis=1), rhs)
    rhs2 = jnp.concatenate([au, jnp.concatenate([jnp.zeros((c, PAIR), F32), v], axis=1)], axis=0)
    mg = _dot(jnp.concatenate([b_h, k_h], axis=0).T, rhs2)
    yield
    rbar = r_t + ry[:, :PAIR]
    yloc = ry[:, PAIR:]
    r2 = lax.broadcasted_iota(jnp.int32, (PAIR, PAIR), 0)
    l2 = lax.broadcasted_iota(jnp.int32, (PAIR, PAIR), 1)
    same_head = (r2 >> HEAD_SHIFT) == (l2 >> HEAD_SHIFT)
    m = jnp.where(same_head, mg[:, :PAIR], 0.0) + jnp.where(r2 == l2, w_tot, 0.0)
    gg = jnp.where(same_head, mg[:, PAIR:], 0.0)
    return rbar, yloc, m, gg


def _scan_kernel(r_ref, v_ref, kn_ref, kd_ref, cum_ref, b_ref, yf_ref, yb_ref, ds_ref, loc_a, loc_b, *, unroll):
    nc = r_ref.shape[1] // CHUNK
    groups = nc // unroll
    total = (r_ref.shape[2] // PAIR) * groups
    y_refs = (yf_ref, yb_ref)
    c = CHUNK
    ds_ref[...] = jnp.zeros(ds_ref.shape, F32)
    loc_b[...] = jnp.zeros(loc_b.shape, F32)

    def places(gidx):
        grp = gidx % groups
        lanes = pl.ds(pl.multiple_of((gidx // groups) * PAIR, PAIR), PAIR)
        out = []
        for u in range(unroll):
            for z in range(2):
                cidx = grp * unroll + u
                if z == 1:
                    cidx = nc - 1 - cidx
                out.append((z, pl.ds(pl.multiple_of(cidx * c, c), c), lanes))
        return out

    def local_terms(gidx, loc_ref):
        f32 = lambda ref, *idx: ref[idx].astype(F32)
        gens = [_chunk_local(f32(r_ref, 0, sl, ln), f32(kd_ref, z, 0, sl, ln), f32(v_ref, 0, sl, ln),
                             f32(kn_ref, 0, sl, ln), cum_ref[z, 0, sl, ln], f32(b_ref, z, 0, sl, ln),
                             rev=(z == 1))
                for z, sl, ln in places(gidx)]

        def park(j, gen):
            rbar, yloc, m, gg = yield from gen
            loc_ref[j, 0:c, :] = rbar
            loc_ref[j, c:2 * c, :] = yloc
            loc_ref[j, 2 * c:2 * c + PAIR, :] = m
            loc_ref[j, 2 * c + PAIR:, :] = gg

        return [park(j, gen) for j, gen in enumerate(gens)]

    def recurrence(gidx, loc_ref):
        keep = jnp.where(gidx % groups == 0, 0.0, 1.0)
        ds = [ds_ref[0] * keep, ds_ref[1] * keep]
        for j, (z, sl, ln) in enumerate(places(gidx)):
            both = _dot(jnp.concatenate([loc_ref[j, 0:c, :], loc_ref[j, 2 * c:2 * c + PAIR, :]], axis=0), ds[z])
            y_refs[z][0, sl, ln] = (both[:c] + loc_ref[j, c:2 * c, :]).astype(y_refs[z].dtype)
            ds[z] = both[c:] + loc_ref[j, 2 * c + PAIR:, :]
            if z == 1:
                yield
        ds_ref[0] = ds[0]
        ds_ref[1] = ds[1]

    def body(it, carry):
        first = 2 * it
        _run_staggered([recurrence(jnp.maximum(first - 1, 0), loc_b)] + local_terms(first, loc_a), 0)
        _run_staggered([recurrence(first, loc_a)] + local_terms(first + 1, loc_b), 0)
        return carry

    lax.fori_loop(0, total // 2, body, 0)
    _run_staggered([recurrence(total - 1, loc_b)], 0)


def _wkv7_scan(r, v, kn, kd, cum, b, unroll=8, pairs=2):
    bsz, t, d = r.shape
    nc = t // CHUNK
    unroll = min(unroll, nc // 2)
    assert nc % (2 * unroll) == 0, "the scan kernel takes chunk groups in pairs"
    one = pl.BlockSpec((1, t, pairs * PAIR), lambda bb, p: (bb, 0, p))
    two = pl.BlockSpec((2, 1, t, pairs * PAIR), lambda bb, p: (0, bb, 0, p))
    out = jax.ShapeDtypeStruct((bsz, t, d), BF16)
    return pl.pallas_call(
        functools.partial(_scan_kernel, unroll=unroll),
        grid=(bsz, d // (pairs * PAIR)),
        in_specs=[one, one, one, two, two, two],
        out_specs=[one, one],
        out_shape=[out, out],
        scratch_shapes=[pltpu.VMEM((2, PAIR, PAIR), F32)]
                       + [pltpu.VMEM((2 * unroll, 2 * CHUNK + 2 * PAIR, PAIR), F32)] * 2,
        compiler_params=_cparams("parallel", "parallel"),
        name="l1_wkv7_scan",
    )(r, v, kn, kd, cum, b)


def _l1_tail_kernel(yf_ref, yb_ref, bonus_ref, gate_ref, h_ref, lg_ref, lb_ref, wo_ref, sel_ref, selt_ref,
                    gx_ref, wq_ref, kv_ref, wxo_ref, out_ref, *, ts):
    tq = h_ref.shape[1]
    inv_n = 1.0 / RWKV_HEAD_DIM

    def sub(r0):
        rs = slice(r0, r0 + ts)
        y = yf_ref[0, rs, :] + yb_ref[0, rs, :]
        mu = _dot(y, sel_ref[...])
        yield
        yc = y - _dot(mu, selt_ref[...]) * inv_n
        var = _dot(yc * yc, sel_ref[...])
        yield
        yn = yc * lax.rsqrt(_dot(var, selt_ref[...]) * inv_n + GN_EPS) * lg_ref[...] + lb_ref[...]
        h1 = h_ref[0, rs, :] + _dot((yn + bonus_ref[0, rs, :]) * gate_ref[0, rs, :], wo_ref[...])
        yield
        out_ref[0, rs, :] = yield from _xattn_rows(h1, gx_ref, wq_ref, kv_ref, wxo_ref)

    _run_staggered([sub(r0) for r0 in range(0, tq, ts)], 1)


def _l1_tail(yf, yb, bonus, gate, h, lg, lb, wo, sel, selt, gx, wq, kv, wxo, tq=1024, ts=256):
    bsz, t, d = h.shape
    tile = pl.BlockSpec((1, tq, d), lambda b, i: (b, i, 0))
    return pl.pallas_call(
        functools.partial(_l1_tail_kernel, ts=ts),
        grid=(bsz, t // tq),
        in_specs=[tile, tile, tile, tile, tile, _resident((1, d)), _resident((1, d)), _resident(wo.shape),
                  _resident(sel.shape), _resident(selt.shape), _resident((1, d)), _resident(wq.shape),
                  pl.BlockSpec((1, kv.shape[1], 2 * d), lambda b, i: (b, 0, 0)), _resident(wxo.shape)],
        out_specs=tile,
        out_shape=jax.ShapeDtypeStruct(h.shape, F32),
        compiler_params=_cparams("parallel", "parallel"),
        name="l1_tail",
    )(yf, yb, bonus, gate, h, lg, lb, wo, sel, selt, gx, wq, kv, wxo)


def _pad_lora_out(w2):
    zero = jnp.zeros_like(w2[0])
    return jnp.stack([jnp.concatenate([w2[0], zero], axis=0), jnp.concatenate([zero, w2[1]], axis=0)])


def kernel(x, mem, rel_bias_table, norm_mix, norm_xattn, norm_mem, norm_ffn, norm_final, ab_w_in, ab_w_out, conv_w, conv_b, conv_ln_g, conv_ln_b, diff_lq1, diff_lk1, diff_lq2, diff_lk2, diff_subln_g, rwkv_mu, rwkv_w_r, rwkv_w_k, rwkv_w_v, rwkv_w_o, rwkv_w0, rwkv_w1, rwkv_w2, rwkv_a0, rwkv_a1, rwkv_a2, rwkv_g1, rwkv_g2, rwkv_k_k, rwkv_k_a, rwkv_r_k, rwkv_ln_g, rwkv_ln_b, xattn_w_q, xattn_w_kv, xattn_w_o, ffn_w_up, ffn_w_down):
    bsz, t, d = x.shape
    n = bsz * t
    depth = norm_mix.shape[0]
    n_mem = mem.shape[1]
    cc = conv_w.shape[-1]
    qk = DIFF_HEADS * 2 * DIFF_HEAD_DIM
    vw = ab_w_in.shape[-1] - 2 * cc - 2 * qk
    bf = lambda w: w.astype(BF16)
    row = lambda w: w.reshape(1, -1)

    heads = d // RWKV_HEAD_DIM
    head_of = jnp.arange(d, dtype=jnp.int32) // RWKV_HEAD_DIM
    assert heads <= LANES
    sel = (head_of[:, None] == jnp.arange(LANES, dtype=jnp.int32)[None, :]).astype(BF16)
    selt = sel.T

    h = x.reshape(n, d)
    mem2 = mem.reshape(bsz * n_mem, d)
    for i in range(depth):
        j = i // 2
        kv = _norm_linear(mem2, row(norm_mem[i]), bf(xattn_w_kv[i]), BF16).reshape(bsz, n_mem, 2 * d)
        if i % 2 == 0:
            lam_init = 0.8 - 0.6 * math.exp(-0.3 * i)
            h3 = h.reshape(bsz, t, d)
            u, q, k, v = _inproj(h3, row(norm_mix[i]), bf(ab_w_in[j]), conv_w[j], row(conv_b[j]),
                                 row(conv_ln_g[j]), row(conv_ln_b[j]), cc, qk, vw)
            lq = jnp.stack([diff_lq1[j], diff_lk1[j], diff_lq2[j], diff_lk2[j]])
            o = _diff_attention(q, k, v, rel_bias_table.reshape(-1), lq, row(diff_subln_g[j]), lam_init)
            h = _l0_tail(u, o, h3, bf(ab_w_out[j]), row(norm_xattn[i]), bf(xattn_w_q[i]), kv,
                         bf(xattn_w_o[i])).reshape(n, d)
        else:
            h3 = h.reshape(bsz, t, d)
            w1 = bf(jnp.concatenate([rwkv_w1[j, 0], rwkv_w1[j, 1]], axis=1))
            a1 = bf(jnp.concatenate([rwkv_a1[j, 0], rwkv_a1[j, 1]], axis=1))
            r, v, kn, gate, bonus, kd, cum, b = _rwkv_prep(
                h3, row(norm_mix[i]), rwkv_mu[j], bf(rwkv_w_r[j]), bf(rwkv_w_k[j]), bf(rwkv_w_v[j]),
                w1, bf(_pad_lora_out(rwkv_w2[j])), rwkv_w0[j], a1, bf(_pad_lora_out(rwkv_a2[j])), rwkv_a0[j],
                bf(rwkv_g1[j]), bf(rwkv_g2[j]), row(rwkv_k_k[j]), row(rwkv_k_a[j]), row(rwkv_r_k[j]),
                sel, selt)
            yf, yb = _wkv7_scan(r, v, kn, kd, cum, b)
            h = _l1_tail(yf, yb, bonus, gate, h3, row(rwkv_ln_g[j]), row(rwkv_ln_b[j]), bf(rwkv_w_o[j]), sel, selt,
                         row(norm_xattn[i]), bf(xattn_w_q[i]), kv, bf(xattn_w_o[i])).reshape(n, d)
        h = _mlp(h, row(norm_ffn[i]), bf(ffn_w_up[i]), bf(ffn_w_down[i]), row(norm_final),
                 final_norm=(i == depth - 1))
    return h.reshape(bsz, t, d)
```

```python
import functools
import math

import jax
import jax.numpy as jnp
from jax import lax
from jax.experimental import pallas as pl
from jax.experimental.pallas import tpu as pltpu

F32 = jnp.float32
BF16 = jnp.bfloat16

V7X_VMEM_BYTES = 64 * 1024 * 1024
VMEM_LIMIT_BYTES = V7X_VMEM_BYTES - 8 * 1024 * 1024

LOG2E = math.log2(math.e)
NORM_EPS = 1e-6
CONV_LN_EPS = 1e-5
GN_EPS = 64e-5
CONV_WIDTH = 31
CONV_PAD = CONV_WIDTH // 2
CONV_HALO = 16
DIFF_HEADS = 4
DIFF_HEAD_DIM = 64
REL_BUCKETS = 32
REL_MAX_DIST = 128
XATTN_HEADS = 4
RWKV_HEAD_DIM = 64
SUBLANES = 8
LANES = 128
CHUNK = 64
CHUNK_SHIFT = CHUNK.bit_length() - 1
HEAD_SHIFT = RWKV_HEAD_DIM.bit_length() - 1
DIAG_SHIFT = 4
PAIR = 2 * RWKV_HEAD_DIM


def _cparams(*sem):
    return pltpu.CompilerParams(dimension_semantics=sem, vmem_limit_bytes=VMEM_LIMIT_BYTES)


def _resident(shape):
    nd = len(shape)
    return pl.BlockSpec(shape, lambda *_: (0,) * nd, pipeline_mode=pl.Buffered(1))


def _rms(x, g):
    ms = jnp.mean(x * x, axis=-1, keepdims=True)
    return x * lax.rsqrt(ms + NORM_EPS) * g


def _sigmoid(x):
    return 1.0 / (1.0 + jnp.exp(-x))


def _dot(a, b):
    return jnp.dot(a.astype(BF16), b.astype(BF16), preferred_element_type=F32)


def _dot_nt(a, b):
    return lax.dot_general(a.astype(BF16), b.astype(BF16), (((1,), (1,)), ((), ())),
                           preferred_element_type=F32)


def _split_dot(mat, x):
    hi = x.astype(BF16)
    lo = (x - hi.astype(F32)).astype(BF16)
    return (jnp.dot(mat, hi, preferred_element_type=F32) + jnp.dot(mat, lo, preferred_element_type=F32))


def _run_staggered(gens, offset):
    results = [None] * len(gens)
    running = [True] * len(gens)
    rnd = 0
    while any(running):
        for i, gen in enumerate(gens):
            if rnd >= i * offset and running[i]:
                try:
                    next(gen)
                except StopIteration as stop:
                    results[i] = stop.value
                    running[i] = False
        rnd += 1
    return results


def _inproj_kernel(h_ref, hp_ref, hn_ref, g_ref, w_ref, cw_ref, cb_ref, cg_ref, cbeta_ref,
                   u_ref, q_ref, k_ref, v_ref, xp_ref, *, cc, qk, scale, rows):
    i = pl.program_id(1)
    last = pl.num_programs(1) - 1
    tm = h_ref.shape[1]
    g = g_ref[...]
    x = _rms(h_ref[0], g).astype(BF16)
    x_ext = jnp.concatenate([_rms(hp_ref[0], g).astype(BF16), x, _rms(hn_ref[0], g).astype(BF16)], axis=0)
    pu = _dot(x_ext, w_ref[:, :2 * cc])
    u = pu[:, :cc] * _sigmoid(pu[:, cc:])
    xp_ref[0:CONV_HALO, :] = u[:CONV_HALO] * jnp.where(i > 0, 1.0, 0.0)
    xp_ref[CONV_HALO:CONV_HALO + tm, :] = u[CONV_HALO:CONV_HALO + tm]
    xp_ref[CONV_HALO + tm:, :] = u[CONV_HALO + tm:] * jnp.where(i < last, 1.0, 0.0)

    def conv():
        for t0 in range(0, tm, rows):
            u_ref[0, t0:t0 + rows, :] = _conv_rows(xp_ref, t0, rows, cw_ref, cb_ref, cg_ref, cbeta_ref)
            yield

    def qkv():
        o = 2 * cc
        q_ref[0] = (_dot(x, w_ref[:, o:o + qk]) * scale).astype(BF16)
        yield
        k_ref[0] = _dot(x, w_ref[:, o + qk:o + 2 * qk]).astype(BF16)
        yield
        v_ref[0] = _dot(x, w_ref[:, o + 2 * qk:]).astype(BF16)

    _run_staggered([conv(), qkv()], 0)


def _inproj(h, g, w, cw, cb, cbg, cbeta, cc, qk, vw, tm=512, rows=64):
    bsz, t, d = h.shape
    nb = tm // CONV_HALO
    tile = lambda width: pl.BlockSpec((1, tm, width), lambda b, i: (b, i, 0))
    out = lambda width: jax.ShapeDtypeStruct((bsz, t, width), BF16)
    return pl.pallas_call(
        functools.partial(_inproj_kernel, cc=cc, qk=qk, scale=DIFF_HEAD_DIM ** -0.5 * LOG2E, rows=rows),
        grid=(bsz, t // tm),
        in_specs=[tile(d),
                  pl.BlockSpec((1, CONV_HALO, d), lambda b, i: (b, jnp.maximum(i * nb - 1, 0), 0)),
                  pl.BlockSpec((1, CONV_HALO, d),
                               lambda b, i: (b, jnp.minimum((i + 1) * nb, t // CONV_HALO - 1), 0)),
                  _resident((1, d)), _resident(w.shape), _resident(cw.shape), _resident((1, cc)),
                  _resident((1, cc)), _resident((1, cc))],
        out_specs=[tile(cc), tile(qk), tile(qk), tile(vw)],
        out_shape=[out(cc), out(qk), out(qk), out(vw)],
        scratch_shapes=[pltpu.VMEM((tm + 2 * CONV_HALO, cc), F32)],
        compiler_params=_cparams("parallel", "parallel"),
        name="l0_inproj",
    )(h, h, h, g, w, cw, cb, cbg, cbeta)


def _conv_rows(xp_ref, base, rows, w_ref, b_ref, g_ref, beta_ref):
    c = xp_ref.shape[-1]
    off = CONV_HALO - CONV_PAD
    span = rows + 2 * CONV_HALO
    pieces = []
    for c0 in range(0, c, LANES):
        win = xp_ref[base:base + span, c0:c0 + LANES]
        acc = jnp.zeros((rows, LANES), F32)
        for rem in range(SUBLANES):
            shifted = win if rem == 0 else pltpu.roll(win, span - rem, 0)
            for start in range(0, 2 * CONV_HALO, SUBLANES):
                k = start + rem - off
                if 0 <= k < CONV_WIDTH:
                    acc = acc + shifted[start:start + rows, :] * w_ref[k:k + 1, c0:c0 + LANES]
        pieces.append(acc)
    y = jnp.concatenate(pieces, axis=1) + b_ref[...]
    mu = jnp.mean(y, axis=-1, keepdims=True)
    yc = y - mu
    var = jnp.mean(yc * yc, axis=-1, keepdims=True)
    yn = yc * lax.rsqrt(var + CONV_LN_EPS) * g_ref[...] + beta_ref[...]
    return (yn * _sigmoid(yn)).astype(BF16)


def _t5_bucket(rel):
    nb = REL_BUCKETS // 2
    max_exact = nb // 2
    n = jnp.abs(rel)
    large = jnp.full(rel.shape, max_exact, jnp.int32)
    steps = nb - max_exact
    for m in range(1, steps):
        thr = math.ceil(max_exact * (REL_MAX_DIST / max_exact) ** (m / steps) - 1e-9)
        large = large + jnp.where(n >= thr, 1, 0)
    mag = jnp.where(n < max_exact, n, large)
    return mag + jnp.where(rel > 0, nb, 0)


def _diffattn_kernel(tbl_ref, lq_ref, sg_ref, q_ref, k_ref, v_ref, o_ref, bias_ref, *, tq, ts, kb, lam_init):
    h = pl.program_id(0)
    qi = pl.program_id(1)
    b = pl.program_id(2)
    t = k_ref.shape[1]

    @pl.when(b == 0)
    def _():
        u = lax.broadcasted_iota(jnp.int32, (1, t + tq), 1)
        bucket = _t5_bucket(u - (tq - 1) - qi * tq)
        line = jnp.zeros((1, t + tq), F32)
        for i in range(REL_BUCKETS):
            line = jnp.where(bucket == i, tbl_ref[i * DIFF_HEADS + h], line)
        rows = pltpu.roll(jnp.broadcast_to(line * LOG2E, (tq, t + tq)), 1, 1, stride=1, stride_axis=0)
        bias_ref[...] = rows[:, tq:]

    lq = lq_ref[...]
    lam = (jnp.exp(jnp.sum(lq[0:1] * lq[1:2], axis=-1, keepdims=True))
           - jnp.exp(jnp.sum(lq[2:3] * lq[3:4], axis=-1, keepdims=True)) + lam_init)
    hw = q_ref.shape[-1]
    first = lax.broadcasted_iota(jnp.int32, (ts, hw), 1) < DIFF_HEAD_DIM
    nkb = t // kb

    def softmax_v(r0, comp):
        q = q_ref[0, r0:r0 + ts, :]
        qc = jnp.where(first, q, jnp.zeros_like(q)) if comp == 0 else jnp.where(first, jnp.zeros_like(q), q)
        s = []
        mx = None
        for j in range(nkb):
            sj = _dot_nt(qc, k_ref[0, j * kb:(j + 1) * kb, :]) + bias_ref[r0:r0 + ts, j * kb:(j + 1) * kb]
            mj = jnp.max(sj, axis=-1, keepdims=True)
            mx = mj if mx is None else jnp.maximum(mx, mj)
            s.append(sj)
            yield
        pv = None
        for j in range(nkb):
            vj = v_ref[0, j * kb:(j + 1) * kb, :]
            dj = _dot(jnp.exp2(s[j] - mx), jnp.concatenate([vj, jnp.ones_like(vj)], axis=1))
            pv = dj if pv is None else pv + dj
            yield
        return pv[:, :hw] / pv[:, hw:]

    starts = range(0, tq, ts)
    maps = _run_staggered([softmax_v(r0, comp) for r0 in starts for comp in range(2)], nkb)
    for i, r0 in enumerate(starts):
        o = maps[2 * i] - lam * maps[2 * i + 1]
        o = o * lax.rsqrt(jnp.mean(o * o, axis=-1, keepdims=True) + NORM_EPS) * sg_ref[...] * (1.0 - lam_init)
        o_ref[0, r0:r0 + ts, :] = o.astype(BF16)


def _diff_attention(q, k, v, tbl, lq, sg, lam_init, tq=2048, ts=256, kb=256):
    bsz, t, _ = q.shape
    hw = 2 * DIFF_HEAD_DIM
    return pl.pallas_call(
        functools.partial(_diffattn_kernel, tq=tq, ts=ts, kb=kb, lam_init=lam_init),
        grid=(DIFF_HEADS, t // tq, bsz),
        in_specs=[pl.BlockSpec(memory_space=pltpu.SMEM), _resident(lq.shape), _resident(sg.shape),
                  pl.BlockSpec((1, tq, hw), lambda h, i, b: (b, i, h)),
                  pl.BlockSpec((1, t, hw), lambda h, i, b: (b, 0, h)),
                  pl.BlockSpec((1, t, hw), lambda h, i, b: (b, 0, h))],
        out_specs=pl.BlockSpec((1, tq, hw), lambda h, i, b: (b, i, h)),
        out_shape=jax.ShapeDtypeStruct(q.shape, BF16),
        scratch_shapes=[pltpu.VMEM((tq, t), F32)],
        compiler_params=_cparams("parallel", "parallel", "arbitrary"),
        name="l0_diffattn",
    )(tbl, lq, sg, q, k, v)


def _kv_proj_kernel(x_ref, g_ref, w_ref, o_ref):
    o_ref[0] = _dot(_rms(x_ref[...], g_ref[0]), w_ref[0]).astype(o_ref.dtype)


def _kv_proj(x, g, w, tm=512):
    n, d = x.shape
    layers, _, m = w.shape
    return pl.pallas_call(
        _kv_proj_kernel,
        grid=(layers, n // tm),
        in_specs=[pl.BlockSpec((tm, d), lambda l, i: (i, 0)), pl.BlockSpec((1, 1, d), lambda l, i: (l, 0, 0)),
                  pl.BlockSpec((1, d, m), lambda l, i: (l, 0, 0))],
        out_specs=pl.BlockSpec((1, tm, m), lambda l, i: (l, i, 0)),
        out_shape=jax.ShapeDtypeStruct((layers, n, m), BF16),
        compiler_params=_cparams("parallel", "parallel"),
        name="kv_proj",
    )(x, g, w)


def _xattn_rows(h, g_ref, wq_ref, kv_ref, wo_ref):
    d = h.shape[-1]
    hd = d // XATTN_HEADS
    q = (_dot(_rms(h, g_ref[...]), wq_ref[...]) * (hd ** -0.5 * LOG2E)).astype(BF16)
    yield
    outs = []
    for i in range(XATTN_HEADS):
        kh = kv_ref[0, :, i * hd:(i + 1) * hd]
        vh = kv_ref[0, :, d + i * hd:d + (i + 1) * hd]
        s = _dot_nt(q[:, i * hd:(i + 1) * hd], kh)
        e = jnp.exp2(s - jnp.max(s, axis=-1, keepdims=True))
        p = e / jnp.sum(e, axis=-1, keepdims=True)
        outs.append(_dot(p, vh).astype(BF16))
    yield
    return h + _dot(jnp.concatenate(outs, axis=1), wo_ref[...])


def _l0_tail_kernel(u_ref, o_ref, h_ref, wout_ref, gx_ref, wq_ref, kv_ref, wo_ref, out_ref, *, ts):
    tq = h_ref.shape[1]
    cc = u_ref.shape[-1]

    def sub(r0):
        rs = slice(r0, r0 + ts)
        h1 = h_ref[0, rs, :] + _dot(u_ref[0, rs, :], wout_ref[:cc, :]) + _dot(o_ref[0, rs, :], wout_ref[cc:, :])
        yield
        out_ref[0, rs, :] = yield from _xattn_rows(h1, gx_ref, wq_ref, kv_ref.at[0], wo_ref)

    _run_staggered([sub(r0) for r0 in range(0, tq, ts)], 1)


def _l0_tail(u, o, h, wout, gx, wq, kv, layer, wo, tq=1024, ts=256):
    bsz, t, d = h.shape
    tile = lambda w: pl.BlockSpec((1, tq, w), lambda b, i: (b, i, 0))
    return pl.pallas_call(
        functools.partial(_l0_tail_kernel, ts=ts),
        grid=(bsz, t // tq),
        in_specs=[tile(u.shape[-1]), tile(o.shape[-1]), tile(d), _resident(wout.shape), _resident((1, d)),
                  _resident(wq.shape), pl.BlockSpec((1, 1, kv.shape[2], 2 * d), lambda b, i: (layer, b, 0, 0)),
                  _resident(wo.shape)],
        out_specs=tile(d),
        out_shape=jax.ShapeDtypeStruct(h.shape, F32),
        compiler_params=_cparams("parallel", "parallel"),
        name="l0_tail",
    )(u, o, h, wout, gx, wq, kv, wo)


def _mlp_kernel(h_ref, g_ref, wu_ref, wd_ref, gf_ref, o_ref, *, hc, final_norm):
    h = h_ref[...]
    xn = _rms(h, g_ref[...]).astype(BF16)
    acc = h
    for c in range(wu_ref.shape[1] // hc):
        a = jnp.maximum(_dot(xn, wu_ref[:, c * hc:(c + 1) * hc]), 0.0)
        acc = acc + _dot(a * a, wd_ref[c * hc:(c + 1) * hc, :])
    if final_norm:
        acc = _rms(acc, gf_ref[...])
    o_ref[...] = acc


def _mlp(h, g, wu, wd, gf, final_norm, tm=512, hc=1024):
    n, d = h.shape
    row = lambda i: (i, 0)
    return pl.pallas_call(
        functools.partial(_mlp_kernel, hc=hc, final_norm=final_norm),
        grid=(n // tm,),
        in_specs=[pl.BlockSpec((tm, d), row), _resident((1, d)), _resident(wu.shape), _resident(wd.shape),
                  _resident((1, d))],
        out_specs=pl.BlockSpec((tm, d), row),
        out_shape=jax.ShapeDtypeStruct((n, d), F32),
        compiler_params=_cparams("parallel"),
        name="mlp",
    )(h, g, wu, wd, gf)


def _rwkv_prep_kernel(h_ref, hp_ref, hn_ref, g_ref, mu_ref, wr_ref, wk_ref, wv_ref, w1_ref, w2_ref, w0_ref,
                      a1_ref, a2_ref, a0_ref, g1_ref, g2_ref, kk_ref, ka_ref, rk_ref, sel_ref, selt_ref,
                      r_out, v_out, kn_out, gate_out, bonus_out, kd_out, cum_out, b_out, *, ts):
    i = pl.program_id(1)
    last = pl.num_programs(1) - 1
    tm = h_ref.shape[1]
    g = g_ref[...]
    ti = lax.broadcasted_iota(jnp.int32, (ts, ts), 0)
    si = lax.broadcasted_iota(jnp.int32, (ts, ts), 1)
    same_chunk = (ti >> CHUNK_SHIFT) == (si >> CHUNK_SHIFT)
    before = (jnp.where(same_chunk, jnp.where(si <= ti, 1.0, 0.0), 0.0).astype(BF16),
              jnp.where(same_chunk, jnp.where(si >= ti, 1.0, 0.0), 0.0).astype(BF16))

    def rows(r0):
        x = _rms(h_ref[0, r0:r0 + ts, :], g)
        if r0 == 0:
            prev_row = _rms(hp_ref[0], g)[SUBLANES - 1:, :] * jnp.where(i > 0, 1.0, 0.0)
        else:
            prev_row = _rms(h_ref[0, r0 - SUBLANES:r0, :], g)[SUBLANES - 1:, :]
        if r0 + ts == tm:
            next_row = _rms(hn_ref[0], g)[0:1, :] * jnp.where(i < last, 1.0, 0.0)
        else:
            next_row = _rms(h_ref[0, r0 + ts:r0 + ts + SUBLANES, :], g)[0:1, :]
        rowid = lax.broadcasted_iota(jnp.int32, x.shape, 0)
        x_prev = jnp.where(rowid == 0, prev_row, pltpu.roll(x, 1, 0))
        x_next = jnp.where(rowid == ts - 1, next_row, pltpu.roll(x, ts - 1, 0))
        hh = 0.5 * (x_prev + x_next) - x
        mix = lambda j: x + hh * mu_ref[j:j + 1, :]
        r = _dot(mix(0), wr_ref[...])
        k = _dot(mix(2), wk_ref[...])
        v = _dot(mix(3), wv_ref[...])
        yield
        gate_in = _dot(mix(5), g1_ref[...])
        lw = _dot(mix(1), w1_ref[...])
        la = _dot(mix(4), a1_ref[...])
        yield
        gate = _dot(_sigmoid(gate_in), g2_ref[...])
        lw = jnp.tanh(lw)
        w_pre = [w0_ref[z:z + 1, :] + _dot(lw, w2_ref[z]) for z in range(2)]
        a_pre = [a0_ref[z:z + 1, :] + _dot(la, a2_ref[z]) for z in range(2)]
        kk = k * kk_ref[...]
        ss = _dot(kk * kk, sel_ref[...])
        yield
        kn = kk * lax.rsqrt(jnp.maximum(_dot(ss, selt_ref[...]), 1e-24))
        kka = k * ka_ref[...]
        kd_sum = jnp.zeros_like(k)
        cum = []
        for z in range(2):
            cum.append(_split_dot(before[z], _sigmoid(w_pre[z]) * (-math.exp(-0.5))))
            rate = _sigmoid(a_pre[z])
            kd = k + kka * (rate - 1.0)
            kd_out[z, 0, r0:r0 + ts, :] = kd.astype(kd_out.dtype)
            b_out[z, 0, r0:r0 + ts, :] = (kn * rate).astype(b_out.dtype)
            kd_sum = kd_sum + kd
        bs = _dot(r * kd_sum * rk_ref[...], sel_ref[...])
        yield
        r_out[0, r0:r0 + ts, :] = r.astype(r_out.dtype)
        v_out[0, r0:r0 + ts, :] = v.astype(v_out.dtype)
        kn_out[0, r0:r0 + ts, :] = kn.astype(kn_out.dtype)
        gate_out[0, r0:r0 + ts, :] = gate.astype(gate_out.dtype)
        bonus_out[0, r0:r0 + ts, :] = (_dot(bs, selt_ref[...]) * v).astype(bonus_out.dtype)
        for z in range(2):
            cum_out[z, 0, r0:r0 + ts, :] = cum[z]

    _run_staggered([rows(r0) for r0 in range(0, tm, ts)], 1)


def _rwkv_prep(h, g, mu, wr, wk, wv, w1, w2, w0, a1, a2, a0, g1, g2, kk, ka, rk, sel, selt, tm=512, ts=256):
    bsz, t, d = h.shape
    nb = tm // SUBLANES
    tile = pl.BlockSpec((1, tm, d), lambda b, i: (b, i, 0))
    tile2 = pl.BlockSpec((2, 1, tm, d), lambda b, i: (0, b, i, 0))
    one = jax.ShapeDtypeStruct((bsz, t, d), BF16)
    two = jax.ShapeDtypeStruct((2, bsz, t, d), BF16)
    consts = [g, mu, wr, wk, wv, w1, w2, w0, a1, a2, a0, g1, g2, kk, ka, rk, sel, selt]
    return pl.pallas_call(
        functools.partial(_rwkv_prep_kernel, ts=ts),
        grid=(bsz, t // tm),
        in_specs=[tile,
                  pl.BlockSpec((1, SUBLANES, d), lambda b, i: (b, jnp.maximum(i * nb - 1, 0), 0)),
                  pl.BlockSpec((1, SUBLANES, d),
                               lambda b, i: (b, jnp.minimum((i + 1) * nb, t // SUBLANES - 1), 0))]
                 + [_resident(c.shape) for c in consts],
        out_specs=[tile, tile, tile, tile, tile, tile2, tile2, tile2],
        out_shape=[one, one, one, one, one, two, jax.ShapeDtypeStruct((2, bsz, t, d), F32), two],
        compiler_params=_cparams("parallel", "parallel"),
        name="l1_rwkv_prep",
    )(h, h, h, *consts)


def _blockdiag(x):
    lane = lax.broadcasted_iota(jnp.int32, x.shape, 1)
    head0 = (lane & (PAIR - 1)) < RWKV_HEAD_DIM
    zero = jnp.zeros_like(x)
    return jnp.concatenate([jnp.where(head0, x, zero), jnp.where(head0, zero, x)], axis=0)


def _chunk_local(r, k, v, kn, cum, b, rev):
    c = CHUNK
    bd = _blockdiag
    row = lax.broadcasted_iota(jnp.int32, (c, PAIR), 0)
    lane = lax.broadcasted_iota(jnp.int32, (c, PAIR), 1)
    s_idx = lane & (c - 1)
    if rev:
        cum_prev = jnp.where(row == c - 1, 0.0, pltpu.roll(cum, c - 1, 0))
        tot = cum[0:1, :]
        strict = s_idx > row
        incl = s_idx >= row
    else:
        cum_prev = jnp.where(row == 0, 0.0, pltpu.roll(cum, 1, 0))
        tot = cum[c - 1:c, :]
        strict = s_idx < row
        incl = s_idx <= row
    w_incl = jnp.exp(cum)
    w_excl = jnp.exp(cum_prev)
    w_inv = jnp.exp(-cum)
    w_tot = jnp.exp(tot)
    w_rest = jnp.exp(tot - cum)
    a_t = -kn * w_excl
    r_t = r * w_incl
    b_t = b * w_inv
    k_t = k * w_inv
    b_h = b * w_rest
    k_h = k * w_rest
    same_blk = (s_idx >> DIAG_SHIFT) == (row >> DIAG_SHIFT)

    sc = _dot_nt(jnp.concatenate([a_t, r_t], axis=0), jnp.concatenate([bd(b_t), bd(k_t)], axis=0))
    yield
    p_ab = jnp.where(strict, sc[:c, :PAIR], 0.0)
    p_ak = jnp.where(strict, sc[:c, PAIR:], 0.0)
    p_rb = jnp.where(incl, sc[c:, :PAIR], 0.0)
    p_rk = jnp.where(incl, sc[c:, PAIR:], 0.0)
    dm = jnp.where(same_blk, p_ab, 0.0)
    em = p_ab - dm
    x2 = _dot(dm, bd(dm))
    av = _dot(p_ak, bd(v))
    yield
    td = jnp.where(s_idx == row, 1.0, 0.0) + dm
    both = _dot(jnp.concatenate([x2, td], axis=0), bd(x2))
    yield
    x4 = both[:c]
    td = td + both[c:]
    both = _dot(jnp.concatenate([x4, td], axis=0), bd(x4))
    yield
    td = td + both[c:]
    td = td + _dot(td, bd(both[:c]))
    yield
    ty = _dot(td, bd(jnp.concatenate([a_t, av, em], axis=1)))
    yield
    au, f1 = ty[:, :2 * PAIR], ty[:, 2 * PAIR:]
    both = _dot(f1, bd(jnp.concatenate([f1, au], axis=1)))
    f2 = both[:, :PAIR]
    au = au + both[:, PAIR:]
    yield
    au = au + _dot(f2, bd(au))
    yield
    rhs = jnp.concatenate([bd(au), jnp.concatenate([jnp.zeros((PAIR, PAIR), F32), bd(v)], axis=1)], axis=0)
    ry = _dot(jnp.concatenate([p_rb, p_rk], axis=1), rhs)
    rhs2 = jnp.concatenate([au, jnp.concatenate([jnp.zeros((c, PAIR), F32), v], axis=1)], axis=0)
    mg = _dot(jnp.concatenate([b_h, k_h], axis=0).T, rhs2)
    yield
    rbar = r_t + ry[:, :PAIR]
    yloc = ry[:, PAIR:]
    r2 = lax.broadcasted_iota(jnp.int32, (PAIR, PAIR), 0)
    l2 = lax.broadcasted_iota(jnp.int32, (PAIR, PAIR), 1)
    same_head = (r2 >> HEAD_SHIFT) == (l2 >> HEAD_SHIFT)
    m = jnp.where(same_head, mg[:, :PAIR], 0.0) + jnp.where(r2 == l2, w_tot, 0.0)
    gg = jnp.where(same_head, mg[:, PAIR:], 0.0)
    return rbar, yloc, m, gg


def _scan_kernel(r_ref, v_ref, kn_ref, kd_ref, cum_ref, b_ref, yf_ref, yb_ref, ds_ref, loc_a, loc_b, *, unroll):
    nc = r_ref.shape[1] // CHUNK
    groups = nc // unroll
    total = (r_ref.shape[2] // PAIR) * groups
    y_refs = (yf_ref, yb_ref)
    c = CHUNK
    ds_ref[...] = jnp.zeros(ds_ref.shape, F32)
    loc_b[...] = jnp.zeros(loc_b.shape, F32)

    def places(gidx):
        grp = gidx % groups
        lanes = pl.ds(pl.multiple_of((gidx // groups) * PAIR, PAIR), PAIR)
        out = []
        for u in range(unroll):
            for z in range(2):
                cidx = grp * unroll + u
                if z == 1:
                    cidx = nc - 1 - cidx
                out.append((z, pl.ds(pl.multiple_of(cidx * c, c), c), lanes))
        return out

    def local_terms(gidx, loc_ref):
        f32 = lambda ref, *idx: ref[idx].astype(F32)
        gens = [_chunk_local(f32(r_ref, 0, sl, ln), f32(kd_ref, z, 0, sl, ln), f32(v_ref, 0, sl, ln),
                             f32(kn_ref, 0, sl, ln), cum_ref[z, 0, sl, ln], f32(b_ref, z, 0, sl, ln),
                             rev=(z == 1))
                for z, sl, ln in places(gidx)]

        def park(j, gen):
            rbar, yloc, m, gg = yield from gen
            loc_ref[j, 0:c, :] = rbar
            loc_ref[j, c:2 * c, :] = yloc
            loc_ref[j, 2 * c:2 * c + PAIR, :] = m
            loc_ref[j, 2 * c + PAIR:, :] = gg

        return [park(j, gen) for j, gen in enumerate(gens)]

    def recurrence(gidx, loc_ref):
        keep = jnp.where(gidx % groups == 0, 0.0, 1.0)
        ds = [ds_ref[0] * keep, ds_ref[1] * keep]
        for j, (z, sl, ln) in enumerate(places(gidx)):
            both = _dot(jnp.concatenate([loc_ref[j, 0:c, :], loc_ref[j, 2 * c:2 * c + PAIR, :]], axis=0), ds[z])
            y_refs[z][0, sl, ln] = (both[:c] + loc_ref[j, c:2 * c, :]).astype(y_refs[z].dtype)
            ds[z] = both[c:] + loc_ref[j, 2 * c + PAIR:, :]
            if z == 1:
                yield
        ds_ref[0] = ds[0]
        ds_ref[1] = ds[1]

    def body(it, carry):
        first = 2 * it
        _run_staggered([recurrence(jnp.maximum(first - 1, 0), loc_b)] + local_terms(first, loc_a), 0)
        _run_staggered([recurrence(first, loc_a)] + local_terms(first + 1, loc_b), 0)
        return carry

    lax.fori_loop(0, total // 2, body, 0)
    _run_staggered([recurrence(total - 1, loc_b)], 0)


def _wkv7_scan(r, v, kn, kd, cum, b, unroll=8, pairs=2):
    bsz, t, d = r.shape
    nc = t // CHUNK
    unroll = min(unroll, nc // 2)
    assert nc % (2 * unroll) == 0, "the scan kernel takes chunk groups in pairs"
    one = pl.BlockSpec((1, t, pairs * PAIR), lambda bb, p: (bb, 0, p))
    two = pl.BlockSpec((2, 1, t, pairs * PAIR), lambda bb, p: (0, bb, 0, p))
    out = jax.ShapeDtypeStruct((bsz, t, d), BF16)
    return pl.pallas_call(
        functools.partial(_scan_kernel, unroll=unroll),
        grid=(bsz, d // (pairs * PAIR)),
        in_specs=[one, one, one, two, two, two],
        out_specs=[one, one],
        out_shape=[out, out],
        scratch_shapes=[pltpu.VMEM((2, PAIR, PAIR), F32)]
                       + [pltpu.VMEM((2 * unroll, 2 * CHUNK + 2 * PAIR, PAIR), F32)] * 2,
        compiler_params=_cparams("parallel", "parallel"),
        name="l1_wkv7_scan",
    )(r, v, kn, kd, cum, b)


def _l1_tail_kernel(yf_ref, yb_ref, bonus_ref, gate_ref, h_ref, lg_ref, lb_ref, wo_ref, sel_ref, selt_ref,
                    gx_ref, wq_ref, kv_ref, wxo_ref, out_ref, *, ts):
    tq = h_ref.shape[1]
    inv_n = 1.0 / RWKV_HEAD_DIM

    def sub(r0):
        rs = slice(r0, r0 + ts)
        y = yf_ref[0, rs, :] + yb_ref[0, rs, :]
        mu = _dot(y, sel_ref[...])
        yield
        yc = y - _dot(mu, selt_ref[...]) * inv_n
        var = _dot(yc * yc, sel_ref[...])
        yield
        yn = yc * lax.rsqrt(_dot(var, selt_ref[...]) * inv_n + GN_EPS) * lg_ref[...] + lb_ref[...]
        h1 = h_ref[0, rs, :] + _dot((yn + bonus_ref[0, rs, :]) * gate_ref[0, rs, :], wo_ref[...])
        yield
        out_ref[0, rs, :] = yield from _xattn_rows(h1, gx_ref, wq_ref, kv_ref.at[0], wxo_ref)

    _run_staggered([sub(r0) for r0 in range(0, tq, ts)], 1)


def _l1_tail(yf, yb, bonus, gate, h, lg, lb, wo, sel, selt, gx, wq, kv, layer, wxo, tq=1024, ts=256):
    bsz, t, d = h.shape
    tile = pl.BlockSpec((1, tq, d), lambda b, i: (b, i, 0))
    return pl.pallas_call(
        functools.partial(_l1_tail_kernel, ts=ts),
        grid=(bsz, t // tq),
        in_specs=[tile, tile, tile, tile, tile, _resident((1, d)), _resident((1, d)), _resident(wo.shape),
                  _resident(sel.shape), _resident(selt.shape), _resident((1, d)), _resident(wq.shape),
                  pl.BlockSpec((1, 1, kv.shape[2], 2 * d), lambda b, i: (layer, b, 0, 0)), _resident(wxo.shape)],
        out_specs=tile,
        out_shape=jax.ShapeDtypeStruct(h.shape, F32),
        compiler_params=_cparams("parallel", "parallel"),
        name="l1_tail",
    )(yf, yb, bonus, gate, h, lg, lb, wo, sel, selt, gx, wq, kv, wxo)


def _pad_lora_out(w2):
    zero = jnp.zeros_like(w2[0])
    return jnp.stack([jnp.concatenate([w2[0], zero], axis=0), jnp.concatenate([zero, w2[1]], axis=0)])


def kernel(x, mem, rel_bias_table, norm_mix, norm_xattn, norm_mem, norm_ffn, norm_final, ab_w_in, ab_w_out, conv_w, conv_b, conv_ln_g, conv_ln_b, diff_lq1, diff_lk1, diff_lq2, diff_lk2, diff_subln_g, rwkv_mu, rwkv_w_r, rwkv_w_k, rwkv_w_v, rwkv_w_o, rwkv_w0, rwkv_w1, rwkv_w2, rwkv_a0, rwkv_a1, rwkv_a2, rwkv_g1, rwkv_g2, rwkv_k_k, rwkv_k_a, rwkv_r_k, rwkv_ln_g, rwkv_ln_b, xattn_w_q, xattn_w_kv, xattn_w_o, ffn_w_up, ffn_w_down):
    bsz, t, d = x.shape
    n = bsz * t
    depth = norm_mix.shape[0]
    n_mem = mem.shape[1]
    cc = conv_w.shape[-1]
    qk = DIFF_HEADS * 2 * DIFF_HEAD_DIM
    vw = ab_w_in.shape[-1] - 2 * cc - 2 * qk
    bf = lambda w: w.astype(BF16)
    row = lambda w: w.reshape(1, -1)

    heads = d // RWKV_HEAD_DIM
    head_of = jnp.arange(d, dtype=jnp.int32) // RWKV_HEAD_DIM
    assert heads <= LANES
    sel = (head_of[:, None] == jnp.arange(LANES, dtype=jnp.int32)[None, :]).astype(BF16)
    selt = sel.T

    h = x.reshape(n, d)
    mem2 = mem.reshape(bsz * n_mem, d)
    kv = _kv_proj(mem2, norm_mem.reshape(depth, 1, d), bf(xattn_w_kv)).reshape(depth, bsz, n_mem, 2 * d)
    for i in range(depth):
        j = i // 2
        if i % 2 == 0:
            lam_init = 0.8 - 0.6 * math.exp(-0.3 * i)
            h3 = h.reshape(bsz, t, d)
            u, q, k, v = _inproj(h3, row(norm_mix[i]), bf(ab_w_in[j]), conv_w[j], row(conv_b[j]),
                                 row(conv_ln_g[j]), row(conv_ln_b[j]), cc, qk, vw)
            lq = jnp.stack([diff_lq1[j], diff_lk1[j], diff_lq2[j], diff_lk2[j]])
            o = _diff_attention(q, k, v, rel_bias_table.reshape(-1), lq, row(diff_subln_g[j]), lam_init)
            h = _l0_tail(u, o, h3, bf(ab_w_out[j]), row(norm_xattn[i]), bf(xattn_w_q[i]), kv, i,
                         bf(xattn_w_o[i])).reshape(n, d)
        else:
            h3 = h.reshape(bsz, t, d)
            w1 = bf(jnp.concatenate([rwkv_w1[j, 0], rwkv_w1[j, 1]], axis=1))
            a1 = bf(jnp.concatenate([rwkv_a1[j, 0], rwkv_a1[j, 1]], axis=1))
            r, v, kn, gate, bonus, kd, cum, b = _rwkv_prep(
                h3, row(norm_mix[i]), rwkv_mu[j], bf(rwkv_w_r[j]), bf(rwkv_w_k[j]), bf(rwkv_w_v[j]),
                w1, bf(_pad_lora_out(rwkv_w2[j])), rwkv_w0[j], a1, bf(_pad_lora_out(rwkv_a2[j])), rwkv_a0[j],
                bf(rwkv_g1[j]), bf(rwkv_g2[j]), row(rwkv_k_k[j]), row(rwkv_k_a[j]), row(rwkv_r_k[j]),
                sel, selt)
            yf, yb = _wkv7_scan(r, v, kn, kd, cum, b)
            h = _l1_tail(yf, yb, bonus, gate, h3, row(rwkv_ln_g[j]), row(rwkv_ln_b[j]), bf(rwkv_w_o[j]), sel, selt,
                         row(norm_xattn[i]), bf(xattn_w_q[i]), kv, i, bf(xattn_w_o[i])).reshape(n, d)
        h = _mlp(h, row(norm_ffn[i]), bf(ffn_w_up[i]), bf(ffn_w_down[i]), row(norm_final),
                 final_norm=(i == depth - 1))
    return h.reshape(bsz, t, d)
```
